```python
import jax
import jax.numpy as jnp
from jax import lax
import numpy as np

D_MODEL = 1024
BATCH = 32
SEQ = 2048
DEPTH = 2

GRID_W = 64
CTX_LEN = 256
HEAD_DIM = 64
LRU_WIDTH = D_MODEL
LRU_BLOCKS = 8
LRU_BW = LRU_WIDTH // LRU_BLOCKS
CONV_W = 4
LRU_C = 8.0
NA_HEADS = D_MODEL // 128
NA_KH_MAX = 8
NA_KW = 16
GQA_HEADS = D_MODEL // 128
GQA_KV_HEADS = 2
GQA_GROUP = GQA_HEADS // GQA_KV_HEADS
Q_BLOCK = 128
ROPE_BASE = 10000.0
N_EXPERTS = 16
EXPERT_FF = D_MODEL
EC_CAPACITY = 2
N_BRANCH = 3
EPS = 1e-6
NEG_INF = -1e30

NA_WIDTH = NA_HEADS * HEAD_DIM
GQA_Q_WIDTH = GQA_HEADS * HEAD_DIM
GQA_KV_WIDTH = GQA_KV_HEADS * HEAD_DIM
IN_SPLITS = (LRU_WIDTH, LRU_WIDTH, NA_WIDTH, NA_WIDTH, NA_WIDTH, GQA_Q_WIDTH, GQA_KV_WIDTH, GQA_KV_WIDTH, N_BRANCH * D_MODEL)
IN_WIDTH = sum(IN_SPLITS)
IN_OFFSETS = tuple(int(o) for o in np.cumsum(IN_SPLITS)[:-1])
CTX_KV_PARTS = (0, 3, 4, 6, 7)

kernel_name = 'hybrid_rglru_natten_gqa_ecmoe_dit'

F32 = jnp.float32


def rmsnorm(x, g):
    xf = x.astype(F32)
    y = xf * lax.rsqrt(jnp.mean(xf * xf, axis=-1, keepdims=True) + EPS)
    return (y * g.astype(F32)).astype(x.dtype)


def adaln(cond, w_mod, b_mod):
    m = jax.nn.silu(cond) @ w_mod + b_mod
    return jnp.split(m[..., None, :], 6, axis=-1)


def modulate(x, g, shift, scale):
    return rmsnorm(x, g) * (1.0 + scale) + shift


def heads(t):
    return t.reshape(t.shape[0], t.shape[1], -1, HEAD_DIM)


def group(q):
    return q.reshape(q.shape[0], q.shape[1], GQA_KV_HEADS, GQA_GROUP, HEAD_DIM)


def attend(q, k, v):
    s = jnp.einsum('bqhgd,bkhd->bhgqk', q, k).astype(F32) * (HEAD_DIM ** -0.5)
    p = jax.nn.softmax(s, axis=-1).astype(v.dtype)
    return jnp.einsum('bhgqk,bkhd->bqhgd', p, v)


def axial_rope(n_tokens):
    t = jnp.arange(n_tokens)
    pos = jnp.stack([t // GRID_W, t % GRID_W], axis=-1).astype(F32)
    n_freq = HEAD_DIM // 4
    inv_freq = ROPE_BASE ** (-jnp.arange(n_freq, dtype=F32) / n_freq)
    ang = (pos[:, :, None] * inv_freq).reshape(n_tokens, 2 * n_freq)
    return jnp.cos(ang)[:, None, :], jnp.sin(ang)[:, None, :]


def apply_rope(x, cos, sin):
    xf = x.astype(F32)
    x1, x2 = jnp.split(xf, 2, axis=-1)
    return jnp.concatenate([x1 * cos - x2 * sin, x1 * sin + x2 * cos], axis=-1).astype(x.dtype)


def dwconv_centred(x, w, b):
    T = x.shape[1]
    left = CONV_W // 2
    xp = jnp.pad(x, ((0, 0), (left, CONV_W - 1 - left), (0, 0)))
    out = b
    for k in range(CONV_W):
        out = out + xp[:, k:k + T] * w[k]
    return out


def rglru_coeffs(u, wa, ba, wx, bx, lam):
    B_, T, _ = u.shape
    ub = u.reshape(B_, T, LRU_BLOCKS, LRU_BW)
    r = jax.nn.sigmoid(jnp.einsum('btnc,ncd->btnd', ub, wa.astype(F32)).reshape(B_, T, LRU_WIDTH) + ba.astype(F32))
    i = jax.nn.sigmoid(jnp.einsum('btnc,ncd->btnd', ub, wx.astype(F32)).reshape(B_, T, LRU_WIDTH) + bx.astype(F32))
    log_a = -LRU_C * r * jax.nn.softplus(-lam.astype(F32))
    a = jnp.exp(log_a)
    b = jnp.sqrt(-jnp.expm1(2.0 * log_a)) * (i * u)
    return a, b


def linear_scan(a, b, h0, reverse):
    if h0 is not None:
        first = a.shape[1] - 1 if reverse else 0
        b = b.at[:, first].add(a[:, first] * h0)

    def comb(left, right):
        a_l, b_l = left
        a_r, b_r = right
        return a_l * a_r, a_r * b_l + b_r

    _, h = lax.associative_scan(comb, (a, b), reverse=reverse, axis=1)
    return h


def rglru_branch(xl, xc, gl, gc, p, need_ctx):
    ul = dwconv_centred(xl, p['conv_w'], p['conv_b']).astype(F32)
    uc = dwconv_centred(xc, p['conv_w'], p['conv_b']).astype(F32)
    hl = 0.0
    hc = 0.0
    for d, rev in enumerate((False, True)):
        dp = (p['lru_wa'][d], p['lru_ba'][d], p['lru_wx'][d], p['lru_bx'][d], p['lru_lam'][d])
        ac, bc = rglru_coeffs(uc, *dp)
        hcd = linear_scan(ac, bc, None, rev)
        h_fin = hcd[:, 0] if rev else hcd[:, -1]
        al, bl = rglru_coeffs(ul, *dp)
        hl = hl + linear_scan(al, bl, h_fin, rev)
        if need_ctx:
            hc = hc + hcd
    yl = (hl * jax.nn.gelu(gl.astype(F32))).astype(xl.dtype)
    yc = (hc * jax.nn.gelu(gc.astype(F32))).astype(xc.dtype) if need_ctx else None
    return yl, yc


def neighbourhood_attention(q, k, v, kc, vc, rpb):
    B_, S, H, dh = q.shape
    rows = S // GRID_W
    kh = min(NA_KH_MAX, rows)
    qg = q.reshape(B_, rows, GRID_W, H, dh)
    kg = k.reshape(B_, rows, GRID_W, H, dh)
    vg = v.reshape(B_, rows, GRID_W, H, dh)
    col = jnp.arange(GRID_W)
    cs = jnp.clip(col - NA_KW // 2, 0, GRID_W - NA_KW)
    col_mask = (col[None, :] >= cs[:, None]) & (col[None, :] < cs[:, None] + NA_KW)
    dcol = jnp.clip(col[None, :] - col[:, None], -(NA_KW - 1), NA_KW - 1) + NA_KW - 1
    n_lat = kh * GRID_W

    def row_fn(args):
        r, qr = args
        rs = jnp.clip(r - kh // 2, 0, rows - kh)
        kb = lax.dynamic_slice_in_dim(kg, rs, kh, axis=1)
        vb = lax.dynamic_slice_in_dim(vg, rs, kh, axis=1)
        drow = jnp.arange(kh) + rs - r + NA_KH_MAX - 1
        bias = rpb[:, drow[None, :, None], dcol[:, None, :]].astype(F32)
        s_lat = jnp.einsum('bqhd,bjkhd->bhqjk', qr, kb).astype(F32) * (dh ** -0.5) + bias
        s_lat = jnp.where(col_mask[:, None, :], s_lat, NEG_INF).reshape(B_, H, GRID_W, n_lat)
        s_ctx = jnp.einsum('bqhd,bchd->bhqc', qr, kc).astype(F32) * (dh ** -0.5)
        pr = jax.nn.softmax(jnp.concatenate([s_lat, s_ctx], axis=-1), axis=-1).astype(v.dtype)
        p_lat = pr[..., :n_lat].reshape(B_, H, GRID_W, kh, GRID_W)
        o = jnp.einsum('bhqjk,bjkhd->bqhd', p_lat, vb)
        return o + jnp.einsum('bhqc,bchd->bqhd', pr[..., n_lat:], vc)

    out = lax.map(row_fn, (jnp.arange(rows), jnp.moveaxis(qg, 1, 0)))
    return jnp.moveaxis(out, 0, 1).reshape(B_, S, H * dh)


def gqa_blocked(q, k, v, kc, vc):
    B_, S = q.shape[:2]
    kall = jnp.concatenate([kc, k], axis=1)
    vall = jnp.concatenate([vc, v], axis=1)
    qb = jnp.moveaxis(q.reshape(B_, S // Q_BLOCK, Q_BLOCK, GQA_KV_HEADS, GQA_GROUP, HEAD_DIM), 1, 0)
    o = lax.map(lambda qi: attend(qi, kall, vall), qb)
    return jnp.moveaxis(o, 0, 1).reshape(B_, S, GQA_Q_WIDTH)


def merge_branches(ya, yb, yc, gate_logits, p):
    ga, gb, gc = jnp.split(jax.nn.sigmoid(gate_logits.astype(F32)).astype(ya.dtype), N_BRANCH, axis=-1)
    m = ga * (ya @ p['w_o_a']) + gb * (yb @ p['w_o_b']) + gc * (yc @ p['w_o_c'])
    return m @ p['w_out']


def parallel_mixer(hl, hc, p, need_ctx):
    lat = jnp.split(hl @ p['w_in'], IN_OFFSETS, axis=-1)
    if need_ctx:
        ctx = jnp.split(hc @ p['w_in'], IN_OFFSETS, axis=-1)
    else:
        w_parts = jnp.split(p['w_in'], IN_OFFSETS, axis=1)
        ctx = [hc @ w if i in CTX_KV_PARTS else None for i, w in enumerate(w_parts)]
    ya_l, ya_c = rglru_branch(lat[0], ctx[0], lat[1], ctx[1], p, need_ctx)
    kb_c, vb_c = heads(ctx[3]), heads(ctx[4])
    yb_l = neighbourhood_attention(heads(lat[2]), heads(lat[3]), heads(lat[4]), kb_c, vb_c, p['na_rpb'])
    cos, sin = axial_rope(hl.shape[1])
    qc_l = apply_rope(rmsnorm(heads(lat[5]), p['qn_g']), cos, sin)
    kc_l = apply_rope(rmsnorm(heads(lat[6]), p['kn_g']), cos, sin)
    kc_c = rmsnorm(heads(ctx[6]), p['kn_g'])
    vc_c = heads(ctx[7])
    yc_l = gqa_blocked(group(qc_l), kc_l, heads(lat[7]), kc_c, vc_c)
    out_l = merge_branches(ya_l, yb_l, yc_l, lat[8], p)
    out_c = None
    if need_ctx:
        B_, L = hc.shape[:2]
        yb_c = attend(heads(ctx[2])[:, :, :, None], kb_c, vb_c).reshape(B_, L, NA_WIDTH)
        yc_c = attend(group(rmsnorm(heads(ctx[5]), p['qn_g'])), kc_c, vc_c).reshape(B_, L, GQA_Q_WIDTH)
        out_c = merge_branches(ya_c, yb_c, yc_c, ctx[8], p)
    return out_l, out_c


def expert_choice_ffn(h, p):
    n, d = h.shape[1], h.shape[2]
    cap = EC_CAPACITY * n // N_EXPERTS
    logits = jnp.einsum('bnd,de->ben', h, p['w_router']).astype(F32)
    aff = jax.nn.softmax(logits, axis=1)
    gate, idx = lax.top_k(aff, cap)
    hs = jax.vmap(lambda hb, ib: hb[ib])(h, idx)
    u = jax.nn.silu(jnp.einsum('becd,edf->becf', hs, p['w_gate'])) * jnp.einsum('becd,edf->becf', hs, p['w_up'])
    y = jnp.einsum('becf,efd->becd', u, p['w_down']) * gate[..., None].astype(h.dtype)
    return jax.vmap(lambda yb, ib: jnp.zeros((n, d), yb.dtype).at[ib.reshape(-1)].add(yb.reshape(-1, d)))(y, idx)


def hybrid_layer(xl, xc, c, c_ctx, p, need_ctx):
    sh1, sc1, g1, sh2, sc2, g2 = adaln(c, p['w_mod'], p['b_mod'])
    csh1, csc1, cg1, csh2, csc2, cg2 = adaln(c_ctx, p['w_mod'], p['b_mod'])
    ng = p['norm_g']
    hl = modulate(xl, ng[0], sh1, sc1)
    hc = modulate(xc, ng[0], csh1, csc1)
    yl, yc = parallel_mixer(hl, hc, p, need_ctx)
    xl = xl + g1 * rmsnorm(yl, ng[1])
    hl = modulate(xl, ng[2], sh2, sc2)
    xl = xl + g2 * rmsnorm(expert_choice_ffn(hl, p), ng[3])
    if need_ctx:
        xc = xc + cg1 * rmsnorm(yc, ng[1])
        hc = modulate(xc, ng[2], csh2, csc2)
        xc = xc + cg2 * rmsnorm(expert_choice_ffn(hc, p), ng[3])
    return xl, xc


def setup_inputs(seed: int = 0) -> dict:
    key = jax.random.key(seed)
    keys = iter(jax.random.split(key, 32))

    def dense(shape, fan_in, mult=1.0):
        return jax.random.normal(next(keys), shape, F32) * (mult * fan_in ** -0.5)

    def gain(shape):
        return 1.0 + 0.05 * jax.random.normal(next(keys), shape, F32)

    def small(shape, s=0.02):
        return s * jax.random.normal(next(keys), shape, F32)

    D = D_MODEL
    a0 = jax.random.uniform(next(keys), (DEPTH, 2, LRU_WIDTH), F32, 0.9, 0.999)
    s0 = a0 ** (1.0 / LRU_C)
    lru_lam = jnp.log(s0) - jnp.log1p(-s0)
    return {
        'x': jax.random.normal(next(keys), (BATCH, SEQ, D), F32),
        'c': jax.random.normal(next(keys), (BATCH, D), F32),
        'ctx': jax.random.normal(next(keys), (BATCH, CTX_LEN, D), F32),
        'c_ctx': jax.random.normal(next(keys), (D,), F32),
        'w_mod': dense((DEPTH, D, 6 * D), D, 0.5),
        'b_mod': small((DEPTH, 6 * D)),
        'norm_g': gain((DEPTH, 4, D)),
        'w_in': dense((DEPTH, D, IN_WIDTH), D),
        'conv_w': dense((DEPTH, CONV_W, LRU_WIDTH), CONV_W),
        'conv_b': small((DEPTH, LRU_WIDTH)),
        'lru_wa': dense((DEPTH, 2, LRU_BLOCKS, LRU_BW, LRU_BW), LRU_BW),
        'lru_ba': small((DEPTH, 2, LRU_WIDTH)),
        'lru_wx': dense((DEPTH, 2, LRU_BLOCKS, LRU_BW, LRU_BW), LRU_BW),
        'lru_bx': small((DEPTH, 2, LRU_WIDTH)),
        'lru_lam': lru_lam,
        'na_rpb': small((DEPTH, NA_HEADS, 2 * NA_KH_MAX - 1, 2 * NA_KW - 1), 0.1),
        'qn_g': gain((DEPTH, HEAD_DIM)),
        'kn_g': gain((DEPTH, HEAD_DIM)),
        'w_o_a': dense((DEPTH, LRU_WIDTH, D), LRU_WIDTH),
        'w_o_b': dense((DEPTH, NA_WIDTH, D), NA_WIDTH),
        'w_o_c': dense((DEPTH, GQA_Q_WIDTH, D), GQA_Q_WIDTH),
        'w_out': dense((DEPTH, D, D), D),
        'w_router': dense((DEPTH, D, N_EXPERTS), D),
        'w_gate': dense((DEPTH, N_EXPERTS, D, EXPERT_FF), D),
        'w_up': dense((DEPTH, N_EXPERTS, D, EXPERT_FF), D),
        'w_down': dense((DEPTH, N_EXPERTS, EXPERT_FF, D), EXPERT_FF),
    }


def reference(x, c, ctx, c_ctx, w_mod, b_mod, norm_g, w_in, conv_w, conv_b, lru_wa, lru_ba, lru_wx, lru_bx, lru_lam, na_rpb, qn_g, kn_g, w_o_a, w_o_b, w_o_c, w_out, w_router, w_gate, w_up, w_down):
    xl, xc = x, ctx
    for l in range(DEPTH):
        p = {
            'w_mod': w_mod[l], 'b_mod': b_mod[l], 'norm_g': norm_g[l], 'w_in': w_in[l],
            'conv_w': conv_w[l], 'conv_b': conv_b[l],
            'lru_wa': lru_wa[l], 'lru_ba': lru_ba[l], 'lru_wx': lru_wx[l], 'lru_bx': lru_bx[l], 'lru_lam': lru_lam[l],
            'na_rpb': na_rpb[l], 'qn_g': qn_g[l], 'kn_g': kn_g[l],
            'w_o_a': w_o_a[l], 'w_o_b': w_o_b[l], 'w_o_c': w_o_c[l], 'w_out': w_out[l],
            'w_router': w_router[l], 'w_gate': w_gate[l], 'w_up': w_up[l], 'w_down': w_down[l],
        }
        xl, xc = hybrid_layer(xl, xc, c, c_ctx, p, l < DEPTH - 1)
    return xl
```

```python
import functools

import jax
import jax.numpy as jnp
import numpy as np
from jax import lax
from jax.experimental import pallas as pl
from jax.experimental.pallas import tpu as pltpu

F32 = jnp.float32
BF16 = jnp.bfloat16

GRID_W = 64
HEAD_DIM = 64
LRU_BLOCKS = 8
CONV_W = 4
LRU_C = 8.0
NA_KH_MAX = 8
NA_KW = 16
GQA_KV_HEADS = 2
ROPE_BASE = 10000.0
N_EXPERTS = 16
EC_CAPACITY = 2
N_BRANCH = 3
EPS = 1e-6
NEG_INF = -1e30

LANES = 128
SUBLANES = 8
BF16_ROWS = 16
VMEM_LIMIT_BYTES = 56 * 1024 * 1024

_NT = (((1,), (1,)), ((), ()))


def _cp(*sem):
    return pltpu.CompilerParams(dimension_semantics=sem, vmem_limit_bytes=VMEM_LIMIT_BYTES)


def _rms(x, g):
    return x * lax.rsqrt(jnp.mean(x * x, axis=-1, keepdims=True) + EPS) * g


def _adaln_kernel(c_ref, w_ref, b_ref, o_ref):
    c = c_ref[...]
    s = (c * jax.nn.sigmoid(c)).astype(BF16)
    o_ref[...] = jnp.dot(s, w_ref[...].astype(BF16), preferred_element_type=F32) + b_ref[...]


def _adaln(cond, w_mod, b_mod):
    r, d = cond.shape
    n = w_mod.shape[1]
    tn = n // 4
    return pl.pallas_call(
        _adaln_kernel,
        out_shape=jax.ShapeDtypeStruct((r, n), F32),
        grid=(n // tn,),
        in_specs=[
            pl.BlockSpec((r, d), lambda j: (0, 0)),
            pl.BlockSpec((d, tn), lambda j: (0, j)),
            pl.BlockSpec((1, tn), lambda j: (0, j)),
        ],
        out_specs=pl.BlockSpec((r, tn), lambda j: (0, j)),
        compiler_params=_cp("parallel"),
        name="adaln",
    )(cond, w_mod, b_mod.reshape(1, n))


def _inproj_kernel(x_ref, sh_ref, sc_ref, g_ref, w_ref, *o_refs, splits):
    h = (_rms(x_ref[...], g_ref[...]) * (1.0 + sc_ref[...]) + sh_ref[...]).astype(BF16)
    off = 0
    for o_ref, width in zip(o_refs, splits):
        o_ref[...] = jnp.dot(h, w_ref[:, off:off + width], preferred_element_type=F32).astype(o_ref.dtype)
        off += width


def _inproj(x, shift, scale, gain, w, splits, mod_row):
    b, t, d = x.shape
    tm = min(t, 512)
    n = w.shape[1]
    if mod_row is None:
        mod_map = lambda bi, i: (bi, 0, 0)
    else:
        mod_map = lambda bi, i: (mod_row, 0, 0)
    return pl.pallas_call(
        functools.partial(_inproj_kernel, splits=splits),
        out_shape=[jax.ShapeDtypeStruct((b, t, s), BF16) for s in splits],
        grid=(b, t // tm),
        in_specs=[
            pl.BlockSpec((None, tm, d), lambda bi, i: (bi, i, 0)),
            pl.BlockSpec((None, 1, d), mod_map),
            pl.BlockSpec((None, 1, d), mod_map),
            pl.BlockSpec((1, d), lambda bi, i: (0, 0)),
            pl.BlockSpec((d, n), lambda bi, i: (0, 0)),
        ],
        out_specs=[pl.BlockSpec((None, tm, s), lambda bi, i: (bi, i, 0)) for s in splits],
        compiler_params=_cp("parallel", "parallel"),
        name="inproj",
    )(x, shift, scale, gain.reshape(1, d), w)


def _lru_kernel(x_ref, xp_ref, xn_ref, h0_ref, cw_ref, cb_ref, wa_ref, ba_ref, wx_ref, bx_ref, lam_ref,
                y_ref, hfin_ref, xe_scr, a_scr, b_scr, st_scr, *, tc, nchunks, nb):
    d = pl.program_id(1)
    j = pl.program_id(2)
    c = jnp.where(d == 0, j, nchunks - 1 - j)
    halo = BF16_ROWS
    left = CONV_W // 2

    xe_scr[:, halo:halo + tc, :] = x_ref[...].astype(F32)
    xe_scr[:, 0:halo, :] = jnp.where(c > 0, xp_ref[...].astype(F32), 0.0)
    xe_scr[:, halo + tc:2 * halo + tc, :] = jnp.where(c < nchunks - 1, xn_ref[...].astype(F32), 0.0)

    @pl.when(j == 0)
    def _():
        for cb in range(LRU_BLOCKS):
            st_scr[cb] = h0_ref[:, cb * LANES:(cb + 1) * LANES]

    c_lam = -LRU_C * jax.nn.softplus(-lam_ref[...])
    for cb in range(LRU_BLOCKS):
        sl = slice(cb * LANES, (cb + 1) * LANES)
        u = cb_ref[:, sl]
        for k in range(CONV_W):
            u = u + xe_scr[:, pl.ds(halo + k - left, tc), sl] * cw_ref[k:k + 1, sl]
        u = u.reshape(nb * tc, LANES)
        ub = u.astype(BF16)
        r = jax.nn.sigmoid(jnp.dot(ub, wa_ref[cb], preferred_element_type=F32) + ba_ref[:, sl])
        i = jax.nn.sigmoid(jnp.dot(ub, wx_ref[cb], preferred_element_type=F32) + bx_ref[:, sl])
        log_a = c_lam[:, sl] * r
        a = jnp.exp(log_a)
        a_scr[cb] = a
        b_scr[cb] = jnp.sqrt(-jnp.tanh(log_a) * (a * a + 1.0)) * (i * u)

    def step(t, hs):
        tt = jnp.where(d == 0, t, tc - 1 - t)
        new = []
        for cb in range(LRU_BLOCKS):
            a = a_scr[cb, pl.ds(tt, nb, stride=tc), :]
            b = b_scr[cb, pl.ds(tt, nb, stride=tc), :]
            h = a * hs[cb] + b
            b_scr[cb, pl.ds(tt, nb, stride=tc), :] = h
            new.append(h)
        return tuple(new)

    hs = lax.fori_loop(0, tc, step, tuple(st_scr[cb] for cb in range(LRU_BLOCKS)), unroll=4)
    for cb in range(LRU_BLOCKS):
        sl = slice(cb * LANES, (cb + 1) * LANES)
        st_scr[cb] = hs[cb]
        y_ref[:, :, sl] = b_scr[cb].reshape(nb, tc, LANES).astype(y_ref.dtype)

    @pl.when(j == nchunks - 1)
    def _():
        for cb in range(LRU_BLOCKS):
            hfin_ref[:, cb * LANES:(cb + 1) * LANES] = hs[cb]


def _lru(x, h0, lp):
    b, t, w = x.shape
    nb = SUBLANES
    tc = min(t, 128)
    nchunks = t // tc
    hb = tc // BF16_ROWS
    nhb = t // BF16_ROWS

    def chunk(di, j):
        return jnp.where(di == 0, j, nchunks - 1 - j)

    vec = lambda g, di, j: (0, 0)
    dvec = lambda g, di, j: (di, 0, 0)
    return pl.pallas_call(
        functools.partial(_lru_kernel, tc=tc, nchunks=nchunks, nb=nb),
        out_shape=[jax.ShapeDtypeStruct((2, b, t, w), BF16), jax.ShapeDtypeStruct((2, b, w), F32)],
        grid=(b // nb, 2, nchunks),
        in_specs=[
            pl.BlockSpec((nb, tc, w), lambda g, di, j: (g, chunk(di, j), 0)),
            pl.BlockSpec((nb, BF16_ROWS, w), lambda g, di, j: (g, jnp.maximum(chunk(di, j) * hb - 1, 0), 0)),
            pl.BlockSpec((nb, BF16_ROWS, w), lambda g, di, j: (g, jnp.minimum((chunk(di, j) + 1) * hb, nhb - 1), 0)),
            pl.BlockSpec((None, nb, w), lambda g, di, j: (di, g, 0)),
            pl.BlockSpec((CONV_W, w), vec),
            pl.BlockSpec((1, w), vec),
            pl.BlockSpec((None, LRU_BLOCKS, LANES, LANES), lambda g, di, j: (di, 0, 0, 0)),
            pl.BlockSpec((None, 1, w), dvec),
            pl.BlockSpec((None, LRU_BLOCKS, LANES, LANES), lambda g, di, j: (di, 0, 0, 0)),
            pl.BlockSpec((None, 1, w), dvec),
            pl.BlockSpec((None, 1, w), dvec),
        ],
        out_specs=[
            pl.BlockSpec((None, nb, tc, w), lambda g, di, j: (di, g, chunk(di, j), 0)),
            pl.BlockSpec((None, nb, w), lambda g, di, j: (di, g, 0)),
        ],
        scratch_shapes=[
            pltpu.VMEM((nb, tc + 2 * BF16_ROWS, w), F32),
            pltpu.VMEM((LRU_BLOCKS, nb * tc, LANES), F32),
            pltpu.VMEM((LRU_BLOCKS, nb * tc, LANES), F32),
            pltpu.VMEM((LRU_BLOCKS, nb, LANES), F32),
        ],
        compiler_params=_cp("parallel", "arbitrary", "arbitrary"),
        name="rglru",
    )(x, x, x, h0, lp["conv_w"], lp["conv_b"], lp["wa"], lp["ba"], lp["wx"], lp["bx"], lp["lam"])


def _softmax_pv(s_parts, v_parts):
    m = functools.reduce(jnp.maximum, [jnp.max(s, axis=-1, keepdims=True) for s in s_parts])
    p_parts = [jnp.exp(s - m) for s in s_parts]
    l = functools.reduce(jnp.add, [jnp.sum(p, axis=-1, keepdims=True) for p in p_parts])
    o = functools.reduce(jnp.add, [jnp.dot(p.astype(BF16), v, preferred_element_type=F32)
                                   for p, v in zip(p_parts, v_parts)])
    return o / l


def _na_kernel(q_ref, k_ref, v_ref, kc_ref, vc_ref, bias_ref, o_ref, *, kh, rows, heads):
    r = pl.program_id(1)
    rs = jnp.clip(r - kh // 2, 0, rows - kh)
    start = pl.multiple_of(rs * GRID_W, GRID_W)
    nwin = kh * GRID_W
    scale = HEAD_DIM ** -0.5
    outs = []
    for h in range(heads):
        sl = slice(h * HEAD_DIM, (h + 1) * HEAD_DIM)
        qh = q_ref[:, sl]
        s_lat = lax.dot_general(qh, k_ref[pl.ds(start, nwin), sl], _NT, preferred_element_type=F32) * scale
        s_lat = s_lat + bias_ref[h]
        s_ctx = lax.dot_general(qh, kc_ref[:, sl], _NT, preferred_element_type=F32) * scale
        outs.append(_softmax_pv([s_lat, s_ctx], [v_ref[pl.ds(start, nwin), sl], vc_ref[:, sl]]))
    o_ref[...] = jnp.concatenate(outs, axis=-1).astype(o_ref.dtype)


def _na_bias_table(rpb, rows):
    kh = min(NA_KH_MAX, rows)
    col = np.arange(GRID_W)
    cs = np.clip(col - NA_KW // 2, 0, GRID_W - NA_KW)
    col_mask = (col[None, :] >= cs[:, None]) & (col[None, :] < cs[:, None] + NA_KW)
    dcol = np.clip(col[None, :] - col[:, None], -(NA_KW - 1), NA_KW - 1) + NA_KW - 1
    drow = np.arange(kh)[None, :] - np.arange(kh)[:, None] + NA_KH_MAX - 1
    tab = rpb[:, drow[:, None, :, None], dcol[None, :, None, :]]
    tab = jnp.where(col_mask[None, None, :, None, :], tab.astype(F32), NEG_INF)
    return jnp.moveaxis(tab, 1, 0).reshape(kh, rpb.shape[0], GRID_W, kh * GRID_W)


def _na(q, k, v, kc, vc, rpb):
    b, s, w = q.shape
    l = kc.shape[1]
    heads = w // HEAD_DIM
    rows = s // GRID_W
    kh = min(NA_KH_MAX, rows)
    bias = _na_bias_table(rpb, rows)

    def variant(bi, r):
        return (r - jnp.clip(r - kh // 2, 0, rows - kh), 0, 0, 0)

    full = lambda bi, r: (bi, 0, 0)
    return pl.pallas_call(
        functools.partial(_na_kernel, kh=kh, rows=rows, heads=heads),
        out_shape=jax.ShapeDtypeStruct((b, s, w), BF16),
        grid=(b, rows),
        in_specs=[
            pl.BlockSpec((None, GRID_W, w), lambda bi, r: (bi, r, 0)),
            pl.BlockSpec((None, s, w), full),
            pl.BlockSpec((None, s, w), full),
            pl.BlockSpec((None, l, w), full),
            pl.BlockSpec((None, l, w), full),
            pl.BlockSpec((None, heads, GRID_W, kh * GRID_W), variant),
        ],
        out_specs=pl.BlockSpec((None, GRID_W, w), lambda bi, r: (bi, r, 0)),
        compiler_params=_cp("parallel", "arbitrary"),
        name="natten",
    )(q, k, v, kc, vc, bias)


def _head_norm(x, g, gmat):
    cols = []
    for c in range(x.shape[1] // LANES):
        xc = x[:, c * LANES:(c + 1) * LANES]
        x2 = xc * xc
        hi = x2.astype(BF16)
        lo = (x2 - hi.astype(F32)).astype(BF16)
        ms = jnp.dot(hi, gmat, preferred_element_type=F32) + jnp.dot(lo, gmat, preferred_element_type=F32)
        cols.append(xc * lax.rsqrt(ms + EPS) * g)
    return cols[0] if len(cols) == 1 else jnp.concatenate(cols, axis=-1)


def _rope(x, cos, sin):
    half = HEAD_DIM // 2
    lane = lax.broadcasted_iota(jnp.int32, (x.shape[0], LANES), 1)
    low = (lane % HEAD_DIM) < half
    cols = []
    for c in range(x.shape[1] // LANES):
        xc = x[:, c * LANES:(c + 1) * LANES]
        sw = jnp.where(low, pltpu.roll(xc, LANES - half, 1), pltpu.roll(xc, half, 1))
        cols.append(xc * cos + sw * sin)
    return cols[0] if len(cols) == 1 else jnp.concatenate(cols, axis=-1)


def _attn_kernel(*refs, norm, rope, has_lat, group, tq):
    refs = list(refs)
    q_ref, kc_ref, vc_ref = refs[:3]
    refs = refs[3:]
    if has_lat:
        kl_ref, vl_ref = refs[:2]
        refs = refs[2:]
    if norm:
        qg_ref, kg_ref, gmat_ref = refs[:3]
        refs = refs[3:]
    if rope:
        cosq_ref, sinq_ref, cosk_ref, sink_ref = refs[:4]
        refs = refs[4:]
    o_ref = refs[0]
    refs = refs[1:]
    if norm:
        kc_scr = refs[0]
        refs = refs[1:]
        if has_lat:
            kl_scr = refs[0]

    if norm:
        @pl.when(pl.program_id(1) == 0)
        def _():
            kc_scr[...] = _head_norm(kc_ref[...].astype(F32), kg_ref[...], gmat_ref[...]).astype(BF16)
            if has_lat:
                kl = _head_norm(kl_ref[...].astype(F32), kg_ref[...], gmat_ref[...])
                if rope:
                    kl = _rope(kl, cosk_ref[...], sink_ref[...])
                kl_scr[...] = kl.astype(BF16)
        kc_src = kc_scr
        kl_src = kl_scr if has_lat else None
    else:
        kc_src = kc_ref
        kl_src = kl_ref if has_lat else None

    q = q_ref[...].astype(F32)
    if norm:
        q = _head_norm(q, qg_ref[...], gmat_ref[...])
    if rope:
        q = _rope(q, cosq_ref[...], sinq_ref[...])
    q = (q * (HEAD_DIM ** -0.5)).astype(BF16)

    n_kv = kc_ref.shape[1] // HEAD_DIM
    outs = []
    for g in range(n_kv):
        sl = slice(g * HEAD_DIM, (g + 1) * HEAD_DIM)
        qs = [q[:, (g * group + jj) * HEAD_DIM:(g * group + jj + 1) * HEAD_DIM] for jj in range(group)]
        qs = qs[0] if group == 1 else jnp.concatenate(qs, axis=0)
        s_parts = [lax.dot_general(qs, kc_src[:, sl], _NT, preferred_element_type=F32)]
        v_parts = [vc_ref[:, sl]]
        if has_lat:
            s_parts.append(lax.dot_general(qs, kl_src[:, sl], _NT, preferred_element_type=F32))
            v_parts.append(vl_ref[:, sl])
        o = _softmax_pv(s_parts, v_parts)
        for jj in range(group):
            outs.append(o[jj * tq:(jj + 1) * tq])
    o_ref[...] = jnp.concatenate(outs, axis=-1).astype(o_ref.dtype)


def _rope_tables(n_tokens):
    t = np.arange(n_tokens)
    pos = np.stack([t // GRID_W, t % GRID_W], axis=-1).astype(np.float32)
    n_freq = HEAD_DIM // 4
    inv_freq = jnp.asarray(ROPE_BASE, F32) ** (-jnp.arange(n_freq, dtype=F32) / n_freq)
    ang = (jnp.asarray(pos)[:, :, None] * inv_freq).reshape(n_tokens, 2 * n_freq)
    cos, sin = jnp.cos(ang), jnp.sin(ang)
    reps = LANES // HEAD_DIM
    return jnp.tile(jnp.concatenate([cos, cos], -1), (1, reps)), jnp.tile(jnp.concatenate([-sin, sin], -1), (1, reps))


def _attn(q, kc, vc, kl=None, vl=None, qg=None, kg=None, rope=False):
    b, t, wq = q.shape
    l, wk = kc.shape[1], kc.shape[2]
    has_lat = kl is not None
    norm = qg is not None
    group = wq // wk
    tq = min(t, 128)
    full = lambda bi, i: (bi, 0, 0)
    const = lambda bi, i: (0, 0)
    args = [q, kc, vc]
    in_specs = [
        pl.BlockSpec((None, tq, wq), lambda bi, i: (bi, i, 0)),
        pl.BlockSpec((None, l, wk), full),
        pl.BlockSpec((None, l, wk), full),
    ]
    scratch = []
    if has_lat:
        s = kl.shape[1]
        args += [kl, vl]
        in_specs += [pl.BlockSpec((None, s, wk), full)] * 2
    if norm:
        reps = LANES // HEAD_DIM
        blk = np.arange(LANES) // HEAD_DIM
        gmat = jnp.asarray((blk[:, None] == blk[None, :]).astype(np.float32) / HEAD_DIM, BF16)
        args += [jnp.tile(qg, reps).reshape(1, LANES), jnp.tile(kg, reps).reshape(1, LANES), gmat]
        in_specs += [pl.BlockSpec((1, LANES), const)] * 2 + [pl.BlockSpec((LANES, LANES), const)]
        scratch.append(pltpu.VMEM((l, wk), BF16))
        if has_lat:
            scratch.append(pltpu.VMEM((kl.shape[1], wk), BF16))
    if rope:
        cos, sin = _rope_tables(t)
        args += [cos, sin, cos, sin]
        in_specs += [pl.BlockSpec((tq, LANES), lambda bi, i: (i, 0))] * 2 + [pl.BlockSpec((t, LANES), const)] * 2
    return pl.pallas_call(
        functools.partial(_attn_kernel, norm=norm, rope=rope, has_lat=has_lat, group=group, tq=tq),
        out_shape=jax.ShapeDtypeStruct((b, t, wq), BF16),
        grid=(b, t // tq),
        in_specs=in_specs,
        out_specs=pl.BlockSpec((None, tq, wq), lambda bi, i: (bi, i, 0)),
        scratch_shapes=scratch,
        compiler_params=_cp("parallel", "arbitrary"),
        name="attn",
    )(*args)


def _merge_kernel(hf_ref, hr_ref, gl_ref, yb_ref, yc_ref, gt_ref, x_ref, g1_ref, sh2_ref, sc2_ref, ng1_ref, ng2_ref,
                  woa_ref, wob_ref, woc_ref, wout_ref, x1_ref, h2_ref):
    d = x_ref.shape[-1]
    ya = (hf_ref[...].astype(F32) + hr_ref[...].astype(F32)) * jax.nn.gelu(gl_ref[...].astype(F32))
    pa = jnp.dot(ya.astype(BF16), woa_ref[...], preferred_element_type=F32)
    pb = jnp.dot(yb_ref[...], wob_ref[...], preferred_element_type=F32)
    pc = jnp.dot(yc_ref[...], woc_ref[...], preferred_element_type=F32)
    ga = jax.nn.sigmoid(gt_ref[:, 0:d].astype(F32))
    gb = jax.nn.sigmoid(gt_ref[:, d:2 * d].astype(F32))
    gc = jax.nn.sigmoid(gt_ref[:, 2 * d:3 * d].astype(F32))
    m = ga * pa + gb * pb + gc * pc
    y = jnp.dot(m.astype(BF16), wout_ref[...], preferred_element_type=F32)
    x1 = x_ref[...] + g1_ref[...] * _rms(y, ng1_ref[...])
    x1_ref[...] = x1
    h2_ref[...] = (_rms(x1, ng2_ref[...]) * (1.0 + sc2_ref[...]) + sh2_ref[...]).astype(h2_ref.dtype)


def _merge(h, gl, yb, yc, gates, x, g1, sh2, sc2, ng1, ng2, woa, wob, woc, wout, mod_row):
    b, t, d = x.shape
    tm = min(t, 256)
    wl, wb, wc = gl.shape[2], yb.shape[2], yc.shape[2]
    if mod_row is None:
        mod_map = lambda bi, i: (bi, 0, 0)
    else:
        mod_map = lambda bi, i: (mod_row, 0, 0)
    tok = lambda w: pl.BlockSpec((None, tm, w), lambda bi, i: (bi, i, 0))
    mod = pl.BlockSpec((None, 1, d), mod_map)
    const = lambda r, c: pl.BlockSpec((r, c), lambda bi, i: (0, 0))
    return pl.pallas_call(
        _merge_kernel,
        out_shape=[jax.ShapeDtypeStruct((b, t, d), F32), jax.ShapeDtypeStruct((b, t, d), BF16)],
        grid=(b, t // tm),
        in_specs=[
            pl.BlockSpec((None, None, tm, wl), lambda bi, i: (0, bi, i, 0)),
            pl.BlockSpec((None, None, tm, wl), lambda bi, i: (1, bi, i, 0)),
            tok(wl), tok(wb), tok(wc), tok(N_BRANCH * d), tok(d),
            mod, mod, mod, const(1, d), const(1, d),
            const(wl, d), const(wb, d), const(wc, d), const(d, d),
        ],
        out_specs=[tok(d), tok(d)],
        compiler_params=_cp("parallel", "parallel"),
        name="merge",
    )(h, h, gl, yb, yc, gates, x, g1, sh2, sc2, ng1.reshape(1, d), ng2.reshape(1, d), woa, wob, woc, wout)


def _lane_cumsum(x, tri):
    e, n = x.shape
    carry = jnp.zeros((e, 1), F32)
    cols = []
    for c in range(n // LANES):
        part = jnp.dot(x[:, c * LANES:(c + 1) * LANES].astype(BF16), tri, preferred_element_type=F32) + carry
        cols.append(part)
        carry = part[:, LANES - 1:LANES]
    return cols[0] if len(cols) == 1 else jnp.concatenate(cols, axis=-1)


def _route_kernel(h_ref, wr_ref, tri_ref, pos_ref, gate_ref, *, cap):
    logits = lax.dot_general(wr_ref[...], h_ref[...], _NT, preferred_element_type=F32)
    z = jnp.exp(logits - jnp.max(logits, axis=0, keepdims=True))
    aff = z / jnp.sum(z, axis=0, keepdims=True)
    gate_ref[...] = aff
    bits = pltpu.bitcast(aff, jnp.int32)
    e = bits.shape[0]

    def bisect(_, lohi):
        lo, hi = lohi
        mid = lo + ((hi - lo + 1) >> 1)
        cnt = jnp.sum(jnp.where(bits >= mid, 1.0, 0.0), axis=1, keepdims=True)
        ok = cnt >= cap
        return jnp.where(ok, mid, lo), jnp.where(ok, hi, mid - 1)

    one_bits = 0x3F800000
    lo, _ = lax.fori_loop(0, 31, bisect, (jnp.zeros((e, 1), jnp.int32), jnp.full((e, 1), one_bits, jnp.int32)))
    gt = jnp.where(bits > lo, 1.0, 0.0)
    eq = jnp.where(bits == lo, 1.0, 0.0)
    need = cap - jnp.sum(gt, axis=1, keepdims=True)
    tri = tri_ref[...]
    sel = gt + jnp.where(_lane_cumsum(eq, tri) <= need, eq, 0.0)
    pos = _lane_cumsum(sel, tri) - 1.0
    pos_ref[...] = jnp.where(sel > 0.0, pos, -1.0).astype(jnp.int32)


def _route(h2, w_router_t, cap):
    b, n, d = h2.shape
    e = w_router_t.shape[0]
    tri = jnp.asarray(np.triu(np.ones((LANES, LANES), np.float32)), BF16)
    return pl.pallas_call(
        functools.partial(_route_kernel, cap=cap),
        out_shape=[jax.ShapeDtypeStruct((b, e, n), jnp.int32), jax.ShapeDtypeStruct((b, e, n), F32)],
        grid=(b,),
        in_specs=[
            pl.BlockSpec((None, n, d), lambda bi: (bi, 0, 0)),
            pl.BlockSpec((e, d), lambda bi: (0, 0)),
            pl.BlockSpec((LANES, LANES), lambda bi: (0, 0)),
        ],
        out_specs=[pl.BlockSpec((None, e, n), lambda bi: (bi, 0, 0))] * 2,
        compiler_params=_cp("parallel"),
        name="route",
    )(h2, w_router_t, tri)


def _expert_kernel(h_ref, pos_ref, gate_ref, tok_ref, wg_ref, wu_ref, wd_ref, o_ref, *, cap):
    e = pl.program_id(1)
    n = h_ref.shape[0]
    pos = pos_ref[pl.ds(e, 1), :]
    gate = gate_ref[pl.ds(e, 1), :]
    onehot = pos == lax.broadcasted_iota(jnp.int32, (cap, n), 0)
    pb = jnp.where(onehot, 1.0, 0.0).astype(BF16)
    hs = jnp.dot(pb, h_ref[...], preferred_element_type=F32).astype(BF16)
    gcol = jnp.sum(jnp.where(onehot, gate, 0.0), axis=1, keepdims=True)
    a = jnp.dot(hs, wg_ref[...], preferred_element_type=F32)
    u = (a * jax.nn.sigmoid(a)) * jnp.dot(hs, wu_ref[...], preferred_element_type=F32)
    y = jnp.dot(u.astype(BF16), wd_ref[...], preferred_element_type=F32) * gcol
    digits = lax.dot_general(tok_ref[...], pb, _NT, preferred_element_type=F32)
    idx = (digits[0:1] * GRID_W + digits[1:2]).astype(jnp.int32)
    pt = jnp.where(idx == lax.broadcasted_iota(jnp.int32, (n, cap), 0), 1.0, 0.0).astype(BF16)
    contrib = jnp.dot(pt, y.astype(BF16), preferred_element_type=F32)

    @pl.when(e == 0)
    def _():
        o_ref[...] = contrib

    @pl.when(e > 0)
    def _():
        o_ref[...] += contrib


def _experts(h2, pos, gate, wg, wu, wd, cap):
    b, n, d = h2.shape
    e, _, f = wg.shape
    t = np.arange(n)
    tok = np.zeros((SUBLANES, n), np.float32)
    tok[0], tok[1] = t // GRID_W, t % GRID_W
    sample = lambda bi, ei: (bi, 0, 0)
    return pl.pallas_call(
        functools.partial(_expert_kernel, cap=cap),
        out_shape=jax.ShapeDtypeStruct((b, n, d), F32),
        grid=(b, e),
        in_specs=[
            pl.BlockSpec((None, n, d), sample),
            pl.BlockSpec((None, e, n), sample),
            pl.BlockSpec((None, e, n), sample),
            pl.BlockSpec((SUBLANES, n), lambda bi, ei: (0, 0)),
            pl.BlockSpec((None, d, f), lambda bi, ei: (ei, 0, 0)),
            pl.BlockSpec((None, d, f), lambda bi, ei: (ei, 0, 0)),
            pl.BlockSpec((None, f, d), lambda bi, ei: (ei, 0, 0)),
        ],
        out_specs=pl.BlockSpec((None, n, d), sample),
        compiler_params=_cp("parallel", "arbitrary"),
        name="experts",
    )(h2, pos, gate, jnp.asarray(tok, BF16), wg, wu, wd)


def _resid_kernel(x_ref, y_ref, g_ref, ng_ref, o_ref):
    o_ref[...] = x_ref[...] + g_ref[...] * _rms(y_ref[...], ng_ref[...])


def _resid(x, y, g2, ng, mod_row):
    b, t, d = x.shape
    tm = min(t, 512)
    if mod_row is None:
        mod_map = lambda bi, i: (bi, 0, 0)
    else:
        mod_map = lambda bi, i: (mod_row, 0, 0)
    tok = pl.BlockSpec((None, tm, d), lambda bi, i: (bi, i, 0))
    return pl.pallas_call(
        _resid_kernel,
        out_shape=jax.ShapeDtypeStruct((b, t, d), F32),
        grid=(b, t // tm),
        in_specs=[tok, tok, pl.BlockSpec((None, 1, d), mod_map), pl.BlockSpec((1, d), lambda bi, i: (0, 0))],
        out_specs=tok,
        compiler_params=_cp("parallel", "parallel"),
        name="resid",
    )(x, y, g2, ng.reshape(1, d))


def _split_cols(w, widths):
    offs = np.cumsum((0,) + tuple(widths))
    return [w[:, int(offs[i]):int(offs[i + 1])] for i in range(len(widths))]


def _layer(xl, xc, mods, p, need_ctx):
    bsz, s, d = xl.shape
    ctx_row = bsz
    sh1, sc1, g1, sh2, sc2, g2 = mods
    ng = p["norm_g"]
    na_w = (d // 128) * HEAD_DIM
    gq_w = (d // 128) * HEAD_DIM
    gkv_w = GQA_KV_HEADS * HEAD_DIM
    splits = (d, d, na_w, na_w, na_w, gq_w, gkv_w, gkv_w, N_BRANCH * d)
    w_in = p["w_in"].astype(BF16)

    lx, lg, nq, nk, nv, gq, gk, gv, gates = _inproj(xl, sh1, sc1, ng[0], w_in, splits, None)
    if need_ctx:
        cx, cg, cnq, cnk, cnv, cgq, cgk, cgv, cgates = _inproj(xc, sh1, sc1, ng[0], w_in, splits, ctx_row)
    else:
        parts = _split_cols(w_in, splits)
        kv_parts = (0, 3, 4, 6, 7)
        w_kv = jnp.concatenate([parts[i] for i in kv_parts], axis=1)
        cx, cnk, cnv, cgk, cgv = _inproj(xc, sh1, sc1, ng[0], w_kv, tuple(splits[i] for i in kv_parts), ctx_row)

    lp = {
        "conv_w": p["conv_w"], "conv_b": p["conv_b"].reshape(1, d),
        "wa": p["lru_wa"].astype(BF16), "ba": p["lru_ba"].reshape(2, 1, d),
        "wx": p["lru_wx"].astype(BF16), "bx": p["lru_bx"].reshape(2, 1, d),
        "lam": p["lru_lam"].reshape(2, 1, d),
    }
    hc, hfin = _lru(cx, jnp.zeros((2, bsz, d), F32), lp)
    hl, _ = _lru(lx, hfin, lp)

    yb = _na(nq, nk, nv, cnk, cnv, p["na_rpb"])
    yc = _attn(gq, cgk, cgv, gk, gv, p["qn_g"], p["kn_g"], rope=True)

    woa, wob, woc, wout = (p[k].astype(BF16) for k in ("w_o_a", "w_o_b", "w_o_c", "w_out"))
    wr_t = p["w_router"].T.astype(BF16)
    wg, wu, wd = (p[k].astype(BF16) for k in ("w_gate", "w_up", "w_down"))

    def ffn(x1, h2, mod_row):
        n = x1.shape[1]
        cap = EC_CAPACITY * n // N_EXPERTS
        pos, gate = _route(h2, wr_t, cap)
        moe = _experts(h2, pos, gate, wg, wu, wd, cap)
        return _resid(x1, moe, g2, ng[3], mod_row)

    x1, h2 = _merge(hl, lg, yb, yc, gates, xl, g1, sh2, sc2, ng[1], ng[2], woa, wob, woc, wout, None)
    xl = ffn(x1, h2, None)
    if need_ctx:
        ybc = _attn(cnq, cnk, cnv)
        ycc = _attn(cgq, cgk, cgv, qg=p["qn_g"], kg=p["kn_g"])
        x1c, h2c = _merge(hc, cg, ybc, ycc, cgates, xc, g1, sh2, sc2, ng[1], ng[2], woa, wob, woc, wout, ctx_row)
        xc = ffn(x1c, h2c, ctx_row)
    return xl, xc


def kernel(x, c, ctx, c_ctx, w_mod, b_mod, norm_g, w_in, conv_w, conv_b, lru_wa, lru_ba, lru_wx, lru_bx, lru_lam,
           na_rpb, qn_g, kn_g, w_o_a, w_o_b, w_o_c, w_out, w_router, w_gate, w_up, w_down):
    bsz, _, d = x.shape
    depth = w_mod.shape[0]
    rows = -(-(bsz + 1) // SUBLANES) * SUBLANES
    cond = jnp.zeros((rows, d), F32).at[:bsz].set(c).at[bsz].set(c_ctx)
    xl, xc = x, ctx
    for l in range(depth):
        p = {
            "norm_g": norm_g[l], "w_in": w_in[l], "conv_w": conv_w[l], "conv_b": conv_b[l],
            "lru_wa": lru_wa[l], "lru_ba": lru_ba[l], "lru_wx": lru_wx[l], "lru_bx": lru_bx[l],
            "lru_lam": lru_lam[l], "na_rpb": na_rpb[l], "qn_g": qn_g[l], "kn_g": kn_g[l],
            "w_o_a": w_o_a[l], "w_o_b": w_o_b[l], "w_o_c": w_o_c[l], "w_out": w_out[l],
            "w_router": w_router[l], "w_gate": w_gate[l], "w_up": w_up[l], "w_down": w_down[l],
        }
        m = _adaln(cond, w_mod[l], b_mod[l])
        mods = [m[:, i * d:(i + 1) * d].reshape(rows, 1, d) for i in range(6)]
        xl, xc = _layer(xl, xc, mods, p, l < depth - 1)
    return xl
```

```python
import functools

import jax
import jax.numpy as jnp
import numpy as np
from jax import lax
from jax.experimental import pallas as pl
from jax.experimental.pallas import tpu as pltpu

F32 = jnp.float32
BF16 = jnp.bfloat16

GRID_W = 64
HEAD_DIM = 64
LRU_BLOCKS = 8
CONV_W = 4
LRU_C = 8.0
NA_KH_MAX = 8
NA_KW = 16
GQA_KV_HEADS = 2
ROPE_BASE = 10000.0
N_EXPERTS = 16
EC_CAPACITY = 2
N_BRANCH = 3
EPS = 1e-6
NEG_INF = -1e30

LANES = 128
SUBLANES = 8
BF16_ROWS = 16
VMEM_LIMIT_BYTES = 56 * 1024 * 1024

_NT = (((1,), (1,)), ((), ()))


def _cp(*sem):
    return pltpu.CompilerParams(dimension_semantics=sem, vmem_limit_bytes=VMEM_LIMIT_BYTES)


def _sigmoid(x):
    return 0.5 * jnp.tanh(0.5 * x) + 0.5


def _rms(x, g):
    return x * lax.rsqrt(jnp.mean(x * x, axis=-1, keepdims=True) + EPS) * g


def _adaln_kernel(c_ref, w_ref, b_ref, o_ref):
    c = c_ref[...]
    s = (c * _sigmoid(c)).astype(BF16)
    o_ref[...] = jnp.dot(s, w_ref[...].astype(BF16), preferred_element_type=F32) + b_ref[...]


def _adaln(cond, w_mod, b_mod):
    r, d = cond.shape
    n = w_mod.shape[1]
    tn = n // 4
    return pl.pallas_call(
        _adaln_kernel,
        out_shape=jax.ShapeDtypeStruct((r, n), F32),
        grid=(n // tn,),
        in_specs=[
            pl.BlockSpec((r, d), lambda j: (0, 0)),
            pl.BlockSpec((d, tn), lambda j: (0, j)),
            pl.BlockSpec((1, tn), lambda j: (0, j)),
        ],
        out_specs=pl.BlockSpec((r, tn), lambda j: (0, j)),
        compiler_params=_cp("parallel"),
        name="adaln",
    )(cond, w_mod, b_mod.reshape(1, n))


def _inproj_kernel(x_ref, sh_ref, sc_ref, g_ref, w_ref, *o_refs, splits):
    h = (_rms(x_ref[...], g_ref[...]) * (1.0 + sc_ref[...]) + sh_ref[...]).astype(BF16)
    off = 0
    for o_ref, width in zip(o_refs, splits):
        o_ref[...] = jnp.dot(h, w_ref[:, off:off + width], preferred_element_type=F32).astype(o_ref.dtype)
        off += width


def _inproj(x, shift, scale, gain, w, splits, mod_row):
    b, t, d = x.shape
    tm = min(t, 512)
    n = w.shape[1]
    if mod_row is None:
        mod_map = lambda bi, i: (bi, 0, 0)
    else:
        mod_map = lambda bi, i: (mod_row, 0, 0)
    return pl.pallas_call(
        functools.partial(_inproj_kernel, splits=splits),
        out_shape=[jax.ShapeDtypeStruct((b, t, s), BF16) for s in splits],
        grid=(b, t // tm),
        in_specs=[
            pl.BlockSpec((None, tm, d), lambda bi, i: (bi, i, 0)),
            pl.BlockSpec((None, 1, d), mod_map),
            pl.BlockSpec((None, 1, d), mod_map),
            pl.BlockSpec((1, d), lambda bi, i: (0, 0)),
            pl.BlockSpec((d, n), lambda bi, i: (0, 0)),
        ],
        out_specs=[pl.BlockSpec((None, tm, s), lambda bi, i: (bi, i, 0)) for s in splits],
        compiler_params=_cp("parallel", "parallel"),
        name="inproj",
    )(x, shift, scale, gain.reshape(1, d), w)


def _lru_kernel(x_ref, xp_ref, xn_ref, h0_ref, smat_ref, cw_ref, cb_ref, wab_ref, ba_ref, bx_ref, lam_ref,
                y_ref, hfin_ref, u_scr, a_scr, b_scr, h_scr, st_scr, *, tc, nchunks, nb):
    d = pl.program_id(1)
    j = pl.program_id(2)
    c = jnp.where(d == 0, j, nchunks - 1 - j)
    left = CONV_W // 2
    w = x_ref.shape[-1]
    cwide = 2 * LANES

    zero = jnp.zeros((), BF16)
    for bb in range(nb):
        xe = jnp.concatenate([jnp.where(c > 0, xp_ref[bb], zero), x_ref[bb],
                              jnp.where(c < nchunks - 1, xn_ref[bb], zero)], axis=0)
        for cc in range(w // cwide):
            sl = slice(cc * cwide, (cc + 1) * cwide)
            sh = jnp.dot(smat_ref[...], xe[:, sl], preferred_element_type=F32)
            u = cb_ref[:, sl]
            for k in range(CONV_W):
                u = u + sh[k * tc:(k + 1) * tc] * cw_ref[k:k + 1, sl]
            u_scr[bb, :, sl] = u

    @pl.when(j == 0)
    def _():
        for cb in range(LRU_BLOCKS):
            h0 = h0_ref[:, cb * LANES:(cb + 1) * LANES]
            st_scr[cb] = jnp.broadcast_to(h0[:, None, :], (nb, SUBLANES, LANES))

    c_lam = -LRU_C * jax.nn.softplus(-lam_ref[...])
    for cb in range(LRU_BLOCKS):
        sl = slice(cb * LANES, (cb + 1) * LANES)
        u = u_scr[:, :, sl].reshape(nb * tc, LANES)
        ub = u.astype(BF16)
        gates = jnp.dot(ub, wab_ref[cb], preferred_element_type=F32)
        r = _sigmoid(gates[:, :LANES] + ba_ref[:, sl])
        i = _sigmoid(gates[:, LANES:] + bx_ref[:, sl])
        log_a = c_lam[:, sl] * r
        a = jnp.exp(log_a)
        a_scr[cb] = a
        m = -jnp.tanh(log_a) * (a * a + 1.0)
        b_scr[cb] = jnp.where(m > 0.0, m * lax.rsqrt(m), 0.0) * (i * u)

    @pl.when(d == 0)
    def _():
        _block_scan(a_scr, b_scr, h_scr, st_scr, False, nb, tc)

    @pl.when(d == 1)
    def _():
        _block_scan(a_scr, b_scr, h_scr, st_scr, True, nb, tc)

    for cb in range(LRU_BLOCKS):
        y_ref[:, :, cb * LANES:(cb + 1) * LANES] = h_scr[cb].astype(y_ref.dtype)

    @pl.when(j == nchunks - 1)
    def _():
        for cb in range(LRU_BLOCKS):
            hfin_ref[:, cb * LANES:(cb + 1) * LANES] = st_scr[cb][:, 0, :]


def _block_scan(a_scr, b_scr, h_scr, st_scr, rev, nb, tc):
    nblk = tc // SUBLANES
    row = lax.broadcasted_iota(jnp.int32, (1, SUBLANES, LANES), 1)
    for cb in range(LRU_BLOCKS):
        a = a_scr[cb].reshape(nb * nblk, SUBLANES, LANES)
        b = b_scr[cb].reshape(nb * nblk, SUBLANES, LANES)
        for s in (1, 2, 4):
            shift, keep = (SUBLANES - s, row < SUBLANES - s) if rev else (s, row >= s)
            b = b + a * jnp.where(keep, pltpu.roll(b, shift, 1), 0.0)
            a = a * jnp.where(keep, pltpu.roll(a, shift, 1), 1.0)
        a = a.reshape(nb, nblk, SUBLANES, LANES)
        b = b.reshape(nb, nblk, SUBLANES, LANES)
        carry = st_scr[cb]
        last = 0 if rev else SUBLANES - 1
        for k in (reversed(range(nblk)) if rev else range(nblk)):
            h = a[:, k] * carry + b[:, k]
            h_scr[cb, :, k * SUBLANES:(k + 1) * SUBLANES, :] = h
            carry = jnp.broadcast_to(h[:, last:last + 1, :], h.shape)
        st_scr[cb] = carry


def _lru(x, h0, lp):
    b, t, w = x.shape
    nb = SUBLANES
    tc = min(t, 128)
    nchunks = t // tc
    hb = tc // BF16_ROWS
    nhb = t // BF16_ROWS

    def chunk(di, j):
        return jnp.where(di == 0, j, nchunks - 1 - j)

    smat = np.zeros((CONV_W * tc, tc + 2 * BF16_ROWS), np.float32)
    for k in range(CONV_W):
        smat[k * tc + np.arange(tc), BF16_ROWS + np.arange(tc) + k - CONV_W // 2] = 1.0

    vec = lambda g, di, j: (0, 0)
    dvec = lambda g, di, j: (di, 0, 0)
    return pl.pallas_call(
        functools.partial(_lru_kernel, tc=tc, nchunks=nchunks, nb=nb),
        out_shape=[jax.ShapeDtypeStruct((2, b, t, w), BF16), jax.ShapeDtypeStruct((2, b, w), F32)],
        grid=(b // nb, 2, nchunks),
        in_specs=[
            pl.BlockSpec((nb, tc, w), lambda g, di, j: (g, chunk(di, j), 0)),
            pl.BlockSpec((nb, BF16_ROWS, w), lambda g, di, j: (g, jnp.maximum(chunk(di, j) * hb - 1, 0), 0)),
            pl.BlockSpec((nb, BF16_ROWS, w), lambda g, di, j: (g, jnp.minimum((chunk(di, j) + 1) * hb, nhb - 1), 0)),
            pl.BlockSpec((None, nb, w), lambda g, di, j: (di, g, 0)),
            pl.BlockSpec(smat.shape, vec),
            pl.BlockSpec((CONV_W, w), vec),
            pl.BlockSpec((1, w), vec),
            pl.BlockSpec((None, LRU_BLOCKS, LANES, 2 * LANES), lambda g, di, j: (di, 0, 0, 0)),
            pl.BlockSpec((None, 1, w), dvec),
            pl.BlockSpec((None, 1, w), dvec),
            pl.BlockSpec((None, 1, w), dvec),
        ],
        out_specs=[
            pl.BlockSpec((None, nb, tc, w), lambda g, di, j: (di, g, chunk(di, j), 0)),
            pl.BlockSpec((None, nb, w), lambda g, di, j: (di, g, 0)),
        ],
        scratch_shapes=[
            pltpu.VMEM((nb, tc, w), F32),
            pltpu.VMEM((LRU_BLOCKS, nb * tc, LANES), F32),
            pltpu.VMEM((LRU_BLOCKS, nb * tc, LANES), F32),
            pltpu.VMEM((LRU_BLOCKS, nb, tc, LANES), F32),
            pltpu.VMEM((LRU_BLOCKS, nb, SUBLANES, LANES), F32),
        ],
        compiler_params=_cp("parallel", "arbitrary", "arbitrary"),
        name="rglru",
    )(x, x, x, h0, jnp.asarray(smat, BF16), lp["conv_w"], lp["conv_b"], lp["wab"], lp["ba"], lp["bx"], lp["lam"])


def _softmax_pv(s_parts, v_parts):
    m = functools.reduce(jnp.maximum, [jnp.max(s, axis=-1, keepdims=True) for s in s_parts])
    p_parts = [jnp.exp(s - m) for s in s_parts]
    l = functools.reduce(jnp.add, [jnp.sum(p, axis=-1, keepdims=True) for p in p_parts])
    o = functools.reduce(jnp.add, [jnp.dot(p.astype(BF16), v, preferred_element_type=F32)
                                   for p, v in zip(p_parts, v_parts)])
    return o / l


def _na_kernel(q_ref, k_ref, v_ref, kc_ref, vc_ref, bias_ref, o_ref, *, kh, rows, heads):
    r = pl.program_id(1)
    rs = jnp.clip(r - kh // 2, 0, rows - kh)
    start = pl.multiple_of(rs * GRID_W, GRID_W)
    nwin = kh * GRID_W
    scale = HEAD_DIM ** -0.5
    outs = []
    for h in range(heads):
        sl = slice(h * HEAD_DIM, (h + 1) * HEAD_DIM)
        qh = q_ref[:, sl]
        s_lat = lax.dot_general(qh, k_ref[pl.ds(start, nwin), sl], _NT, preferred_element_type=F32) * scale
        s_lat = s_lat + bias_ref[h]
        s_ctx = lax.dot_general(qh, kc_ref[:, sl], _NT, preferred_element_type=F32) * scale
        outs.append(_softmax_pv([s_lat, s_ctx], [v_ref[pl.ds(start, nwin), sl], vc_ref[:, sl]]))
    o_ref[...] = jnp.concatenate(outs, axis=-1).astype(o_ref.dtype)


def _na_bias_table(rpb, rows):
    kh = min(NA_KH_MAX, rows)
    col = np.arange(GRID_W)
    cs = np.clip(col - NA_KW // 2, 0, GRID_W - NA_KW)
    col_mask = (col[None, :] >= cs[:, None]) & (col[None, :] < cs[:, None] + NA_KW)
    dcol = np.clip(col[None, :] - col[:, None], -(NA_KW - 1), NA_KW - 1) + NA_KW - 1
    heads, n_drow, n_dcol = rpb.shape
    onehot = jnp.asarray(dcol.reshape(-1)[None, :] == np.arange(n_dcol)[:, None], F32)
    t2 = jnp.dot(rpb.reshape(heads * n_drow, n_dcol).astype(F32), onehot, precision=lax.Precision.HIGHEST)
    t2 = t2.reshape(heads, n_drow, GRID_W, GRID_W)
    t2 = jnp.where(col_mask[None, None], t2, NEG_INF)
    tabs = [jnp.moveaxis(t2[:, NA_KH_MAX - 1 - v:NA_KH_MAX - 1 - v + kh], 1, 2) for v in range(kh)]
    return jnp.stack(tabs).reshape(kh, heads, GRID_W, kh * GRID_W)


def _na(q, k, v, kc, vc, rpb):
    b, s, w = q.shape
    l = kc.shape[1]
    heads = w // HEAD_DIM
    rows = s // GRID_W
    kh = min(NA_KH_MAX, rows)
    bias = _na_bias_table(rpb, rows)

    def variant(bi, r):
        return (r - jnp.clip(r - kh // 2, 0, rows - kh), 0, 0, 0)

    full = lambda bi, r: (bi, 0, 0)
    return pl.pallas_call(
        functools.partial(_na_kernel, kh=kh, rows=rows, heads=heads),
        out_shape=jax.ShapeDtypeStruct((b, s, w), BF16),
        grid=(b, rows),
        in_specs=[
            pl.BlockSpec((None, GRID_W, w), lambda bi, r: (bi, r, 0)),
            pl.BlockSpec((None, s, w), full),
            pl.BlockSpec((None, s, w), full),
            pl.BlockSpec((None, l, w), full),
            pl.BlockSpec((None, l, w), full),
            pl.BlockSpec((None, heads, GRID_W, kh * GRID_W), variant),
        ],
        out_specs=pl.BlockSpec((None, GRID_W, w), lambda bi, r: (bi, r, 0)),
        compiler_params=_cp("parallel", "arbitrary"),
        name="natten",
    )(q, k, v, kc, vc, bias)


def _head_norm(x, g, gmat):
    cols = []
    for c in range(x.shape[1] // LANES):
        xc = x[:, c * LANES:(c + 1) * LANES]
        x2 = xc * xc
        hi = x2.astype(BF16)
        lo = (x2 - hi.astype(F32)).astype(BF16)
        ms = jnp.dot(hi, gmat, preferred_element_type=F32) + jnp.dot(lo, gmat, preferred_element_type=F32)
        cols.append(xc * lax.rsqrt(ms + EPS) * g)
    return cols[0] if len(cols) == 1 else jnp.concatenate(cols, axis=-1)


def _rope(x, cos, sin):
    half = HEAD_DIM // 2
    lane = lax.broadcasted_iota(jnp.int32, (x.shape[0], LANES), 1)
    low = (lane % HEAD_DIM) < half
    cols = []
    for c in range(x.shape[1] // LANES):
        xc = x[:, c * LANES:(c + 1) * LANES]
        sw = jnp.where(low, pltpu.roll(xc, LANES - half, 1), pltpu.roll(xc, half, 1))
        cols.append(xc * cos + sw * sin)
    return cols[0] if len(cols) == 1 else jnp.concatenate(cols, axis=-1)


def _attn_kernel(*refs, norm, rope, has_lat, group, tq):
    refs = list(refs)
    q_ref, kc_ref, vc_ref = refs[:3]
    refs = refs[3:]
    if has_lat:
        kl_ref, vl_ref = refs[:2]
        refs = refs[2:]
    if norm:
        qg_ref, kg_ref, gmat_ref = refs[:3]
        refs = refs[3:]
    if rope:
        cosq_ref, sinq_ref, cosk_ref, sink_ref = refs[:4]
        refs = refs[4:]
    o_ref = refs[0]
    refs = refs[1:]
    if norm:
        kc_scr = refs[0]
        refs = refs[1:]
        if has_lat:
            kl_scr = refs[0]

    if norm:
        @pl.when(pl.program_id(1) == 0)
        def _():
            kc_scr[...] = _head_norm(kc_ref[...].astype(F32), kg_ref[...], gmat_ref[...]).astype(BF16)
            if has_lat:
                kl = _head_norm(kl_ref[...].astype(F32), kg_ref[...], gmat_ref[...])
                if rope:
                    kl = _rope(kl, cosk_ref[...], sink_ref[...])
                kl_scr[...] = kl.astype(BF16)
        kc_src = kc_scr
        kl_src = kl_scr if has_lat else None
    else:
        kc_src = kc_ref
        kl_src = kl_ref if has_lat else None

    q = q_ref[...].astype(F32)
    if norm:
        q = _head_norm(q, qg_ref[...], gmat_ref[...])
    if rope:
        q = _rope(q, cosq_ref[...], sinq_ref[...])
    q = (q * (HEAD_DIM ** -0.5)).astype(BF16)

    n_kv = kc_ref.shape[1] // HEAD_DIM
    outs = []
    for g in range(n_kv):
        sl = slice(g * HEAD_DIM, (g + 1) * HEAD_DIM)
        qs = [q[:, (g * group + jj) * HEAD_DIM:(g * group + jj + 1) * HEAD_DIM] for jj in range(group)]
        qs = qs[0] if group == 1 else jnp.concatenate(qs, axis=0)
        s_parts = [lax.dot_general(qs, kc_src[:, sl], _NT, preferred_element_type=F32)]
        v_parts = [vc_ref[:, sl]]
        if has_lat:
            s_parts.append(lax.dot_general(qs, kl_src[:, sl], _NT, preferred_element_type=F32))
            v_parts.append(vl_ref[:, sl])
        o = _softmax_pv(s_parts, v_parts)
        for jj in range(group):
            outs.append(o[jj * tq:(jj + 1) * tq])
    o_ref[...] = jnp.concatenate(outs, axis=-1).astype(o_ref.dtype)


def _rope_tables(n_tokens):
    t = np.arange(n_tokens)
    pos = np.stack([t // GRID_W, t % GRID_W], axis=-1).astype(np.float32)
    n_freq = HEAD_DIM // 4
    inv_freq = jnp.asarray(ROPE_BASE, F32) ** (-jnp.arange(n_freq, dtype=F32) / n_freq)
    ang = (jnp.asarray(pos)[:, :, None] * inv_freq).reshape(n_tokens, 2 * n_freq)
    cos, sin = jnp.cos(ang), jnp.sin(ang)
    reps = LANES // HEAD_DIM
    return jnp.tile(jnp.concatenate([cos, cos], -1), (1, reps)), jnp.tile(jnp.concatenate([-sin, sin], -1), (1, reps))


def _attn(q, kc, vc, kl=None, vl=None, qg=None, kg=None, rope=False):
    b, t, wq = q.shape
    l, wk = kc.shape[1], kc.shape[2]
    has_lat = kl is not None
    norm = qg is not None
    group = wq // wk
    tq = min(t, 128)
    full = lambda bi, i: (bi, 0, 0)
    const = lambda bi, i: (0, 0)
    args = [q, kc, vc]
    in_specs = [
        pl.BlockSpec((None, tq, wq), lambda bi, i: (bi, i, 0)),
        pl.BlockSpec((None, l, wk), full),
        pl.BlockSpec((None, l, wk), full),
    ]
    scratch = []
    if has_lat:
        s = kl.shape[1]
        args += [kl, vl]
        in_specs += [pl.BlockSpec((None, s, wk), full)] * 2
    if norm:
        reps = LANES // HEAD_DIM
        blk = np.arange(LANES) // HEAD_DIM
        gmat = jnp.asarray((blk[:, None] == blk[None, :]).astype(np.float32) / HEAD_DIM, BF16)
        args += [jnp.tile(qg, reps).reshape(1, LANES), jnp.tile(kg, reps).reshape(1, LANES), gmat]
        in_specs += [pl.BlockSpec((1, LANES), const)] * 2 + [pl.BlockSpec((LANES, LANES), const)]
        scratch.append(pltpu.VMEM((l, wk), BF16))
        if has_lat:
            scratch.append(pltpu.VMEM((kl.shape[1], wk), BF16))
    if rope:
        cos, sin = _rope_tables(t)
        args += [cos, sin, cos, sin]
        in_specs += [pl.BlockSpec((tq, LANES), lambda bi, i: (i, 0))] * 2 + [pl.BlockSpec((t, LANES), const)] * 2
    return pl.pallas_call(
        functools.partial(_attn_kernel, norm=norm, rope=rope, has_lat=has_lat, group=group, tq=tq),
        out_shape=jax.ShapeDtypeStruct((b, t, wq), BF16),
        grid=(b, t // tq),
        in_specs=in_specs,
        out_specs=pl.BlockSpec((None, tq, wq), lambda bi, i: (bi, i, 0)),
        scratch_shapes=scratch,
        compiler_params=_cp("parallel", "arbitrary"),
        name="attn",
    )(*args)


def _merge_kernel(hf_ref, hr_ref, gl_ref, yb_ref, yc_ref, gt_ref, x_ref, g1_ref, sh2_ref, sc2_ref, ng1_ref, ng2_ref,
                  woa_ref, wob_ref, woc_ref, wout_ref, x1_ref, h2_ref):
    d = x_ref.shape[-1]
    ya = (hf_ref[...].astype(F32) + hr_ref[...].astype(F32)) * jax.nn.gelu(gl_ref[...].astype(F32))
    pa = jnp.dot(ya.astype(BF16), woa_ref[...], preferred_element_type=F32)
    pb = jnp.dot(yb_ref[...], wob_ref[...], preferred_element_type=F32)
    pc = jnp.dot(yc_ref[...], woc_ref[...], preferred_element_type=F32)
    ga = _sigmoid(gt_ref[:, 0:d].astype(F32))
    gb = _sigmoid(gt_ref[:, d:2 * d].astype(F32))
    gc = _sigmoid(gt_ref[:, 2 * d:3 * d].astype(F32))
    m = ga * pa + gb * pb + gc * pc
    y = jnp.dot(m.astype(BF16), wout_ref[...], preferred_element_type=F32)
    x1 = x_ref[...] + g1_ref[...] * _rms(y, ng1_ref[...])
    x1_ref[...] = x1
    h2_ref[...] = (_rms(x1, ng2_ref[...]) * (1.0 + sc2_ref[...]) + sh2_ref[...]).astype(h2_ref.dtype)


def _merge(h, gl, yb, yc, gates, x, g1, sh2, sc2, ng1, ng2, woa, wob, woc, wout, mod_row):
    b, t, d = x.shape
    tm = min(t, 256)
    wl, wb, wc = gl.shape[2], yb.shape[2], yc.shape[2]
    if mod_row is None:
        mod_map = lambda bi, i: (bi, 0, 0)
    else:
        mod_map = lambda bi, i: (mod_row, 0, 0)
    tok = lambda w: pl.BlockSpec((None, tm, w), lambda bi, i: (bi, i, 0))
    mod = pl.BlockSpec((None, 1, d), mod_map)
    const = lambda r, c: pl.BlockSpec((r, c), lambda bi, i: (0, 0))
    return pl.pallas_call(
        _merge_kernel,
        out_shape=[jax.ShapeDtypeStruct((b, t, d), F32), jax.ShapeDtypeStruct((b, t, d), BF16)],
        grid=(b, t // tm),
        in_specs=[
            pl.BlockSpec((None, None, tm, wl), lambda bi, i: (0, bi, i, 0)),
            pl.BlockSpec((None, None, tm, wl), lambda bi, i: (1, bi, i, 0)),
            tok(wl), tok(wb), tok(wc), tok(N_BRANCH * d), tok(d),
            mod, mod, mod, const(1, d), const(1, d),
            const(wl, d), const(wb, d), const(wc, d), const(d, d),
        ],
        out_specs=[tok(d), tok(d)],
        compiler_params=_cp("parallel", "parallel"),
        name="merge",
    )(h, h, gl, yb, yc, gates, x, g1, sh2, sc2, ng1.reshape(1, d), ng2.reshape(1, d), woa, wob, woc, wout)


def _lane_cumsum(x, tri):
    e, n = x.shape
    carry = jnp.zeros((e, 1), F32)
    cols = []
    for c in range(n // LANES):
        part = jnp.dot(x[:, c * LANES:(c + 1) * LANES].astype(BF16), tri, preferred_element_type=F32) + carry
        cols.append(part)
        carry = part[:, LANES - 1:LANES]
    return cols[0] if len(cols) == 1 else jnp.concatenate(cols, axis=-1)


def _route_kernel(h_ref, wr_ref, tri_ref, pos_ref, gate_ref, *, cap):
    logits = lax.dot_general(wr_ref[...], h_ref[...], _NT, preferred_element_type=F32)
    z = jnp.exp(logits - jnp.max(logits, axis=0, keepdims=True))
    aff = z / jnp.sum(z, axis=0, keepdims=True)
    gate_ref[...] = aff
    bits = pltpu.bitcast(aff, jnp.int32)
    e = bits.shape[0]

    def bisect(_, lohi):
        lo, hi = lohi
        mid = lo + ((hi - lo + 1) >> 1)
        cnt = jnp.sum(jnp.where(bits >= mid, 1.0, 0.0), axis=1, keepdims=True)
        ok = cnt >= cap
        return jnp.where(ok, mid, lo), jnp.where(ok, hi, mid - 1)

    one_bits = 0x3F800000
    lo, _ = lax.fori_loop(0, 31, bisect, (jnp.zeros((e, 1), jnp.int32), jnp.full((e, 1), one_bits, jnp.int32)))
    gt = jnp.where(bits > lo, 1.0, 0.0)
    eq = jnp.where(bits == lo, 1.0, 0.0)
    need = cap - jnp.sum(gt, axis=1, keepdims=True)
    tri = tri_ref[...]
    sel = gt + jnp.where(_lane_cumsum(eq, tri) <= need, eq, 0.0)
    pos = _lane_cumsum(sel, tri) - 1.0
    pos_ref[...] = jnp.where(sel > 0.0, pos, -1.0).astype(jnp.int32)


def _route(h2, w_router_t, cap):
    b, n, d = h2.shape
    e = w_router_t.shape[0]
    tri = jnp.asarray(np.triu(np.ones((LANES, LANES), np.float32)), BF16)
    return pl.pallas_call(
        functools.partial(_route_kernel, cap=cap),
        out_shape=[jax.ShapeDtypeStruct((b, e, n), jnp.int32), jax.ShapeDtypeStruct((b, e, n), F32)],
        grid=(b,),
        in_specs=[
            pl.BlockSpec((None, n, d), lambda bi: (bi, 0, 0)),
            pl.BlockSpec((e, d), lambda bi: (0, 0)),
            pl.BlockSpec((LANES, LANES), lambda bi: (0, 0)),
        ],
        out_specs=[pl.BlockSpec((None, e, n), lambda bi: (bi, 0, 0))] * 2,
        compiler_params=_cp("parallel"),
        name="route",
    )(h2, w_router_t, tri)


def _expert_kernel(h_ref, pos_ref, gate_ref, tok_ref, wg_ref, wu_ref, wd_ref, o_ref, *, cap):
    e = pl.program_id(1)
    n = h_ref.shape[0]
    pos = pos_ref[pl.ds(e, 1), :]
    gate = gate_ref[pl.ds(e, 1), :]
    onehot = pos == lax.broadcasted_iota(jnp.int32, (cap, n), 0)
    pb = jnp.where(onehot, 1.0, 0.0).astype(BF16)
    hs = jnp.dot(pb, h_ref[...], preferred_element_type=F32).astype(BF16)
    gcol = jnp.sum(jnp.where(onehot, gate, 0.0), axis=1, keepdims=True)
    a = jnp.dot(hs, wg_ref[...], preferred_element_type=F32)
    u = (a * _sigmoid(a)) * jnp.dot(hs, wu_ref[...], preferred_element_type=F32)
    y = jnp.dot(u.astype(BF16), wd_ref[...], preferred_element_type=F32) * gcol
    digits = lax.dot_general(tok_ref[...], pb, _NT, preferred_element_type=F32)
    idx = (digits[0:1] * GRID_W + digits[1:2]).astype(jnp.int32)
    pt = jnp.where(idx == lax.broadcasted_iota(jnp.int32, (n, cap), 0), 1.0, 0.0).astype(BF16)
    contrib = jnp.dot(pt, y.astype(BF16), preferred_element_type=F32)

    @pl.when(e == 0)
    def _():
        o_ref[...] = contrib

    @pl.when(e > 0)
    def _():
        o_ref[...] += contrib


def _experts(h2, pos, gate, wg, wu, wd, cap):
    b, n, d = h2.shape
    e, _, f = wg.shape
    t = np.arange(n)
    tok = np.zeros((SUBLANES, n), np.float32)
    tok[0], tok[1] = t // GRID_W, t % GRID_W
    sample = lambda bi, ei: (bi, 0, 0)
    return pl.pallas_call(
        functools.partial(_expert_kernel, cap=cap),
        out_shape=jax.ShapeDtypeStruct((b, n, d), F32),
        grid=(b, e),
        in_specs=[
            pl.BlockSpec((None, n, d), sample),
            pl.BlockSpec((None, e, n), sample),
            pl.BlockSpec((None, e, n), sample),
            pl.BlockSpec((SUBLANES, n), lambda bi, ei: (0, 0)),
            pl.BlockSpec((None, d, f), lambda bi, ei: (ei, 0, 0)),
            pl.BlockSpec((None, d, f), lambda bi, ei: (ei, 0, 0)),
            pl.BlockSpec((None, f, d), lambda bi, ei: (ei, 0, 0)),
        ],
        out_specs=pl.BlockSpec((None, n, d), sample),
        compiler_params=_cp("parallel", "arbitrary"),
        name="experts",
    )(h2, pos, gate, jnp.asarray(tok, BF16), wg, wu, wd)


def _resid_kernel(x_ref, y_ref, g_ref, ng_ref, o_ref):
    o_ref[...] = x_ref[...] + g_ref[...] * _rms(y_ref[...], ng_ref[...])


def _resid(x, y, g2, ng, mod_row):
    b, t, d = x.shape
    tm = min(t, 512)
    if mod_row is None:
        mod_map = lambda bi, i: (bi, 0, 0)
    else:
        mod_map = lambda bi, i: (mod_row, 0, 0)
    tok = pl.BlockSpec((None, tm, d), lambda bi, i: (bi, i, 0))
    return pl.pallas_call(
        _resid_kernel,
        out_shape=jax.ShapeDtypeStruct((b, t, d), F32),
        grid=(b, t // tm),
        in_specs=[tok, tok, pl.BlockSpec((None, 1, d), mod_map), pl.BlockSpec((1, d), lambda bi, i: (0, 0))],
        out_specs=tok,
        compiler_params=_cp("parallel", "parallel"),
        name="resid",
    )(x, y, g2, ng.reshape(1, d))


def _split_cols(w, widths):
    offs = np.cumsum((0,) + tuple(widths))
    return [w[:, int(offs[i]):int(offs[i + 1])] for i in range(len(widths))]


def _layer(xl, xc, mods, p, need_ctx):
    bsz, s, d = xl.shape
    ctx_row = bsz
    sh1, sc1, g1, sh2, sc2, g2 = mods
    ng = p["norm_g"]
    na_w = (d // 128) * HEAD_DIM
    gq_w = (d // 128) * HEAD_DIM
    gkv_w = GQA_KV_HEADS * HEAD_DIM
    splits = (d, d, na_w, na_w, na_w, gq_w, gkv_w, gkv_w, N_BRANCH * d)
    w_in = p["w_in"].astype(BF16)

    lx, lg, nq, nk, nv, gq, gk, gv, gates = _inproj(xl, sh1, sc1, ng[0], w_in, splits, None)
    if need_ctx:
        cx, cg, cnq, cnk, cnv, cgq, cgk, cgv, cgates = _inproj(xc, sh1, sc1, ng[0], w_in, splits, ctx_row)
    else:
        parts = _split_cols(w_in, splits)
        kv_parts = (0, 3, 4, 6, 7)
        w_kv = jnp.concatenate([parts[i] for i in kv_parts], axis=1)
        cx, cnk, cnv, cgk, cgv = _inproj(xc, sh1, sc1, ng[0], w_kv, tuple(splits[i] for i in kv_parts), ctx_row)

    lp = {
        "conv_w": p["conv_w"], "conv_b": p["conv_b"].reshape(1, d),
        "wab": jnp.concatenate([p["lru_wa"], p["lru_wx"]], axis=-1).astype(BF16),
        "ba": p["lru_ba"].reshape(2, 1, d), "bx": p["lru_bx"].reshape(2, 1, d),
        "lam": p["lru_lam"].reshape(2, 1, d),
    }
    hc, hfin = _lru(cx, jnp.zeros((2, bsz, d), F32), lp)
    hl, _ = _lru(lx, hfin, lp)

    yb = _na(nq, nk, nv, cnk, cnv, p["na_rpb"])
    yc = _attn(gq, cgk, cgv, gk, gv, p["qn_g"], p["kn_g"], rope=True)

    woa, wob, woc, wout = (p[k].astype(BF16) for k in ("w_o_a", "w_o_b", "w_o_c", "w_out"))
    wr_t = p["w_router"].T.astype(BF16)
    wg, wu, wd = (p[k].astype(BF16) for k in ("w_gate", "w_up", "w_down"))

    def ffn(x1, h2, mod_row):
        n = x1.shape[1]
        cap = EC_CAPACITY * n // N_EXPERTS
        pos, gate = _route(h2, wr_t, cap)
        moe = _experts(h2, pos, gate, wg, wu, wd, cap)
        return _resid(x1, moe, g2, ng[3], mod_row)

    x1, h2 = _merge(hl, lg, yb, yc, gates, xl, g1, sh2, sc2, ng[1], ng[2], woa, wob, woc, wout, None)
    xl = ffn(x1, h2, None)
    if need_ctx:
        ybc = _attn(cnq, cnk, cnv)
        ycc = _attn(cgq, cgk, cgv, qg=p["qn_g"], kg=p["kn_g"])
        x1c, h2c = _merge(hc, cg, ybc, ycc, cgates, xc, g1, sh2, sc2, ng[1], ng[2], woa, wob, woc, wout, ctx_row)
        xc = ffn(x1c, h2c, ctx_row)
    return xl, xc


def kernel(x, c, ctx, c_ctx, w_mod, b_mod, norm_g, w_in, conv_w, conv_b, lru_wa, lru_ba, lru_wx, lru_bx, lru_lam,
           na_rpb, qn_g, kn_g, w_o_a, w_o_b, w_o_c, w_out, w_router, w_gate, w_up, w_down):
    bsz, _, d = x.shape
    depth = w_mod.shape[0]
    rows = -(-(bsz + 1) // SUBLANES) * SUBLANES
    cond = jnp.zeros((rows, d), F32).at[:bsz].set(c).at[bsz].set(c_ctx)
    xl, xc = x, ctx
    for l in range(depth):
        p = {
            "norm_g": norm_g[l], "w_in": w_in[l], "conv_w": conv_w[l], "conv_b": conv_b[l],
            "lru_wa": lru_wa[l], "lru_ba": lru_ba[l], "lru_wx": lru_wx[l], "lru_bx": lru_bx[l],
            "lru_lam": lru_lam[l], "na_rpb": na_rpb[l], "qn_g": qn_g[l], "kn_g": kn_g[l],
            "w_o_a": w_o_a[l], "w_o_b": w_o_b[l], "w_o_c": w_o_c[l], "w_out": w_out[l],
            "w_router": w_router[l], "w_gate": w_gate[l], "w_up": w_up[l], "w_down": w_down[l],
        }
        m = _adaln(cond, w_mod[l], b_mod[l])
        mods = [m[:, i * d:(i + 1) * d].reshape(rows, 1, d) for i in range(6)]
        xl, xc = _layer(xl, xc, mods, p, l < depth - 1)
    return xl
```

```python
import functools

import jax
import jax.numpy as jnp
import numpy as np
from jax import lax
from jax.experimental import pallas as pl
from jax.experimental.pallas import tpu as pltpu

F32 = jnp.float32
BF16 = jnp.bfloat16

GRID_W = 64
HEAD_DIM = 64
LRU_BLOCKS = 8
CONV_W = 4
LRU_C = 8.0
NA_KH_MAX = 8
NA_KW = 16
GQA_KV_HEADS = 2
ROPE_BASE = 10000.0
N_EXPERTS = 16
EC_CAPACITY = 2
N_BRANCH = 3
EPS = 1e-6
NEG_INF = -1e30

LANES = 128
SUBLANES = 8
BF16_ROWS = 16
VMEM_LIMIT_BYTES = 56 * 1024 * 1024

_NT = (((1,), (1,)), ((), ()))


def _cp(*sem):
    return pltpu.CompilerParams(dimension_semantics=sem, vmem_limit_bytes=VMEM_LIMIT_BYTES)


def _sigmoid(x):
    return 0.5 * jnp.tanh(0.5 * x) + 0.5


def _rms(x, g):
    return x * lax.rsqrt(jnp.mean(x * x, axis=-1, keepdims=True) + EPS) * g


def _adaln_kernel(c_ref, w_ref, b_ref, o_ref):
    c = c_ref[...]
    s = (c * _sigmoid(c)).astype(BF16)
    o_ref[...] = jnp.dot(s, w_ref[...].astype(BF16), preferred_element_type=F32) + b_ref[...]


def _adaln(cond, w_mod, b_mod):
    r, d = cond.shape
    n = w_mod.shape[1]
    tn = n // 4
    return pl.pallas_call(
        _adaln_kernel,
        out_shape=jax.ShapeDtypeStruct((r, n), F32),
        grid=(n // tn,),
        in_specs=[
            pl.BlockSpec((r, d), lambda j: (0, 0)),
            pl.BlockSpec((d, tn), lambda j: (0, j)),
            pl.BlockSpec((1, tn), lambda j: (0, j)),
        ],
        out_specs=pl.BlockSpec((r, tn), lambda j: (0, j)),
        compiler_params=_cp("parallel"),
        name="adaln",
    )(cond, w_mod, b_mod.reshape(1, n))


def _inproj_kernel(x_ref, sh_ref, sc_ref, g_ref, w_ref, *o_refs, splits):
    h = (_rms(x_ref[...], g_ref[...]) * (1.0 + sc_ref[...]) + sh_ref[...]).astype(BF16)
    off = 0
    for o_ref, width in zip(o_refs, splits):
        o_ref[...] = jnp.dot(h, w_ref[:, off:off + width], preferred_element_type=F32).astype(o_ref.dtype)
        off += width


def _inproj(x, shift, scale, gain, w, splits, mod_row):
    b, t, d = x.shape
    tm = min(t, 512)
    n = w.shape[1]
    if mod_row is None:
        mod_map = lambda bi, i: (bi, 0, 0)
    else:
        mod_map = lambda bi, i: (mod_row, 0, 0)
    return pl.pallas_call(
        functools.partial(_inproj_kernel, splits=splits),
        out_shape=[jax.ShapeDtypeStruct((b, t, s), BF16) for s in splits],
        grid=(b, t // tm),
        in_specs=[
            pl.BlockSpec((None, tm, d), lambda bi, i: (bi, i, 0)),
            pl.BlockSpec((None, 1, d), mod_map),
            pl.BlockSpec((None, 1, d), mod_map),
            pl.BlockSpec((1, d), lambda bi, i: (0, 0)),
            pl.BlockSpec((d, n), lambda bi, i: (0, 0)),
        ],
        out_specs=[pl.BlockSpec((None, tm, s), lambda bi, i: (bi, i, 0)) for s in splits],
        compiler_params=_cp("parallel", "parallel"),
        name="inproj",
    )(x, shift, scale, gain.reshape(1, d), w)


def _lru_kernel(x_ref, xp_ref, xn_ref, h0_ref, smat_ref, cw_ref, cb_ref, wab_ref, ba_ref, bx_ref, lam_ref,
                y_ref, hfin_ref, u_scr, a_scr, b_scr, h_scr, st_scr, *, tc, nchunks, nb):
    d = pl.program_id(1)
    j = pl.program_id(2)
    c = jnp.where(d == 0, j, nchunks - 1 - j)
    left = CONV_W // 2
    w = x_ref.shape[-1]
    cwide = 2 * LANES

    zero = jnp.zeros((), BF16)
    for bb in range(nb):
        xe = jnp.concatenate([jnp.where(c > 0, xp_ref[bb], zero), x_ref[bb],
                              jnp.where(c < nchunks - 1, xn_ref[bb], zero)], axis=0)
        for cc in range(w // cwide):
            sl = slice(cc * cwide, (cc + 1) * cwide)
            sh = jnp.dot(smat_ref[...], xe[:, sl], preferred_element_type=F32)
            u = cb_ref[:, sl]
            for k in range(CONV_W):
                u = u + sh[k * tc:(k + 1) * tc] * cw_ref[k:k + 1, sl]
            u_scr[bb, :, sl] = u

    @pl.when(j == 0)
    def _():
        for cb in range(LRU_BLOCKS):
            h0 = h0_ref[:, cb * LANES:(cb + 1) * LANES]
            st_scr[cb] = jnp.broadcast_to(h0[:, None, :], (nb, SUBLANES, LANES))

    c_lam = -LRU_C * jax.nn.softplus(-lam_ref[...])
    for cb in range(LRU_BLOCKS):
        sl = slice(cb * LANES, (cb + 1) * LANES)
        u = u_scr[:, :, sl].reshape(nb * tc, LANES)
        ub = u.astype(BF16)
        gates = jnp.dot(ub, wab_ref[cb], preferred_element_type=F32)
        r = _sigmoid(gates[:, :LANES] + ba_ref[:, sl])
        i = _sigmoid(gates[:, LANES:] + bx_ref[:, sl])
        log_a = c_lam[:, sl] * r
        a = jnp.exp(log_a)
        a_scr[cb] = a
        m = -jnp.tanh(log_a) * (a * a + 1.0)
        b_scr[cb] = jnp.where(m > 0.0, m * lax.rsqrt(m), 0.0) * (i * u)

    @pl.when(d == 0)
    def _():
        _block_scan(a_scr, b_scr, h_scr, st_scr, False, nb, tc)

    @pl.when(d == 1)
    def _():
        _block_scan(a_scr, b_scr, h_scr, st_scr, True, nb, tc)

    for cb in range(LRU_BLOCKS):
        y_ref[:, :, cb * LANES:(cb + 1) * LANES] = h_scr[cb].astype(y_ref.dtype)

    @pl.when(j == nchunks - 1)
    def _():
        for cb in range(LRU_BLOCKS):
            hfin_ref[:, cb * LANES:(cb + 1) * LANES] = st_scr[cb][:, 0, :]


def _block_scan(a_scr, b_scr, h_scr, st_scr, rev, nb, tc):
    nblk = tc // SUBLANES
    row = lax.broadcasted_iota(jnp.int32, (1, SUBLANES, LANES), 1)
    for cb in range(LRU_BLOCKS):
        a = a_scr[cb].reshape(nb * nblk, SUBLANES, LANES)
        b = b_scr[cb].reshape(nb * nblk, SUBLANES, LANES)
        for s in (1, 2, 4):
            shift, keep = (SUBLANES - s, row < SUBLANES - s) if rev else (s, row >= s)
            b = b + a * jnp.where(keep, pltpu.roll(b, shift, 1), 0.0)
            a = a * jnp.where(keep, pltpu.roll(a, shift, 1), 1.0)
        a = a.reshape(nb, nblk, SUBLANES, LANES)
        b = b.reshape(nb, nblk, SUBLANES, LANES)
        carry = st_scr[cb]
        last = 0 if rev else SUBLANES - 1
        for k in (reversed(range(nblk)) if rev else range(nblk)):
            h = a[:, k] * carry + b[:, k]
            h_scr[cb, :, k * SUBLANES:(k + 1) * SUBLANES, :] = h
            carry = jnp.broadcast_to(h[:, last:last + 1, :], h.shape)
        st_scr[cb] = carry


def _lru(x, h0, lp):
    b, t, w = x.shape
    nb = SUBLANES
    tc = min(t, 128)
    nchunks = t // tc
    hb = tc // BF16_ROWS
    nhb = t // BF16_ROWS

    def chunk(di, j):
        return jnp.where(di == 0, j, nchunks - 1 - j)

    smat = np.zeros((CONV_W * tc, tc + 2 * BF16_ROWS), np.float32)
    for k in range(CONV_W):
        smat[k * tc + np.arange(tc), BF16_ROWS + np.arange(tc) + k - CONV_W // 2] = 1.0

    vec = lambda g, di, j: (0, 0)
    dvec = lambda g, di, j: (di, 0, 0)
    return pl.pallas_call(
        functools.partial(_lru_kernel, tc=tc, nchunks=nchunks, nb=nb),
        out_shape=[jax.ShapeDtypeStruct((2, b, t, w), BF16), jax.ShapeDtypeStruct((2, b, w), F32)],
        grid=(b // nb, 2, nchunks),
        in_specs=[
            pl.BlockSpec((nb, tc, w), lambda g, di, j: (g, chunk(di, j), 0)),
            pl.BlockSpec((nb, BF16_ROWS, w), lambda g, di, j: (g, jnp.maximum(chunk(di, j) * hb - 1, 0), 0)),
            pl.BlockSpec((nb, BF16_ROWS, w), lambda g, di, j: (g, jnp.minimum((chunk(di, j) + 1) * hb, nhb - 1), 0)),
            pl.BlockSpec((None, nb, w), lambda g, di, j: (di, g, 0)),
            pl.BlockSpec(smat.shape, vec),
            pl.BlockSpec((CONV_W, w), vec),
            pl.BlockSpec((1, w), vec),
            pl.BlockSpec((None, LRU_BLOCKS, LANES, 2 * LANES), lambda g, di, j: (di, 0, 0, 0)),
            pl.BlockSpec((None, 1, w), dvec),
            pl.BlockSpec((None, 1, w), dvec),
            pl.BlockSpec((None, 1, w), dvec),
        ],
        out_specs=[
            pl.BlockSpec((None, nb, tc, w), lambda g, di, j: (di, g, chunk(di, j), 0)),
            pl.BlockSpec((None, nb, w), lambda g, di, j: (di, g, 0)),
        ],
        scratch_shapes=[
            pltpu.VMEM((nb, tc, w), F32),
            pltpu.VMEM((LRU_BLOCKS, nb * tc, LANES), F32),
            pltpu.VMEM((LRU_BLOCKS, nb * tc, LANES), F32),
            pltpu.VMEM((LRU_BLOCKS, nb, tc, LANES), F32),
            pltpu.VMEM((LRU_BLOCKS, nb, SUBLANES, LANES), F32),
        ],
        compiler_params=_cp("parallel", "arbitrary", "arbitrary"),
        name="rglru",
    )(x, x, x, h0, jnp.asarray(smat, BF16), lp["conv_w"], lp["conv_b"], lp["wab"], lp["ba"], lp["bx"], lp["lam"])


def _softmax_pv(s_parts, v_parts):
    m = functools.reduce(jnp.maximum, [jnp.max(s, axis=-1, keepdims=True) for s in s_parts])
    p_parts = [jnp.exp(s - m) for s in s_parts]
    l = functools.reduce(jnp.add, [jnp.sum(p, axis=-1, keepdims=True) for p in p_parts])
    o = functools.reduce(jnp.add, [jnp.dot(p.astype(BF16), v, preferred_element_type=F32)
                                   for p, v in zip(p_parts, v_parts)])
    return o / l


def _pair_rows(x2):
    low = lax.broadcasted_iota(jnp.int32, x2.shape, 1) < HEAD_DIM
    zero = jnp.zeros((), x2.dtype)
    return jnp.concatenate([jnp.where(low, x2, zero), jnp.where(low, zero, x2)], axis=0)


def _softmax_rows(s):
    p = jnp.exp(s - jnp.max(s, axis=-1, keepdims=True))
    return p.astype(BF16), jnp.sum(p, axis=-1, keepdims=True)


def _na_kernel(var_ref, q_ref, k_ref, v_ref, kc_ref, vc_ref, bias_ref, o_ref, s_scr, *, band, win_rows, kh, rows):
    del var_ref
    r0 = pl.program_id(1) * band
    ws = jnp.clip(r0 - kh // 2, 0, rows - win_rows)
    start = pl.multiple_of(ws * GRID_W, GRID_W)
    nwin = win_rows * GRID_W
    m = band * GRID_W
    npairs = q_ref.shape[-1] // LANES
    low = lax.broadcasted_iota(jnp.int32, (m, LANES), 1) < HEAD_DIM

    def scores(hp):
        sl = slice(hp * LANES, (hp + 1) * LANES)
        qs = _pair_rows(q_ref[:, sl] * jnp.asarray(HEAD_DIM ** -0.5, BF16))
        s_lat = lax.dot_general(qs, k_ref[pl.ds(start, nwin), sl], _NT, preferred_element_type=F32)
        s_scr[hp % 2, :, 0:nwin] = s_lat + bias_ref[2 * hp:2 * hp + 2].reshape(2 * m, nwin)
        s_scr[hp % 2, :, nwin:] = lax.dot_general(qs, kc_ref[:, sl], _NT, preferred_element_type=F32)

    def finish(hp):
        sl = slice(hp * LANES, (hp + 1) * LANES)
        p, l = _softmax_rows(s_scr[hp % 2])
        o = jnp.dot(p[:, :nwin], v_ref[pl.ds(start, nwin), sl], preferred_element_type=F32)
        o = (o + jnp.dot(p[:, nwin:], vc_ref[:, sl], preferred_element_type=F32)) / l
        o_ref[:, sl] = jnp.where(low, o[:m], o[m:]).astype(o_ref.dtype)

    scores(0)
    for hp in range(npairs):
        if hp + 1 < npairs:
            scores(hp + 1)
        finish(hp)


def _na_bands(rows, band, win_rows):
    kh = min(NA_KH_MAX, rows)
    variants, var_of_band = [], []
    for r0 in range(0, rows, band):
        ws = int(np.clip(r0 - kh // 2, 0, rows - win_rows))
        key = (r0 - ws, tuple(int(np.clip(r0 + rr - kh // 2, 0, rows - kh)) - ws for rr in range(band)))
        if key not in variants:
            variants.append(key)
        var_of_band.append(variants.index(key))
    return variants, np.asarray(var_of_band, np.int32)


def _na_bias_table(rpb, rows, band, win_rows):
    kh = min(NA_KH_MAX, rows)
    col = np.arange(GRID_W)
    cs = np.clip(col - NA_KW // 2, 0, GRID_W - NA_KW)
    col_mask = (col[None, :] >= cs[:, None]) & (col[None, :] < cs[:, None] + NA_KW)
    dcol = np.clip(col[None, :] - col[:, None], -(NA_KW - 1), NA_KW - 1) + NA_KW - 1
    heads, n_drow, n_dcol = rpb.shape
    onehot = jnp.asarray(dcol.reshape(-1)[None, :] == np.arange(n_dcol)[:, None], F32)
    t2 = jnp.dot(rpb.reshape(heads * n_drow, n_dcol).astype(F32), onehot, precision=lax.Precision.HIGHEST)
    t2 = t2.reshape(heads, n_drow, GRID_W, GRID_W)
    t2 = jnp.where(col_mask[None, None], t2, NEG_INF)
    masked = jnp.full((heads, GRID_W, GRID_W), NEG_INF, F32)
    variants, _ = _na_bands(rows, band, win_rows)
    tabs = []
    for delta, first in variants:
        q_rows = []
        for rr in range(band):
            blocks = [t2[:, j - delta - rr + NA_KH_MAX - 1] if first[rr] <= j < first[rr] + kh else masked
                      for j in range(win_rows)]
            q_rows.append(jnp.concatenate(blocks, axis=-1))
        tabs.append(jnp.concatenate(q_rows, axis=1))
    return jnp.stack(tabs)


def _na(q, k, v, kc, vc, rpb):
    b, s, w = q.shape
    l = kc.shape[1]
    heads = w // HEAD_DIM
    rows = s // GRID_W
    kh = min(NA_KH_MAX, rows)
    band = min(4, rows)
    win_rows = min(-(-(kh + band - 1) // 2) * 2, rows)
    m, nwin = band * GRID_W, win_rows * GRID_W
    bias = _na_bias_table(rpb, rows, band, win_rows)
    _, var_of_band = _na_bands(rows, band, win_rows)

    full = lambda bi, r, var: (bi, 0, 0)
    grid_spec = pltpu.PrefetchScalarGridSpec(
        num_scalar_prefetch=1,
        grid=(b, rows // band),
        in_specs=[
            pl.BlockSpec((None, m, w), lambda bi, r, var: (bi, r, 0)),
            pl.BlockSpec((None, s, w), full),
            pl.BlockSpec((None, s, w), full),
            pl.BlockSpec((None, l, w), full),
            pl.BlockSpec((None, l, w), full),
            pl.BlockSpec((None, heads, m, nwin), lambda bi, r, var: (var[r], 0, 0, 0)),
        ],
        out_specs=pl.BlockSpec((None, m, w), lambda bi, r, var: (bi, r, 0)),
        scratch_shapes=[pltpu.VMEM((2, 2 * m, nwin + l), F32)],
    )
    return pl.pallas_call(
        functools.partial(_na_kernel, band=band, win_rows=win_rows, kh=kh, rows=rows),
        out_shape=jax.ShapeDtypeStruct((b, s, w), BF16),
        grid_spec=grid_spec,
        compiler_params=_cp("parallel", "arbitrary"),
        name="natten",
    )(jnp.asarray(var_of_band), q, k, v, kc, vc, bias)


def _head_norm(x, g, gmat):
    cols = []
    for c in range(x.shape[1] // LANES):
        xc = x[:, c * LANES:(c + 1) * LANES]
        x2 = xc * xc
        hi = x2.astype(BF16)
        lo = (x2 - hi.astype(F32)).astype(BF16)
        ms = jnp.dot(hi, gmat, preferred_element_type=F32) + jnp.dot(lo, gmat, preferred_element_type=F32)
        cols.append(xc * lax.rsqrt(ms + EPS) * g)
    return cols[0] if len(cols) == 1 else jnp.concatenate(cols, axis=-1)


def _rope(x, cos, sin):
    half = HEAD_DIM // 2
    lane = lax.broadcasted_iota(jnp.int32, (x.shape[0], LANES), 1)
    low = (lane % HEAD_DIM) < half
    cols = []
    for c in range(x.shape[1] // LANES):
        xc = x[:, c * LANES:(c + 1) * LANES]
        sw = jnp.where(low, pltpu.roll(xc, LANES - half, 1), pltpu.roll(xc, half, 1))
        cols.append(xc * cos + sw * sin)
    return cols[0] if len(cols) == 1 else jnp.concatenate(cols, axis=-1)


def _attn_kernel(*refs, norm, rope, has_lat, group, tq):
    refs = list(refs)
    q_ref, kc_ref, vc_ref = refs[:3]
    refs = refs[3:]
    if has_lat:
        kl_ref, vl_ref = refs[:2]
        refs = refs[2:]
    if norm:
        qg_ref, kg_ref, gmat_ref = refs[:3]
        refs = refs[3:]
    if rope:
        cosq_ref, sinq_ref, cosk_ref, sink_ref = refs[:4]
        refs = refs[4:]
    o_ref = refs[0]
    refs = refs[1:]
    if norm:
        kc_scr = refs[0]
        refs = refs[1:]
        if has_lat:
            kl_scr = refs[0]

    if norm:
        @pl.when(pl.program_id(1) == 0)
        def _():
            kc_scr[...] = _head_norm(kc_ref[...].astype(F32), kg_ref[...], gmat_ref[...]).astype(BF16)
            if has_lat:
                kl = _head_norm(kl_ref[...].astype(F32), kg_ref[...], gmat_ref[...])
                if rope:
                    kl = _rope(kl, cosk_ref[...], sink_ref[...])
                kl_scr[...] = kl.astype(BF16)
        kc_src = kc_scr
        kl_src = kl_scr if has_lat else None
    else:
        kc_src = kc_ref
        kl_src = kl_ref if has_lat else None

    q = q_ref[...].astype(F32)
    if norm:
        q = _head_norm(q, qg_ref[...], gmat_ref[...])
    if rope:
        q = _rope(q, cosq_ref[...], sinq_ref[...])
    q = (q * (HEAD_DIM ** -0.5)).astype(BF16)

    n_kv = kc_ref.shape[1] // HEAD_DIM
    outs = []
    for g in range(n_kv):
        sl = slice(g * HEAD_DIM, (g + 1) * HEAD_DIM)
        qs = [q[:, (g * group + jj) * HEAD_DIM:(g * group + jj + 1) * HEAD_DIM] for jj in range(group)]
        qs = qs[0] if group == 1 else jnp.concatenate(qs, axis=0)
        s_parts = [lax.dot_general(qs, kc_src[:, sl], _NT, preferred_element_type=F32)]
        v_parts = [vc_ref[:, sl]]
        if has_lat:
            s_parts.append(lax.dot_general(qs, kl_src[:, sl], _NT, preferred_element_type=F32))
            v_parts.append(vl_ref[:, sl])
        o = _softmax_pv(s_parts, v_parts)
        for jj in range(group):
            outs.append(o[jj * tq:(jj + 1) * tq])
    o_ref[...] = jnp.concatenate(outs, axis=-1).astype(o_ref.dtype)


def _dup_heads(x):
    low = lax.broadcasted_iota(jnp.int32, x.shape, 1) < HEAD_DIM
    sw = pltpu.roll(x, HEAD_DIM, 1)
    return jnp.where(low, x, sw), jnp.where(low, sw, x)


def _gqa_kernel(q_ref, kc_ref, vc_ref, kl_ref, vl_ref, qg_ref, kg_ref, gmat_ref, cosq_ref, sinq_ref, cosk_ref,
                sink_ref, o_ref, kc_scr, kl_scr, vc_scr, vl_scr, s_scr, *, sub):
    n_ctx = kc_ref.shape[0]
    n_kv = GQA_KV_HEADS

    @pl.when(pl.program_id(1) == 0)
    def _():
        gmat = gmat_ref[...]
        kc = _head_norm(kc_ref[...].astype(F32), kg_ref[...], gmat)
        kl = _rope(_head_norm(kl_ref[...].astype(F32), kg_ref[...], gmat), cosk_ref[...], sink_ref[...])
        for scr, val in ((kc_scr, kc), (kl_scr, kl), (vc_scr, vc_ref[...].astype(F32)),
                         (vl_scr, vl_ref[...].astype(F32))):
            for g, dup in enumerate(_dup_heads(val)):
                scr[g] = dup.astype(BF16)

    tq = q_ref.shape[0]
    chunks = []
    for c in range(q_ref.shape[1] // LANES):
        qc = q_ref[:, c * LANES:(c + 1) * LANES].astype(F32)
        qc = _rope(_head_norm(qc, qg_ref[...], gmat_ref[...]), cosq_ref[...], sinq_ref[...])
        chunks.append((qc * (HEAD_DIM ** -0.5)).astype(BF16))
    pairs_per_group = len(chunks) // n_kv
    low = lax.broadcasted_iota(jnp.int32, (sub, LANES), 1) < HEAD_DIM
    units = [(t, g) for t in range(tq // sub) for g in range(n_kv)]

    def scores(u):
        t, g = units[u]
        rows = [_pair_rows(chunks[g * pairs_per_group + j][t * sub:(t + 1) * sub]) for j in range(pairs_per_group)]
        qs = jnp.concatenate(rows, axis=0)
        s_scr[u % 2, :, 0:n_ctx] = lax.dot_general(qs, kc_scr[g], _NT, preferred_element_type=F32)
        s_scr[u % 2, :, n_ctx:] = lax.dot_general(qs, kl_scr[g], _NT, preferred_element_type=F32)

    def finish(u):
        t, g = units[u]
        p, l = _softmax_rows(s_scr[u % 2])
        o = jnp.dot(p[:, :n_ctx], vc_scr[g], preferred_element_type=F32)
        o = (o + jnp.dot(p[:, n_ctx:], vl_scr[g], preferred_element_type=F32)) / l
        for j in range(pairs_per_group):
            c = g * pairs_per_group + j
            lo, hi = o[2 * j * sub:(2 * j + 1) * sub], o[(2 * j + 1) * sub:(2 * j + 2) * sub]
            o_ref[t * sub:(t + 1) * sub, c * LANES:(c + 1) * LANES] = jnp.where(low, lo, hi).astype(o_ref.dtype)

    scores(0)
    for u in range(len(units)):
        if u + 1 < len(units):
            scores(u + 1)
        finish(u)


def _gqa(q, kc, vc, kl, vl, qg, kg):
    b, t, wq = q.shape
    l, wk = kc.shape[1], kc.shape[2]
    s = kl.shape[1]
    assert wk == GQA_KV_HEADS * HEAD_DIM == LANES
    tq = min(t, 256)
    sub = min(tq, 128)
    m_unit = 2 * (wq // LANES // GQA_KV_HEADS) * sub
    reps = LANES // HEAD_DIM
    blk = np.arange(LANES) // HEAD_DIM
    gmat = jnp.asarray((blk[:, None] == blk[None, :]).astype(np.float32) / HEAD_DIM, BF16)
    cos, sin = _rope_tables(t)
    full = lambda bi, i: (bi, 0, 0)
    const = lambda bi, i: (0, 0)
    return pl.pallas_call(
        functools.partial(_gqa_kernel, sub=sub),
        out_shape=jax.ShapeDtypeStruct((b, t, wq), BF16),
        grid=(b, t // tq),
        in_specs=[
            pl.BlockSpec((None, tq, wq), lambda bi, i: (bi, i, 0)),
            pl.BlockSpec((None, l, wk), full), pl.BlockSpec((None, l, wk), full),
            pl.BlockSpec((None, s, wk), full), pl.BlockSpec((None, s, wk), full),
            pl.BlockSpec((1, LANES), const), pl.BlockSpec((1, LANES), const), pl.BlockSpec((LANES, LANES), const),
            pl.BlockSpec((tq, LANES), lambda bi, i: (i, 0)), pl.BlockSpec((tq, LANES), lambda bi, i: (i, 0)),
            pl.BlockSpec((s, LANES), const), pl.BlockSpec((s, LANES), const),
        ],
        out_specs=pl.BlockSpec((None, tq, wq), lambda bi, i: (bi, i, 0)),
        scratch_shapes=[
            pltpu.VMEM((GQA_KV_HEADS, l, LANES), BF16), pltpu.VMEM((GQA_KV_HEADS, s, LANES), BF16),
            pltpu.VMEM((GQA_KV_HEADS, l, LANES), BF16), pltpu.VMEM((GQA_KV_HEADS, s, LANES), BF16),
            pltpu.VMEM((2, m_unit, l + s), F32),
        ],
        compiler_params=_cp("parallel", "arbitrary"),
        name="gqa",
    )(q, kc, vc, kl, vl, jnp.tile(qg, reps).reshape(1, LANES), jnp.tile(kg, reps).reshape(1, LANES), gmat,
      cos, sin, cos, sin)


def _rope_tables(n_tokens):
    t = np.arange(n_tokens)
    pos = np.stack([t // GRID_W, t % GRID_W], axis=-1).astype(np.float32)
    n_freq = HEAD_DIM // 4
    inv_freq = jnp.asarray(ROPE_BASE, F32) ** (-jnp.arange(n_freq, dtype=F32) / n_freq)
    ang = (jnp.asarray(pos)[:, :, None] * inv_freq).reshape(n_tokens, 2 * n_freq)
    cos, sin = jnp.cos(ang), jnp.sin(ang)
    reps = LANES // HEAD_DIM
    return jnp.tile(jnp.concatenate([cos, cos], -1), (1, reps)), jnp.tile(jnp.concatenate([-sin, sin], -1), (1, reps))


def _attn(q, kc, vc, kl=None, vl=None, qg=None, kg=None, rope=False):
    b, t, wq = q.shape
    l, wk = kc.shape[1], kc.shape[2]
    has_lat = kl is not None
    norm = qg is not None
    group = wq // wk
    tq = min(t, 128)
    full = lambda bi, i: (bi, 0, 0)
    const = lambda bi, i: (0, 0)
    args = [q, kc, vc]
    in_specs = [
        pl.BlockSpec((None, tq, wq), lambda bi, i: (bi, i, 0)),
        pl.BlockSpec((None, l, wk), full),
        pl.BlockSpec((None, l, wk), full),
    ]
    scratch = []
    if has_lat:
        s = kl.shape[1]
        args += [kl, vl]
        in_specs += [pl.BlockSpec((None, s, wk), full)] * 2
    if norm:
        reps = LANES // HEAD_DIM
        blk = np.arange(LANES) // HEAD_DIM
        gmat = jnp.asarray((blk[:, None] == blk[None, :]).astype(np.float32) / HEAD_DIM, BF16)
        args += [jnp.tile(qg, reps).reshape(1, LANES), jnp.tile(kg, reps).reshape(1, LANES), gmat]
        in_specs += [pl.BlockSpec((1, LANES), const)] * 2 + [pl.BlockSpec((LANES, LANES), const)]
        scratch.append(pltpu.VMEM((l, wk), BF16))
        if has_lat:
            scratch.append(pltpu.VMEM((kl.shape[1], wk), BF16))
    if rope:
        cos, sin = _rope_tables(t)
        args += [cos, sin, cos, sin]
        in_specs += [pl.BlockSpec((tq, LANES), lambda bi, i: (i, 0))] * 2 + [pl.BlockSpec((t, LANES), const)] * 2
    return pl.pallas_call(
        functools.partial(_attn_kernel, norm=norm, rope=rope, has_lat=has_lat, group=group, tq=tq),
        out_shape=jax.ShapeDtypeStruct((b, t, wq), BF16),
        grid=(b, t // tq),
        in_specs=in_specs,
        out_specs=pl.BlockSpec((None, tq, wq), lambda bi, i: (bi, i, 0)),
        scratch_shapes=scratch,
        compiler_params=_cp("parallel", "arbitrary"),
        name="attn",
    )(*args)


def _merge_kernel(hf_ref, hr_ref, gl_ref, yb_ref, yc_ref, gt_ref, x_ref, g1_ref, sh2_ref, sc2_ref, ng1_ref, ng2_ref,
                  woa_ref, wob_ref, woc_ref, wout_ref, x1_ref, h2_ref):
    d = x_ref.shape[-1]
    ya = (hf_ref[...].astype(F32) + hr_ref[...].astype(F32)) * jax.nn.gelu(gl_ref[...].astype(F32))
    pa = jnp.dot(ya.astype(BF16), woa_ref[...], preferred_element_type=F32)
    pb = jnp.dot(yb_ref[...], wob_ref[...], preferred_element_type=F32)
    pc = jnp.dot(yc_ref[...], woc_ref[...], preferred_element_type=F32)
    ga = _sigmoid(gt_ref[:, 0:d].astype(F32))
    gb = _sigmoid(gt_ref[:, d:2 * d].astype(F32))
    gc = _sigmoid(gt_ref[:, 2 * d:3 * d].astype(F32))
    m = ga * pa + gb * pb + gc * pc
    y = jnp.dot(m.astype(BF16), wout_ref[...], preferred_element_type=F32)
    x1 = x_ref[...] + g1_ref[...] * _rms(y, ng1_ref[...])
    x1_ref[...] = x1
    h2_ref[...] = (_rms(x1, ng2_ref[...]) * (1.0 + sc2_ref[...]) + sh2_ref[...]).astype(h2_ref.dtype)


def _merge(h, gl, yb, yc, gates, x, g1, sh2, sc2, ng1, ng2, woa, wob, woc, wout, mod_row):
    b, t, d = x.shape
    tm = min(t, 256)
    wl, wb, wc = gl.shape[2], yb.shape[2], yc.shape[2]
    if mod_row is None:
        mod_map = lambda bi, i: (bi, 0, 0)
    else:
        mod_map = lambda bi, i: (mod_row, 0, 0)
    tok = lambda w: pl.BlockSpec((None, tm, w), lambda bi, i: (bi, i, 0))
    mod = pl.BlockSpec((None, 1, d), mod_map)
    const = lambda r, c: pl.BlockSpec((r, c), lambda bi, i: (0, 0))
    return pl.pallas_call(
        _merge_kernel,
        out_shape=[jax.ShapeDtypeStruct((b, t, d), F32), jax.ShapeDtypeStruct((b, t, d), BF16)],
        grid=(b, t // tm),
        in_specs=[
            pl.BlockSpec((None, None, tm, wl), lambda bi, i: (0, bi, i, 0)),
            pl.BlockSpec((None, None, tm, wl), lambda bi, i: (1, bi, i, 0)),
            tok(wl), tok(wb), tok(wc), tok(N_BRANCH * d), tok(d),
            mod, mod, mod, const(1, d), const(1, d),
            const(wl, d), const(wb, d), const(wc, d), const(d, d),
        ],
        out_specs=[tok(d), tok(d)],
        compiler_params=_cp("parallel", "parallel"),
        name="merge",
    )(h, h, gl, yb, yc, gates, x, g1, sh2, sc2, ng1.reshape(1, d), ng2.reshape(1, d), woa, wob, woc, wout)


def _lane_cumsum(x, tri):
    e, n = x.shape
    carry = jnp.zeros((e, 1), F32)
    cols = []
    for c in range(n // LANES):
        part = jnp.dot(x[:, c * LANES:(c + 1) * LANES].astype(BF16), tri, preferred_element_type=F32) + carry
        cols.append(part)
        carry = part[:, LANES - 1:LANES]
    return cols[0] if len(cols) == 1 else jnp.concatenate(cols, axis=-1)


def _route_kernel(h_ref, wr_ref, tri_ref, pos_ref, gate_ref, *, cap):
    logits = lax.dot_general(wr_ref[...], h_ref[...], _NT, preferred_element_type=F32)
    z = jnp.exp(logits - jnp.max(logits, axis=0, keepdims=True))
    aff = z / jnp.sum(z, axis=0, keepdims=True)
    gate_ref[...] = aff
    bits = pltpu.bitcast(aff, jnp.int32)
    e = bits.shape[0]

    def bisect(_, lohi):
        lo, hi = lohi
        mid = lo + ((hi - lo + 1) >> 1)
        cnt = jnp.sum(jnp.where(bits >= mid, 1.0, 0.0), axis=1, keepdims=True)
        ok = cnt >= cap
        return jnp.where(ok, mid, lo), jnp.where(ok, hi, mid - 1)

    one_bits = 0x3F800000
    lo, _ = lax.fori_loop(0, 31, bisect, (jnp.zeros((e, 1), jnp.int32), jnp.full((e, 1), one_bits, jnp.int32)))
    gt = jnp.where(bits > lo, 1.0, 0.0)
    eq = jnp.where(bits == lo, 1.0, 0.0)
    need = cap - jnp.sum(gt, axis=1, keepdims=True)
    tri = tri_ref[...]
    sel = gt + jnp.where(_lane_cumsum(eq, tri) <= need, eq, 0.0)
    pos = _lane_cumsum(sel, tri) - 1.0
    pos_ref[...] = jnp.where(sel > 0.0, pos, -1.0).astype(jnp.int32)


def _route(h2, w_router_t, cap):
    b, n, d = h2.shape
    e = w_router_t.shape[0]
    tri = jnp.asarray(np.triu(np.ones((LANES, LANES), np.float32)), BF16)
    return pl.pallas_call(
        functools.partial(_route_kernel, cap=cap),
        out_shape=[jax.ShapeDtypeStruct((b, e, n), jnp.int32), jax.ShapeDtypeStruct((b, e, n), F32)],
        grid=(b,),
        in_specs=[
            pl.BlockSpec((None, n, d), lambda bi: (bi, 0, 0)),
            pl.BlockSpec((e, d), lambda bi: (0, 0)),
            pl.BlockSpec((LANES, LANES), lambda bi: (0, 0)),
        ],
        out_specs=[pl.BlockSpec((None, e, n), lambda bi: (bi, 0, 0))] * 2,
        compiler_params=_cp("parallel"),
        name="route",
    )(h2, w_router_t, tri)


def _expert_kernel(h_ref, pos_ref, gate_ref, tok_ref, wg_ref, wu_ref, wd_ref, o_ref, *, cap, nbs):
    e = pl.program_id(1)
    n = h_ref.shape[1]

    @pl.when(e == 0)
    def _():
        o_ref[...] = jnp.zeros_like(o_ref)

    slot = lax.broadcasted_iota(jnp.int32, (cap, n), 0)
    hs, gcol, onehots = [], [], []
    for s in range(nbs):
        pos = pos_ref[s, pl.ds(e, 1), :]
        gate = gate_ref[s, pl.ds(e, 1), :]
        hit = pos == slot
        pb = jnp.where(hit, 1.0, 0.0).astype(BF16)
        hs.append(jnp.dot(pb, h_ref[s], preferred_element_type=F32).astype(BF16))
        gcol.append(jnp.sum(jnp.where(hit, gate, 0.0), axis=1, keepdims=True))
        onehots.append(pb)
    hs = hs[0] if nbs == 1 else jnp.concatenate(hs, axis=0)
    gcol = gcol[0] if nbs == 1 else jnp.concatenate(gcol, axis=0)
    a = jnp.dot(hs, wg_ref[...], preferred_element_type=F32)
    u = (a * _sigmoid(a)) * jnp.dot(hs, wu_ref[...], preferred_element_type=F32)
    y = (jnp.dot(u.astype(BF16), wd_ref[...], preferred_element_type=F32) * gcol).astype(BF16)

    rc = min(n, 256)
    rows = lax.broadcasted_iota(jnp.int32, (rc, cap), 0)
    for s in range(nbs):
        digits = lax.dot_general(tok_ref[...], onehots[s], _NT, preferred_element_type=F32)
        idx = (digits[0:1] * GRID_W + digits[1:2]).astype(jnp.int32)
        ys = y[s * cap:(s + 1) * cap]
        for c in range(n // rc):
            pt = jnp.where(idx == rows + c * rc, 1.0, 0.0).astype(BF16)
            o_ref[s, c * rc:(c + 1) * rc, :] += jnp.dot(pt, ys, preferred_element_type=F32)


def _experts(h2, pos, gate, wg, wu, wd, cap):
    b, n, d = h2.shape
    e, _, f = wg.shape
    nbs = max(1, min(b, 256 // cap))
    t = np.arange(n)
    tok = np.zeros((SUBLANES, n), np.float32)
    tok[0], tok[1] = t // GRID_W, t % GRID_W
    sample = lambda bi, ei: (bi, 0, 0)
    return pl.pallas_call(
        functools.partial(_expert_kernel, cap=cap, nbs=nbs),
        out_shape=jax.ShapeDtypeStruct((b, n, d), F32),
        grid=(b // nbs, e),
        in_specs=[
            pl.BlockSpec((nbs, n, d), sample),
            pl.BlockSpec((nbs, e, n), sample),
            pl.BlockSpec((nbs, e, n), sample),
            pl.BlockSpec((SUBLANES, n), lambda bi, ei: (0, 0)),
            pl.BlockSpec((None, d, f), lambda bi, ei: (ei, 0, 0)),
            pl.BlockSpec((None, d, f), lambda bi, ei: (ei, 0, 0)),
            pl.BlockSpec((None, f, d), lambda bi, ei: (ei, 0, 0)),
        ],
        out_specs=pl.BlockSpec((nbs, n, d), sample),
        compiler_params=_cp("parallel", "arbitrary"),
        name="experts",
    )(h2, pos, gate, jnp.asarray(tok, BF16), wg, wu, wd)


def _resid_kernel(x_ref, y_ref, g_ref, ng_ref, o_ref):
    o_ref[...] = x_ref[...] + g_ref[...] * _rms(y_ref[...], ng_ref[...])


def _resid(x, y, g2, ng, mod_row):
    b, t, d = x.shape
    tm = min(t, 512)
    if mod_row is None:
        mod_map = lambda bi, i: (bi, 0, 0)
    else:
        mod_map = lambda bi, i: (mod_row, 0, 0)
    tok = pl.BlockSpec((None, tm, d), lambda bi, i: (bi, i, 0))
    return pl.pallas_call(
        _resid_kernel,
        out_shape=jax.ShapeDtypeStruct((b, t, d), F32),
        grid=(b, t // tm),
        in_specs=[tok, tok, pl.BlockSpec((None, 1, d), mod_map), pl.BlockSpec((1, d), lambda bi, i: (0, 0))],
        out_specs=tok,
        compiler_params=_cp("parallel", "parallel"),
        name="resid",
    )(x, y, g2, ng.reshape(1, d))


def _split_cols(w, widths):
    offs = np.cumsum((0,) + tuple(widths))
    return [w[:, int(offs[i]):int(offs[i + 1])] for i in range(len(widths))]


def _layer(xl, xc, mods, p, need_ctx):
    bsz, s, d = xl.shape
    ctx_row = bsz
    sh1, sc1, g1, sh2, sc2, g2 = mods
    ng = p["norm_g"]
    na_w = (d // 128) * HEAD_DIM
    gq_w = (d // 128) * HEAD_DIM
    gkv_w = GQA_KV_HEADS * HEAD_DIM
    splits = (d, d, na_w, na_w, na_w, gq_w, gkv_w, gkv_w, N_BRANCH * d)
    w_in = p["w_in"].astype(BF16)

    lx, lg, nq, nk, nv, gq, gk, gv, gates = _inproj(xl, sh1, sc1, ng[0], w_in, splits, None)
    if need_ctx:
        cx, cg, cnq, cnk, cnv, cgq, cgk, cgv, cgates = _inproj(xc, sh1, sc1, ng[0], w_in, splits, ctx_row)
    else:
        parts = _split_cols(w_in, splits)
        kv_parts = (0, 3, 4, 6, 7)
        w_kv = jnp.concatenate([parts[i] for i in kv_parts], axis=1)
        cx, cnk, cnv, cgk, cgv = _inproj(xc, sh1, sc1, ng[0], w_kv, tuple(splits[i] for i in kv_parts), ctx_row)

    lp = {
        "conv_w": p["conv_w"], "conv_b": p["conv_b"].reshape(1, d),
        "wab": jnp.concatenate([p["lru_wa"], p["lru_wx"]], axis=-1).astype(BF16),
        "ba": p["lru_ba"].reshape(2, 1, d), "bx": p["lru_bx"].reshape(2, 1, d),
        "lam": p["lru_lam"].reshape(2, 1, d),
    }
    hc, hfin = _lru(cx, jnp.zeros((2, bsz, d), F32), lp)
    hl, _ = _lru(lx, hfin, lp)

    yb = _na(nq, nk, nv, cnk, cnv, p["na_rpb"])
    yc = _gqa(gq, cgk, cgv, gk, gv, p["qn_g"], p["kn_g"])

    woa, wob, woc, wout = (p[k].astype(BF16) for k in ("w_o_a", "w_o_b", "w_o_c", "w_out"))
    wr_t = p["w_router"].T.astype(BF16)
    wg, wu, wd = (p[k].astype(BF16) for k in ("w_gate", "w_up", "w_down"))

    def ffn(x1, h2, mod_row):
        n = x1.shape[1]
        cap = EC_CAPACITY * n // N_EXPERTS
        pos, gate = _route(h2, wr_t, cap)
        moe = _experts(h2, pos, gate, wg, wu, wd, cap)
        return _resid(x1, moe, g2, ng[3], mod_row)

    x1, h2 = _merge(hl, lg, yb, yc, gates, xl, g1, sh2, sc2, ng[1], ng[2], woa, wob, woc, wout, None)
    xl = ffn(x1, h2, None)
    if need_ctx:
        ybc = _attn(cnq, cnk, cnv)
        ycc = _attn(cgq, cgk, cgv, qg=p["qn_g"], kg=p["kn_g"])
        x1c, h2c = _merge(hc, cg, ybc, ycc, cgates, xc, g1, sh2, sc2, ng[1], ng[2], woa, wob, woc, wout, ctx_row)
        xc = ffn(x1c, h2c, ctx_row)
    return xl, xc


def kernel(x, c, ctx, c_ctx, w_mod, b_mod, norm_g, w_in, conv_w, conv_b, lru_wa, lru_ba, lru_wx, lru_bx, lru_lam,
           na_rpb, qn_g, kn_g, w_o_a, w_o_b, w_o_c, w_out, w_router, w_gate, w_up, w_down):
    bsz, _, d = x.shape
    depth = w_mod.shape[0]
    rows = -(-(bsz + 1) // SUBLANES) * SUBLANES
    cond = jnp.zeros((rows, d), F32).at[:bsz].set(c).at[bsz].set(c_ctx)
    xl, xc = x, ctx
    for l in range(depth):
        p = {
            "norm_g": norm_g[l], "w_in": w_in[l], "conv_w": conv_w[l], "conv_b": conv_b[l],
            "lru_wa": lru_wa[l], "lru_ba": lru_ba[l], "lru_wx": lru_wx[l], "lru_bx": lru_bx[l],
            "lru_lam": lru_lam[l], "na_rpb": na_rpb[l], "qn_g": qn_g[l], "kn_g": kn_g[l],
            "w_o_a": w_o_a[l], "w_o_b": w_o_b[l], "w_o_c": w_o_c[l], "w_out": w_out[l],
            "w_router": w_router[l], "w_gate": w_gate[l], "w_up": w_up[l], "w_down": w_down[l],
        }
        m = _adaln(cond, w_mod[l], b_mod[l])
        mods = [m[:, i * d:(i + 1) * d].reshape(rows, 1, d) for i in range(6)]
        xl, xc = _layer(xl, xc, mods, p, l < depth - 1)
    return xl
```

```python
import functools

import jax
import jax.numpy as jnp
import numpy as np
from jax import lax
from jax.experimental import pallas as pl
from jax.experimental.pallas import tpu as pltpu

F32 = jnp.float32
BF16 = jnp.bfloat16

GRID_W = 64
HEAD_DIM = 64
LRU_BLOCKS = 8
CONV_W = 4
LRU_C = 8.0
NA_KH_MAX = 8
NA_KW = 16
GQA_KV_HEADS = 2
ROPE_BASE = 10000.0
N_EXPERTS = 16
EC_CAPACITY = 2
N_BRANCH = 3
EPS = 1e-6
NEG_INF = -1e30

LANES = 128
SUBLANES = 8
BF16_ROWS = 16
VMEM_LIMIT_BYTES = 56 * 1024 * 1024

_NT = (((1,), (1,)), ((), ()))


def _cp(*sem):
    return pltpu.CompilerParams(dimension_semantics=sem, vmem_limit_bytes=VMEM_LIMIT_BYTES)


def _sigmoid(x):
    return 0.5 * jnp.tanh(0.5 * x) + 0.5


def _rms(x, g):
    return x * lax.rsqrt(jnp.mean(x * x, axis=-1, keepdims=True) + EPS) * g


def _adaln_kernel(c_ref, w_ref, b_ref, o_ref):
    c = c_ref[...]
    s = (c * _sigmoid(c)).astype(BF16)
    o_ref[...] = jnp.dot(s, w_ref[...].astype(BF16), preferred_element_type=F32) + b_ref[...]


def _adaln(cond, w_mod, b_mod):
    r, d = cond.shape
    n = w_mod.shape[1]
    tn = n // 4
    return pl.pallas_call(
        _adaln_kernel,
        out_shape=jax.ShapeDtypeStruct((r, n), F32),
        grid=(n // tn,),
        in_specs=[
            pl.BlockSpec((r, d), lambda j: (0, 0)),
            pl.BlockSpec((d, tn), lambda j: (0, j)),
            pl.BlockSpec((1, tn), lambda j: (0, j)),
        ],
        out_specs=pl.BlockSpec((r, tn), lambda j: (0, j)),
        compiler_params=_cp("parallel"),
        name="adaln",
    )(cond, w_mod, b_mod.reshape(1, n))


def _inproj_kernel(x_ref, sh_ref, sc_ref, g_ref, w_ref, *o_refs, splits):
    h = (_rms(x_ref[...], g_ref[...]) * (1.0 + sc_ref[...]) + sh_ref[...]).astype(BF16)
    off = 0
    for o_ref, width in zip(o_refs, splits):
        o_ref[...] = jnp.dot(h, w_ref[:, off:off + width], preferred_element_type=F32).astype(o_ref.dtype)
        off += width


def _inproj(x, shift, scale, gain, w, splits, mod_row):
    b, t, d = x.shape
    tm = min(t, 512)
    n = w.shape[1]
    if mod_row is None:
        mod_map = lambda bi, i: (bi, 0, 0)
    else:
        mod_map = lambda bi, i: (mod_row, 0, 0)
    return pl.pallas_call(
        functools.partial(_inproj_kernel, splits=splits),
        out_shape=[jax.ShapeDtypeStruct((b, t, s), BF16) for s in splits],
        grid=(b, t // tm),
        in_specs=[
            pl.BlockSpec((None, tm, d), lambda bi, i: (bi, i, 0)),
            pl.BlockSpec((None, 1, d), mod_map),
            pl.BlockSpec((None, 1, d), mod_map),
            pl.BlockSpec((1, d), lambda bi, i: (0, 0)),
            pl.BlockSpec((d, n), lambda bi, i: (0, 0)),
        ],
        out_specs=[pl.BlockSpec((None, tm, s), lambda bi, i: (bi, i, 0)) for s in splits],
        compiler_params=_cp("parallel", "parallel"),
        name="inproj",
    )(x, shift, scale, gain.reshape(1, d), w)


def _lru_kernel(x_ref, xp_ref, xn_ref, h0_ref, smat_ref, cw_ref, cb_ref, wab_ref, ba_ref, bx_ref, lam_ref,
                y_ref, hfin_ref, u_scr, a_scr, b_scr, h_scr, st_scr, *, tc, nchunks, nb):
    d = pl.program_id(1)
    j = pl.program_id(2)
    c = jnp.where(d == 0, j, nchunks - 1 - j)
    w = x_ref.shape[-1]
    cwide = 2 * LANES

    @pl.when(j == 0)
    def _():
        for cb in range(LRU_BLOCKS):
            h0 = h0_ref[:, cb * LANES:(cb + 1) * LANES]
            st_scr[cb] = jnp.broadcast_to(h0[:, None, :], (nb, SUBLANES, LANES))

    zero = jnp.zeros((), BF16)
    c_lam = -LRU_C * jax.nn.softplus(-lam_ref[...])
    for cc in range(w // cwide):
        sl2 = slice(cc * cwide, (cc + 1) * cwide)
        for bb in range(nb):
            xe = jnp.concatenate([jnp.where(c > 0, xp_ref[bb, :, sl2], zero), x_ref[bb, :, sl2],
                                  jnp.where(c < nchunks - 1, xn_ref[bb, :, sl2], zero)], axis=0)
            sh = jnp.dot(smat_ref[...], xe, preferred_element_type=F32)
            u = cb_ref[:, sl2]
            for k in range(CONV_W):
                u = u + sh[k * tc:(k + 1) * tc] * cw_ref[k:k + 1, sl2]
            u_scr[bb, :, sl2] = u
        for cb in range(cc * (cwide // LANES), (cc + 1) * (cwide // LANES)):
            sl = slice(cb * LANES, (cb + 1) * LANES)
            u = u_scr[:, :, sl].reshape(nb * tc, LANES)
            gates = jnp.dot(u.astype(BF16), wab_ref[cb], preferred_element_type=F32)
            r = _sigmoid(gates[:, :LANES] + ba_ref[:, sl])
            i = _sigmoid(gates[:, LANES:] + bx_ref[:, sl])
            log_a = c_lam[:, sl] * r
            a = jnp.exp(log_a)
            a_scr[cb] = a
            m = -jnp.tanh(log_a) * (a * a + 1.0)
            b_scr[cb] = jnp.where(m > 0.0, m * lax.rsqrt(m), 0.0) * (i * u)

    @pl.when(d == 0)
    def _():
        _block_scan(a_scr, b_scr, h_scr, st_scr, False, nb, tc)

    @pl.when(d == 1)
    def _():
        _block_scan(a_scr, b_scr, h_scr, st_scr, True, nb, tc)

    for cb in range(LRU_BLOCKS):
        y_ref[:, :, cb * LANES:(cb + 1) * LANES] = h_scr[cb].astype(y_ref.dtype)

    @pl.when(j == nchunks - 1)
    def _():
        for cb in range(LRU_BLOCKS):
            hfin_ref[:, cb * LANES:(cb + 1) * LANES] = st_scr[cb][:, 0, :]


def _block_scan(a_scr, b_scr, h_scr, st_scr, rev, nb, tc):
    nblk = tc // SUBLANES
    row = lax.broadcasted_iota(jnp.int32, (1, SUBLANES, LANES), 1)
    for cb in range(LRU_BLOCKS):
        a = a_scr[cb].reshape(nb * nblk, SUBLANES, LANES)
        b = b_scr[cb].reshape(nb * nblk, SUBLANES, LANES)
        for s in (1, 2, 4):
            shift, keep = (SUBLANES - s, row < SUBLANES - s) if rev else (s, row >= s)
            b = b + a * jnp.where(keep, pltpu.roll(b, shift, 1), 0.0)
            a = a * jnp.where(keep, pltpu.roll(a, shift, 1), 1.0)
        a = a.reshape(nb, nblk, SUBLANES, LANES)
        b = b.reshape(nb, nblk, SUBLANES, LANES)
        carry = st_scr[cb]
        last = 0 if rev else SUBLANES - 1
        for k in (reversed(range(nblk)) if rev else range(nblk)):
            h = a[:, k] * carry + b[:, k]
            h_scr[cb, :, k * SUBLANES:(k + 1) * SUBLANES, :] = h
            carry = jnp.broadcast_to(h[:, last:last + 1, :], h.shape)
        st_scr[cb] = carry


def _lru(x, h0, lp):
    b, t, w = x.shape
    nb = SUBLANES
    tc = min(t, 128)
    nchunks = t // tc
    hb = tc // BF16_ROWS
    nhb = t // BF16_ROWS

    def chunk(di, j):
        return jnp.where(di == 0, j, nchunks - 1 - j)

    smat = np.zeros((CONV_W * tc, tc + 2 * BF16_ROWS), np.float32)
    for k in range(CONV_W):
        smat[k * tc + np.arange(tc), BF16_ROWS + np.arange(tc) + k - CONV_W // 2] = 1.0

    vec = lambda g, di, j: (0, 0)
    dvec = lambda g, di, j: (di, 0, 0)
    return pl.pallas_call(
        functools.partial(_lru_kernel, tc=tc, nchunks=nchunks, nb=nb),
        out_shape=[jax.ShapeDtypeStruct((2, b, t, w), BF16), jax.ShapeDtypeStruct((2, b, w), F32)],
        grid=(b // nb, 2, nchunks),
        in_specs=[
            pl.BlockSpec((nb, tc, w), lambda g, di, j: (g, chunk(di, j), 0)),
            pl.BlockSpec((nb, BF16_ROWS, w), lambda g, di, j: (g, jnp.maximum(chunk(di, j) * hb - 1, 0), 0)),
            pl.BlockSpec((nb, BF16_ROWS, w), lambda g, di, j: (g, jnp.minimum((chunk(di, j) + 1) * hb, nhb - 1), 0)),
            pl.BlockSpec((None, nb, w), lambda g, di, j: (di, g, 0)),
            pl.BlockSpec(smat.shape, vec),
            pl.BlockSpec((CONV_W, w), vec),
            pl.BlockSpec((1, w), vec),
            pl.BlockSpec((None, LRU_BLOCKS, LANES, 2 * LANES), lambda g, di, j: (di, 0, 0, 0)),
            pl.BlockSpec((None, 1, w), dvec),
            pl.BlockSpec((None, 1, w), dvec),
            pl.BlockSpec((None, 1, w), dvec),
        ],
        out_specs=[
            pl.BlockSpec((None, nb, tc, w), lambda g, di, j: (di, g, chunk(di, j), 0)),
            pl.BlockSpec((None, nb, w), lambda g, di, j: (di, g, 0)),
        ],
        scratch_shapes=[
            pltpu.VMEM((nb, tc, w), F32),
            pltpu.VMEM((LRU_BLOCKS, nb * tc, LANES), F32),
            pltpu.VMEM((LRU_BLOCKS, nb * tc, LANES), F32),
            pltpu.VMEM((LRU_BLOCKS, nb, tc, LANES), F32),
            pltpu.VMEM((LRU_BLOCKS, nb, SUBLANES, LANES), F32),
        ],
        compiler_params=_cp("parallel", "arbitrary", "arbitrary"),
        name="rglru",
    )(x, x, x, h0, jnp.asarray(smat, BF16), lp["conv_w"], lp["conv_b"], lp["wab"], lp["ba"], lp["bx"], lp["lam"])


def _softmax_pv(s_parts, v_parts):
    m = functools.reduce(jnp.maximum, [jnp.max(s, axis=-1, keepdims=True) for s in s_parts])
    p_parts = [jnp.exp(s - m) for s in s_parts]
    l = functools.reduce(jnp.add, [jnp.sum(p, axis=-1, keepdims=True) for p in p_parts])
    o = functools.reduce(jnp.add, [jnp.dot(p.astype(BF16), v, preferred_element_type=F32)
                                   for p, v in zip(p_parts, v_parts)])
    return o / l


def _pair_rows(x2):
    low = lax.broadcasted_iota(jnp.int32, x2.shape, 1) < HEAD_DIM
    zero = jnp.zeros((), x2.dtype)
    return jnp.concatenate([jnp.where(low, x2, zero), jnp.where(low, zero, x2)], axis=0)


def _softmax_rows(s):
    p = jnp.exp(s - jnp.max(s, axis=-1, keepdims=True))
    return p.astype(BF16), jnp.sum(p, axis=-1, keepdims=True)


def _na_kernel(var_ref, q_ref, k_ref, v_ref, kc_ref, vc_ref, bias_ref, o_ref, s_scr, *, band, win_rows, kh, rows):
    del var_ref
    r0 = pl.program_id(1) * band
    ws = jnp.clip(r0 - kh // 2, 0, rows - win_rows)
    start = pl.multiple_of(ws * GRID_W, GRID_W)
    nwin = win_rows * GRID_W
    m = band * GRID_W
    npairs = q_ref.shape[-1] // LANES
    low = lax.broadcasted_iota(jnp.int32, (m, LANES), 1) < HEAD_DIM

    def scores(hp):
        sl = slice(hp * LANES, (hp + 1) * LANES)
        qs = _pair_rows(q_ref[:, sl] * jnp.asarray(HEAD_DIM ** -0.5, BF16))
        s_lat = lax.dot_general(qs, k_ref[pl.ds(start, nwin), sl], _NT, preferred_element_type=F32)
        s_scr[hp % 2, :, 0:nwin] = s_lat + bias_ref[2 * hp:2 * hp + 2].reshape(2 * m, nwin)
        s_scr[hp % 2, :, nwin:] = lax.dot_general(qs, kc_ref[:, sl], _NT, preferred_element_type=F32)

    def finish(hp):
        sl = slice(hp * LANES, (hp + 1) * LANES)
        p, l = _softmax_rows(s_scr[hp % 2])
        o = jnp.dot(p[:, :nwin], v_ref[pl.ds(start, nwin), sl], preferred_element_type=F32)
        o = (o + jnp.dot(p[:, nwin:], vc_ref[:, sl], preferred_element_type=F32)) / l
        o_ref[:, sl] = jnp.where(low, o[:m], o[m:]).astype(o_ref.dtype)

    scores(0)
    for hp in range(npairs):
        if hp + 1 < npairs:
            scores(hp + 1)
        finish(hp)


def _na_bands(rows, band, win_rows):
    kh = min(NA_KH_MAX, rows)
    variants, var_of_band = [], []
    for r0 in range(0, rows, band):
        ws = int(np.clip(r0 - kh // 2, 0, rows - win_rows))
        key = (r0 - ws, tuple(int(np.clip(r0 + rr - kh // 2, 0, rows - kh)) - ws for rr in range(band)))
        if key not in variants:
            variants.append(key)
        var_of_band.append(variants.index(key))
    return variants, np.asarray(var_of_band, np.int32)


def _na_bias_table(rpb, rows, band, win_rows):
    kh = min(NA_KH_MAX, rows)
    col = np.arange(GRID_W)
    cs = np.clip(col - NA_KW // 2, 0, GRID_W - NA_KW)
    col_mask = (col[None, :] >= cs[:, None]) & (col[None, :] < cs[:, None] + NA_KW)
    dcol = np.clip(col[None, :] - col[:, None], -(NA_KW - 1), NA_KW - 1) + NA_KW - 1
    heads, n_drow, n_dcol = rpb.shape
    onehot = jnp.asarray(dcol.reshape(-1)[None, :] == np.arange(n_dcol)[:, None], F32)
    t2 = jnp.dot(rpb.reshape(heads * n_drow, n_dcol).astype(F32), onehot, precision=lax.Precision.HIGHEST)
    t2 = t2.reshape(heads, n_drow, GRID_W, GRID_W)
    t2 = jnp.where(col_mask[None, None], t2, NEG_INF)
    masked = jnp.full((heads, GRID_W, GRID_W), NEG_INF, F32)
    variants, _ = _na_bands(rows, band, win_rows)
    tabs = []
    for delta, first in variants:
        q_rows = []
        for rr in range(band):
            blocks = [t2[:, j - delta - rr + NA_KH_MAX - 1] if first[rr] <= j < first[rr] + kh else masked
                      for j in range(win_rows)]
            q_rows.append(jnp.concatenate(blocks, axis=-1))
        tabs.append(jnp.concatenate(q_rows, axis=1))
    return jnp.stack(tabs)


def _na(q, k, v, kc, vc, rpb):
    b, s, w = q.shape
    l = kc.shape[1]
    heads = w // HEAD_DIM
    rows = s // GRID_W
    kh = min(NA_KH_MAX, rows)
    band = min(4, rows)
    win_rows = min(-(-(kh + band - 1) // 2) * 2, rows)
    m, nwin = band * GRID_W, win_rows * GRID_W
    bias = _na_bias_table(rpb, rows, band, win_rows)
    _, var_of_band = _na_bands(rows, band, win_rows)

    full = lambda bi, r, var: (bi, 0, 0)
    grid_spec = pltpu.PrefetchScalarGridSpec(
        num_scalar_prefetch=1,
        grid=(b, rows // band),
        in_specs=[
            pl.BlockSpec((None, m, w), lambda bi, r, var: (bi, r, 0)),
            pl.BlockSpec((None, s, w), full),
            pl.BlockSpec((None, s, w), full),
            pl.BlockSpec((None, l, w), full),
            pl.BlockSpec((None, l, w), full),
            pl.BlockSpec((None, heads, m, nwin), lambda bi, r, var: (var[r], 0, 0, 0)),
        ],
        out_specs=pl.BlockSpec((None, m, w), lambda bi, r, var: (bi, r, 0)),
        scratch_shapes=[pltpu.VMEM((2, 2 * m, nwin + l), F32)],
    )
    return pl.pallas_call(
        functools.partial(_na_kernel, band=band, win_rows=win_rows, kh=kh, rows=rows),
        out_shape=jax.ShapeDtypeStruct((b, s, w), BF16),
        grid_spec=grid_spec,
        compiler_params=_cp("parallel", "arbitrary"),
        name="natten",
    )(jnp.asarray(var_of_band), q, k, v, kc, vc, bias)


def _head_norm(x, g, gmat):
    cols = []
    for c in range(x.shape[1] // LANES):
        xc = x[:, c * LANES:(c + 1) * LANES]
        x2 = xc * xc
        hi = x2.astype(BF16)
        lo = (x2 - hi.astype(F32)).astype(BF16)
        ms = jnp.dot(hi, gmat, preferred_element_type=F32) + jnp.dot(lo, gmat, preferred_element_type=F32)
        cols.append(xc * lax.rsqrt(ms + EPS) * g)
    return cols[0] if len(cols) == 1 else jnp.concatenate(cols, axis=-1)


def _rope(x, cos, sin):
    half = HEAD_DIM // 2
    lane = lax.broadcasted_iota(jnp.int32, (x.shape[0], LANES), 1)
    low = (lane % HEAD_DIM) < half
    cols = []
    for c in range(x.shape[1] // LANES):
        xc = x[:, c * LANES:(c + 1) * LANES]
        sw = jnp.where(low, pltpu.roll(xc, LANES - half, 1), pltpu.roll(xc, half, 1))
        cols.append(xc * cos + sw * sin)
    return cols[0] if len(cols) == 1 else jnp.concatenate(cols, axis=-1)


def _attn_kernel(*refs, norm, rope, has_lat, group, tq):
    refs = list(refs)
    q_ref, kc_ref, vc_ref = refs[:3]
    refs = refs[3:]
    if has_lat:
        kl_ref, vl_ref = refs[:2]
        refs = refs[2:]
    if norm:
        qg_ref, kg_ref, gmat_ref = refs[:3]
        refs = refs[3:]
    if rope:
        cosq_ref, sinq_ref, cosk_ref, sink_ref = refs[:4]
        refs = refs[4:]
    o_ref = refs[0]
    refs = refs[1:]
    if norm:
        kc_scr = refs[0]
        refs = refs[1:]
        if has_lat:
            kl_scr = refs[0]

    if norm:
        @pl.when(pl.program_id(1) == 0)
        def _():
            kc_scr[...] = _head_norm(kc_ref[...].astype(F32), kg_ref[...], gmat_ref[...]).astype(BF16)
            if has_lat:
                kl = _head_norm(kl_ref[...].astype(F32), kg_ref[...], gmat_ref[...])
                if rope:
                    kl = _rope(kl, cosk_ref[...], sink_ref[...])
                kl_scr[...] = kl.astype(BF16)
        kc_src = kc_scr
        kl_src = kl_scr if has_lat else None
    else:
        kc_src = kc_ref
        kl_src = kl_ref if has_lat else None

    q = q_ref[...].astype(F32)
    if norm:
        q = _head_norm(q, qg_ref[...], gmat_ref[...])
    if rope:
        q = _rope(q, cosq_ref[...], sinq_ref[...])
    q = (q * (HEAD_DIM ** -0.5)).astype(BF16)

    n_kv = kc_ref.shape[1] // HEAD_DIM
    outs = []
    for g in range(n_kv):
        sl = slice(g * HEAD_DIM, (g + 1) * HEAD_DIM)
        qs = [q[:, (g * group + jj) * HEAD_DIM:(g * group + jj + 1) * HEAD_DIM] for jj in range(group)]
        qs = qs[0] if group == 1 else jnp.concatenate(qs, axis=0)
        s_parts = [lax.dot_general(qs, kc_src[:, sl], _NT, preferred_element_type=F32)]
        v_parts = [vc_ref[:, sl]]
        if has_lat:
            s_parts.append(lax.dot_general(qs, kl_src[:, sl], _NT, preferred_element_type=F32))
            v_parts.append(vl_ref[:, sl])
        o = _softmax_pv(s_parts, v_parts)
        for jj in range(group):
            outs.append(o[jj * tq:(jj + 1) * tq])
    o_ref[...] = jnp.concatenate(outs, axis=-1).astype(o_ref.dtype)


def _dup_heads(x):
    low = lax.broadcasted_iota(jnp.int32, x.shape, 1) < HEAD_DIM
    sw = pltpu.roll(x, HEAD_DIM, 1)
    return jnp.where(low, x, sw), jnp.where(low, sw, x)


def _gqa_kernel(q_ref, kc_ref, vc_ref, kl_ref, vl_ref, qg_ref, kg_ref, gmat_ref, cosq_ref, sinq_ref, cosk_ref,
                sink_ref, o_ref, kc_scr, kl_scr, vc_scr, vl_scr, s_scr, *, sub):
    n_ctx = kc_ref.shape[0]
    n_kv = GQA_KV_HEADS

    @pl.when(pl.program_id(1) == 0)
    def _():
        gmat = gmat_ref[...]
        kc = _head_norm(kc_ref[...].astype(F32), kg_ref[...], gmat)
        kl = _rope(_head_norm(kl_ref[...].astype(F32), kg_ref[...], gmat), cosk_ref[...], sink_ref[...])
        for scr, val in ((kc_scr, kc), (kl_scr, kl), (vc_scr, vc_ref[...].astype(F32)),
                         (vl_scr, vl_ref[...].astype(F32))):
            for g, dup in enumerate(_dup_heads(val)):
                scr[g] = dup.astype(BF16)

    tq = q_ref.shape[0]
    chunks = []
    for c in range(q_ref.shape[1] // LANES):
        qc = q_ref[:, c * LANES:(c + 1) * LANES].astype(F32)
        qc = _rope(_head_norm(qc, qg_ref[...], gmat_ref[...]), cosq_ref[...], sinq_ref[...])
        chunks.append((qc * (HEAD_DIM ** -0.5)).astype(BF16))
    pairs_per_group = len(chunks) // n_kv
    low = lax.broadcasted_iota(jnp.int32, (sub, LANES), 1) < HEAD_DIM
    units = [(t, g) for t in range(tq // sub) for g in range(n_kv)]

    def scores(u):
        t, g = units[u]
        rows = [_pair_rows(chunks[g * pairs_per_group + j][t * sub:(t + 1) * sub]) for j in range(pairs_per_group)]
        qs = jnp.concatenate(rows, axis=0)
        s_scr[u % 2, :, 0:n_ctx] = lax.dot_general(qs, kc_scr[g], _NT, preferred_element_type=F32)
        s_scr[u % 2, :, n_ctx:] = lax.dot_general(qs, kl_scr[g], _NT, preferred_element_type=F32)

    def finish(u):
        t, g = units[u]
        p, l = _softmax_rows(s_scr[u % 2])
        o = jnp.dot(p[:, :n_ctx], vc_scr[g], preferred_element_type=F32)
        o = (o + jnp.dot(p[:, n_ctx:], vl_scr[g], preferred_element_type=F32)) / l
        for j in range(pairs_per_group):
            c = g * pairs_per_group + j
            lo, hi = o[2 * j * sub:(2 * j + 1) * sub], o[(2 * j + 1) * sub:(2 * j + 2) * sub]
            o_ref[t * sub:(t + 1) * sub, c * LANES:(c + 1) * LANES] = jnp.where(low, lo, hi).astype(o_ref.dtype)

    scores(0)
    for u in range(len(units)):
        if u + 1 < len(units):
            scores(u + 1)
        finish(u)


def _gqa(q, kc, vc, kl, vl, qg, kg):
    b, t, wq = q.shape
    l, wk = kc.shape[1], kc.shape[2]
    s = kl.shape[1]
    assert wk == GQA_KV_HEADS * HEAD_DIM == LANES
    tq = min(t, 256)
    sub = min(tq, 128)
    m_unit = 2 * (wq // LANES // GQA_KV_HEADS) * sub
    reps = LANES // HEAD_DIM
    blk = np.arange(LANES) // HEAD_DIM
    gmat = jnp.asarray((blk[:, None] == blk[None, :]).astype(np.float32) / HEAD_DIM, BF16)
    cos, sin = _rope_tables(t)
    full = lambda bi, i: (bi, 0, 0)
    const = lambda bi, i: (0, 0)
    return pl.pallas_call(
        functools.partial(_gqa_kernel, sub=sub),
        out_shape=jax.ShapeDtypeStruct((b, t, wq), BF16),
        grid=(b, t // tq),
        in_specs=[
            pl.BlockSpec((None, tq, wq), lambda bi, i: (bi, i, 0)),
            pl.BlockSpec((None, l, wk), full), pl.BlockSpec((None, l, wk), full),
            pl.BlockSpec((None, s, wk), full), pl.BlockSpec((None, s, wk), full),
            pl.BlockSpec((1, LANES), const), pl.BlockSpec((1, LANES), const), pl.BlockSpec((LANES, LANES), const),
            pl.BlockSpec((tq, LANES), lambda bi, i: (i, 0)), pl.BlockSpec((tq, LANES), lambda bi, i: (i, 0)),
            pl.BlockSpec((s, LANES), const), pl.BlockSpec((s, LANES), const),
        ],
        out_specs=pl.BlockSpec((None, tq, wq), lambda bi, i: (bi, i, 0)),
        scratch_shapes=[
            pltpu.VMEM((GQA_KV_HEADS, l, LANES), BF16), pltpu.VMEM((GQA_KV_HEADS, s, LANES), BF16),
            pltpu.VMEM((GQA_KV_HEADS, l, LANES), BF16), pltpu.VMEM((GQA_KV_HEADS, s, LANES), BF16),
            pltpu.VMEM((2, m_unit, l + s), F32),
        ],
        compiler_params=_cp("parallel", "arbitrary"),
        name="gqa",
    )(q, kc, vc, kl, vl, jnp.tile(qg, reps).reshape(1, LANES), jnp.tile(kg, reps).reshape(1, LANES), gmat,
      cos, sin, cos, sin)


def _rope_tables(n_tokens):
    t = np.arange(n_tokens)
    pos = np.stack([t // GRID_W, t % GRID_W], axis=-1).astype(np.float32)
    n_freq = HEAD_DIM // 4
    inv_freq = jnp.asarray(ROPE_BASE, F32) ** (-jnp.arange(n_freq, dtype=F32) / n_freq)
    ang = (jnp.asarray(pos)[:, :, None] * inv_freq).reshape(n_tokens, 2 * n_freq)
    cos, sin = jnp.cos(ang), jnp.sin(ang)
    reps = LANES // HEAD_DIM
    return jnp.tile(jnp.concatenate([cos, cos], -1), (1, reps)), jnp.tile(jnp.concatenate([-sin, sin], -1), (1, reps))


def _attn(q, kc, vc, kl=None, vl=None, qg=None, kg=None, rope=False):
    b, t, wq = q.shape
    l, wk = kc.shape[1], kc.shape[2]
    has_lat = kl is not None
    norm = qg is not None
    group = wq // wk
    tq = min(t, 128)
    full = lambda bi, i: (bi, 0, 0)
    const = lambda bi, i: (0, 0)
    args = [q, kc, vc]
    in_specs = [
        pl.BlockSpec((None, tq, wq), lambda bi, i: (bi, i, 0)),
        pl.BlockSpec((None, l, wk), full),
        pl.BlockSpec((None, l, wk), full),
    ]
    scratch = []
    if has_lat:
        s = kl.shape[1]
        args += [kl, vl]
        in_specs += [pl.BlockSpec((None, s, wk), full)] * 2
    if norm:
        reps = LANES // HEAD_DIM
        blk = np.arange(LANES) // HEAD_DIM
        gmat = jnp.asarray((blk[:, None] == blk[None, :]).astype(np.float32) / HEAD_DIM, BF16)
        args += [jnp.tile(qg, reps).reshape(1, LANES), jnp.tile(kg, reps).reshape(1, LANES), gmat]
        in_specs += [pl.BlockSpec((1, LANES), const)] * 2 + [pl.BlockSpec((LANES, LANES), const)]
        scratch.append(pltpu.VMEM((l, wk), BF16))
        if has_lat:
            scratch.append(pltpu.VMEM((kl.shape[1], wk), BF16))
    if rope:
        cos, sin = _rope_tables(t)
        args += [cos, sin, cos, sin]
        in_specs += [pl.BlockSpec((tq, LANES), lambda bi, i: (i, 0))] * 2 + [pl.BlockSpec((t, LANES), const)] * 2
    return pl.pallas_call(
        functools.partial(_attn_kernel, norm=norm, rope=rope, has_lat=has_lat, group=group, tq=tq),
        out_shape=jax.ShapeDtypeStruct((b, t, wq), BF16),
        grid=(b, t // tq),
        in_specs=in_specs,
        out_specs=pl.BlockSpec((None, tq, wq), lambda bi, i: (bi, i, 0)),
        scratch_shapes=scratch,
        compiler_params=_cp("parallel", "arbitrary"),
        name="attn",
    )(*args)


def _merge_kernel(hf_ref, hr_ref, gl_ref, yb_ref, yc_ref, gt_ref, x_ref, g1_ref, sh2_ref, sc2_ref, ng1_ref, ng2_ref,
                  woa_ref, wob_ref, woc_ref, wout_ref, x1_ref, h2_ref):
    d = x_ref.shape[-1]
    ya = (hf_ref[...].astype(F32) + hr_ref[...].astype(F32)) * jax.nn.gelu(gl_ref[...].astype(F32))
    pa = jnp.dot(ya.astype(BF16), woa_ref[...], preferred_element_type=F32)
    pb = jnp.dot(yb_ref[...], wob_ref[...], preferred_element_type=F32)
    pc = jnp.dot(yc_ref[...], woc_ref[...], preferred_element_type=F32)
    ga = _sigmoid(gt_ref[:, 0:d].astype(F32))
    gb = _sigmoid(gt_ref[:, d:2 * d].astype(F32))
    gc = _sigmoid(gt_ref[:, 2 * d:3 * d].astype(F32))
    m = ga * pa + gb * pb + gc * pc
    y = jnp.dot(m.astype(BF16), wout_ref[...], preferred_element_type=F32)
    x1 = x_ref[...] + g1_ref[...] * _rms(y, ng1_ref[...])
    x1_ref[...] = x1
    h2_ref[...] = (_rms(x1, ng2_ref[...]) * (1.0 + sc2_ref[...]) + sh2_ref[...]).astype(h2_ref.dtype)


def _merge(h, gl, yb, yc, gates, x, g1, sh2, sc2, ng1, ng2, woa, wob, woc, wout, mod_row):
    b, t, d = x.shape
    tm = min(t, 256)
    wl, wb, wc = gl.shape[2], yb.shape[2], yc.shape[2]
    if mod_row is None:
        mod_map = lambda bi, i: (bi, 0, 0)
    else:
        mod_map = lambda bi, i: (mod_row, 0, 0)
    tok = lambda w: pl.BlockSpec((None, tm, w), lambda bi, i: (bi, i, 0))
    mod = pl.BlockSpec((None, 1, d), mod_map)
    const = lambda r, c: pl.BlockSpec((r, c), lambda bi, i: (0, 0))
    return pl.pallas_call(
        _merge_kernel,
        out_shape=[jax.ShapeDtypeStruct((b, t, d), F32), jax.ShapeDtypeStruct((b, t, d), BF16)],
        grid=(b, t // tm),
        in_specs=[
            pl.BlockSpec((None, None, tm, wl), lambda bi, i: (0, bi, i, 0)),
            pl.BlockSpec((None, None, tm, wl), lambda bi, i: (1, bi, i, 0)),
            tok(wl), tok(wb), tok(wc), tok(N_BRANCH * d), tok(d),
            mod, mod, mod, const(1, d), const(1, d),
            const(wl, d), const(wb, d), const(wc, d), const(d, d),
        ],
        out_specs=[tok(d), tok(d)],
        compiler_params=_cp("parallel", "parallel"),
        name="merge",
    )(h, h, gl, yb, yc, gates, x, g1, sh2, sc2, ng1.reshape(1, d), ng2.reshape(1, d), woa, wob, woc, wout)


def _lane_cumsum(x, tri):
    e, n = x.shape
    carry = jnp.zeros((e, 1), F32)
    cols = []
    for c in range(n // LANES):
        part = jnp.dot(x[:, c * LANES:(c + 1) * LANES].astype(BF16), tri, preferred_element_type=F32) + carry
        cols.append(part)
        carry = part[:, LANES - 1:LANES]
    return cols[0] if len(cols) == 1 else jnp.concatenate(cols, axis=-1)


def _route_kernel(h_ref, wr_ref, tri_ref, pos_ref, gate_ref, *, cap):
    logits = lax.dot_general(wr_ref[...], h_ref[...], _NT, preferred_element_type=F32)
    z = jnp.exp(logits - jnp.max(logits, axis=0, keepdims=True))
    aff = z / jnp.sum(z, axis=0, keepdims=True)
    gate_ref[...] = aff
    bits = pltpu.bitcast(aff, jnp.int32)
    e = bits.shape[0]

    def bisect(_, lohi):
        lo, hi = lohi
        mid = lo + ((hi - lo + 1) >> 1)
        cnt = jnp.sum(jnp.where(bits >= mid, 1.0, 0.0), axis=1, keepdims=True)
        ok = cnt >= cap
        return jnp.where(ok, mid, lo), jnp.where(ok, hi, mid - 1)

    one_bits = 0x3F800000
    lo, _ = lax.fori_loop(0, 31, bisect, (jnp.zeros((e, 1), jnp.int32), jnp.full((e, 1), one_bits, jnp.int32)))
    gt = jnp.where(bits > lo, 1.0, 0.0)
    eq = jnp.where(bits == lo, 1.0, 0.0)
    need = cap - jnp.sum(gt, axis=1, keepdims=True)
    tri = tri_ref[...]
    sel = gt + jnp.where(_lane_cumsum(eq, tri) <= need, eq, 0.0)
    pos = _lane_cumsum(sel, tri) - 1.0
    pos_ref[...] = jnp.where(sel > 0.0, pos, -1.0).astype(jnp.int32)


def _route(h2, w_router_t, cap):
    b, n, d = h2.shape
    e = w_router_t.shape[0]
    tri = jnp.asarray(np.triu(np.ones((LANES, LANES), np.float32)), BF16)
    return pl.pallas_call(
        functools.partial(_route_kernel, cap=cap),
        out_shape=[jax.ShapeDtypeStruct((b, e, n), jnp.int32), jax.ShapeDtypeStruct((b, e, n), F32)],
        grid=(b,),
        in_specs=[
            pl.BlockSpec((None, n, d), lambda bi: (bi, 0, 0)),
            pl.BlockSpec((e, d), lambda bi: (0, 0)),
            pl.BlockSpec((LANES, LANES), lambda bi: (0, 0)),
        ],
        out_specs=[pl.BlockSpec((None, e, n), lambda bi: (bi, 0, 0))] * 2,
        compiler_params=_cp("parallel"),
        name="route",
    )(h2, w_router_t, tri)


def _expert_kernel(h_ref, pos_ref, gate_ref, tok_ref, wg_ref, wu_ref, wd_ref, x_ref, g2_ref, ng_ref, o_ref,
                   *, cap, nbs, per_sample_gate):
    e = pl.program_id(1)
    n = h_ref.shape[1]

    @pl.when(e == 0)
    def _():
        o_ref[...] = jnp.zeros_like(o_ref)

    slot = lax.broadcasted_iota(jnp.int32, (cap, n), 0)
    hs, gcol, idx = [], [], []
    for s in range(nbs):
        pos = pos_ref[s, pl.ds(e, 1), :]
        gate = gate_ref[s, pl.ds(e, 1), :]
        hit = pos == slot
        pb = jnp.where(hit, 1.0, 0.0).astype(BF16)
        digits = lax.dot_general(tok_ref[...], pb, _NT, preferred_element_type=F32)
        idx.append((digits[0:1] * GRID_W + digits[1:2]).astype(jnp.int32))
        hs.append(jnp.dot(pb, h_ref[s], preferred_element_type=F32).astype(BF16))
        gcol.append(jnp.sum(jnp.where(hit, gate, 0.0), axis=1, keepdims=True))
    hs = hs[0] if nbs == 1 else jnp.concatenate(hs, axis=0)
    gcol = gcol[0] if nbs == 1 else jnp.concatenate(gcol, axis=0)
    a = jnp.dot(hs, wg_ref[...], preferred_element_type=F32)
    u = (a * _sigmoid(a)) * jnp.dot(hs, wu_ref[...], preferred_element_type=F32)
    y = (jnp.dot(u.astype(BF16), wd_ref[...], preferred_element_type=F32) * gcol).astype(BF16)

    rc = min(n, 256)
    rows = lax.broadcasted_iota(jnp.int32, (rc, cap), 0)
    for s in range(nbs):
        ys = y[s * cap:(s + 1) * cap]
        for c in range(n // rc):
            pt = jnp.where(idx[s] == rows + c * rc, 1.0, 0.0).astype(BF16)
            o_ref[s, c * rc:(c + 1) * rc, :] += jnp.dot(pt, ys, preferred_element_type=F32)

    @pl.when(e == pl.num_programs(1) - 1)
    def _():
        for s in range(nbs):
            g2 = g2_ref[s if per_sample_gate else 0]
            for c in range(n // rc):
                sl = slice(c * rc, (c + 1) * rc)
                o_ref[s, sl, :] = x_ref[s, sl, :] + g2 * _rms(o_ref[s, sl, :], ng_ref[...])


def _experts(h2, pos, gate, wg, wu, wd, cap, x1, g2, ng, mod_row):
    b, n, d = h2.shape
    e, _, f = wg.shape
    nbs = max(1, min(b, 256 // cap))
    t = np.arange(n)
    tok = np.zeros((SUBLANES, n), np.float32)
    tok[0], tok[1] = t // GRID_W, t % GRID_W
    sample = lambda bi, ei: (bi, 0, 0)
    if mod_row is None:
        g2_spec = pl.BlockSpec((nbs, 1, d), sample)
    else:
        g2_spec = pl.BlockSpec((1, 1, d), lambda bi, ei: (mod_row, 0, 0))
    once = pl.Buffered(1)
    return pl.pallas_call(
        functools.partial(_expert_kernel, cap=cap, nbs=nbs, per_sample_gate=mod_row is None),
        out_shape=jax.ShapeDtypeStruct((b, n, d), F32),
        grid=(b // nbs, e),
        in_specs=[
            pl.BlockSpec((nbs, n, d), sample, pipeline_mode=once),
            pl.BlockSpec((nbs, e, n), sample),
            pl.BlockSpec((nbs, e, n), sample),
            pl.BlockSpec((SUBLANES, n), lambda bi, ei: (0, 0)),
            pl.BlockSpec((None, d, f), lambda bi, ei: (ei, 0, 0)),
            pl.BlockSpec((None, d, f), lambda bi, ei: (ei, 0, 0)),
            pl.BlockSpec((None, f, d), lambda bi, ei: (ei, 0, 0)),
            pl.BlockSpec((nbs, n, d), sample),
            g2_spec,
            pl.BlockSpec((1, d), lambda bi, ei: (0, 0)),
        ],
        out_specs=pl.BlockSpec((nbs, n, d), sample),
        compiler_params=_cp("parallel", "arbitrary"),
        name="experts",
    )(h2, pos, gate, jnp.asarray(tok, BF16), wg, wu, wd, x1, g2, ng.reshape(1, d))


def _split_cols(w, widths):
    offs = np.cumsum((0,) + tuple(widths))
    return [w[:, int(offs[i]):int(offs[i + 1])] for i in range(len(widths))]


def _layer(xl, xc, mods, p, need_ctx):
    bsz, s, d = xl.shape
    ctx_row = bsz
    sh1, sc1, g1, sh2, sc2, g2 = mods
    ng = p["norm_g"]
    na_w = (d // 128) * HEAD_DIM
    gq_w = (d // 128) * HEAD_DIM
    gkv_w = GQA_KV_HEADS * HEAD_DIM
    splits = (d, d, na_w, na_w, na_w, gq_w, gkv_w, gkv_w, N_BRANCH * d)
    w_in = p["w_in"].astype(BF16)

    lx, lg, nq, nk, nv, gq, gk, gv, gates = _inproj(xl, sh1, sc1, ng[0], w_in, splits, None)
    if need_ctx:
        cx, cg, cnq, cnk, cnv, cgq, cgk, cgv, cgates = _inproj(xc, sh1, sc1, ng[0], w_in, splits, ctx_row)
    else:
        parts = _split_cols(w_in, splits)
        kv_parts = (0, 3, 4, 6, 7)
        w_kv = jnp.concatenate([parts[i] for i in kv_parts], axis=1)
        cx, cnk, cnv, cgk, cgv = _inproj(xc, sh1, sc1, ng[0], w_kv, tuple(splits[i] for i in kv_parts), ctx_row)

    lp = {
        "conv_w": p["conv_w"], "conv_b": p["conv_b"].reshape(1, d),
        "wab": jnp.concatenate([p["lru_wa"], p["lru_wx"]], axis=-1).astype(BF16),
        "ba": p["lru_ba"].reshape(2, 1, d), "bx": p["lru_bx"].reshape(2, 1, d),
        "lam": p["lru_lam"].reshape(2, 1, d),
    }
    hc, hfin = _lru(cx, jnp.zeros((2, bsz, d), F32), lp)
    hl, _ = _lru(lx, hfin, lp)

    yb = _na(nq, nk, nv, cnk, cnv, p["na_rpb"])
    yc = _gqa(gq, cgk, cgv, gk, gv, p["qn_g"], p["kn_g"])

    woa, wob, woc, wout = (p[k].astype(BF16) for k in ("w_o_a", "w_o_b", "w_o_c", "w_out"))
    wr_t = p["w_router"].T.astype(BF16)
    wg, wu, wd = (p[k].astype(BF16) for k in ("w_gate", "w_up", "w_down"))

    def ffn(x1, h2, mod_row):
        n = x1.shape[1]
        cap = EC_CAPACITY * n // N_EXPERTS
        pos, gate = _route(h2, wr_t, cap)
        return _experts(h2, pos, gate, wg, wu, wd, cap, x1, g2, ng[3], mod_row)

    x1, h2 = _merge(hl, lg, yb, yc, gates, xl, g1, sh2, sc2, ng[1], ng[2], woa, wob, woc, wout, None)
    xl = ffn(x1, h2, None)
    if need_ctx:
        ybc = _attn(cnq, cnk, cnv)
        ycc = _attn(cgq, cgk, cgv, qg=p["qn_g"], kg=p["kn_g"])
        x1c, h2c = _merge(hc, cg, ybc, ycc, cgates, xc, g1, sh2, sc2, ng[1], ng[2], woa, wob, woc, wout, ctx_row)
        xc = ffn(x1c, h2c, ctx_row)
    return xl, xc


def kernel(x, c, ctx, c_ctx, w_mod, b_mod, norm_g, w_in, conv_w, conv_b, lru_wa, lru_ba, lru_wx, lru_bx, lru_lam,
           na_rpb, qn_g, kn_g, w_o_a, w_o_b, w_o_c, w_out, w_router, w_gate, w_up, w_down):
    bsz, _, d = x.shape
    depth = w_mod.shape[0]
    rows = -(-(bsz + 1) // SUBLANES) * SUBLANES
    cond = jnp.zeros((rows, d), F32).at[:bsz].set(c).at[bsz].set(c_ctx)
    xl, xc = x, ctx
    for l in range(depth):
        p = {
            "norm_g": norm_g[l], "w_in": w_in[l], "conv_w": conv_w[l], "conv_b": conv_b[l],
            "lru_wa": lru_wa[l], "lru_ba": lru_ba[l], "lru_wx": lru_wx[l], "lru_bx": lru_bx[l],
            "lru_lam": lru_lam[l], "na_rpb": na_rpb[l], "qn_g": qn_g[l], "kn_g": kn_g[l],
            "w_o_a": w_o_a[l], "w_o_b": w_o_b[l], "w_o_c": w_o_c[l], "w_out": w_out[l],
            "w_router": w_router[l], "w_gate": w_gate[l], "w_up": w_up[l], "w_down": w_down[l],
        }
        m = _adaln(cond, w_mod[l], b_mod[l])
        mods = [m[:, i * d:(i + 1) * d].reshape(rows, 1, d) for i in range(6)]
        xl, xc = _layer(xl, xc, mods, p, l < depth - 1)
    return xl
```

```python
import functools

import jax
import jax.numpy as jnp
import numpy as np
from jax import lax
from jax.experimental import pallas as pl
from jax.experimental.pallas import tpu as pltpu

F32 = jnp.float32
BF16 = jnp.bfloat16

GRID_W = 64
HEAD_DIM = 64
LRU_BLOCKS = 8
CONV_W = 4
LRU_C = 8.0
NA_KH_MAX = 8
NA_KW = 16
GQA_KV_HEADS = 2
ROPE_BASE = 10000.0
N_EXPERTS = 16
EC_CAPACITY = 2
N_BRANCH = 3
EPS = 1e-6
NEG_INF = -1e30

LANES = 128
SUBLANES = 8
BF16_ROWS = 16
VMEM_LIMIT_BYTES = 56 * 1024 * 1024

_NT = (((1,), (1,)), ((), ()))


def _cp(*sem):
    return pltpu.CompilerParams(dimension_semantics=sem, vmem_limit_bytes=VMEM_LIMIT_BYTES)


def _sigmoid(x):
    return 0.5 * jnp.tanh(0.5 * x) + 0.5


def _rms(x, g):
    return x * lax.rsqrt(jnp.mean(x * x, axis=-1, keepdims=True) + EPS) * g


def _adaln_kernel(c_ref, w_ref, b_ref, o_ref):
    c = c_ref[...]
    s = (c * _sigmoid(c)).astype(BF16)
    o_ref[...] = jnp.dot(s, w_ref[...].astype(BF16), preferred_element_type=F32) + b_ref[...]


def _adaln(cond, w_mod, b_mod):
    r, d = cond.shape
    n = w_mod.shape[1]
    tn = n // 4
    return pl.pallas_call(
        _adaln_kernel,
        out_shape=jax.ShapeDtypeStruct((r, n), F32),
        grid=(n // tn,),
        in_specs=[
            pl.BlockSpec((r, d), lambda j: (0, 0)),
            pl.BlockSpec((d, tn), lambda j: (0, j)),
            pl.BlockSpec((1, tn), lambda j: (0, j)),
        ],
        out_specs=pl.BlockSpec((r, tn), lambda j: (0, j)),
        compiler_params=_cp("parallel"),
        name="adaln",
    )(cond, w_mod, b_mod.reshape(1, n))


_ACTIVATIONS = {None: lambda v: v, "gelu": jax.nn.gelu, "sigmoid": _sigmoid}


def _inproj_kernel(x_ref, sh_ref, sc_ref, g_ref, w_ref, *o_refs, splits, acts):
    h = (_rms(x_ref[...], g_ref[...]) * (1.0 + sc_ref[...]) + sh_ref[...]).astype(BF16)
    off = 0
    for o_ref, width, act in zip(o_refs, splits, acts):
        y = _ACTIVATIONS[act](jnp.dot(h, w_ref[:, off:off + width], preferred_element_type=F32))
        o_ref[...] = y.astype(o_ref.dtype)
        off += width


def _inproj(x, shift, scale, gain, w, splits, acts, mod_row):
    b, t, d = x.shape
    tm = min(t, 512)
    n = w.shape[1]
    if mod_row is None:
        mod_map = lambda bi, i: (bi, 0, 0)
    else:
        mod_map = lambda bi, i: (mod_row, 0, 0)
    return pl.pallas_call(
        functools.partial(_inproj_kernel, splits=splits, acts=acts),
        out_shape=[jax.ShapeDtypeStruct((b, t, s), BF16) for s in splits],
        grid=(b, t // tm),
        in_specs=[
            pl.BlockSpec((None, tm, d), lambda bi, i: (bi, i, 0)),
            pl.BlockSpec((None, 1, d), mod_map),
            pl.BlockSpec((None, 1, d), mod_map),
            pl.BlockSpec((1, d), lambda bi, i: (0, 0)),
            pl.BlockSpec((d, n), lambda bi, i: (0, 0)),
        ],
        out_specs=[pl.BlockSpec((None, tm, s), lambda bi, i: (bi, i, 0)) for s in splits],
        compiler_params=_cp("parallel", "parallel"),
        name="inproj",
    )(x, shift, scale, gain.reshape(1, d), w)


def _lru_kernel(x_ref, xp_ref, xn_ref, h0_ref, smat_ref, cw_ref, cb_ref, wab_ref, ba_ref, bx_ref, lam_ref,
                y_ref, hfin_ref, u_scr, a_scr, b_scr, h_scr, st_scr, *, tc, nchunks, nb):
    d = pl.program_id(1)
    j = pl.program_id(2)
    c = jnp.where(d == 0, j, nchunks - 1 - j)
    w = x_ref.shape[-1]
    cwide = 2 * LANES

    @pl.when(j == 0)
    def _():
        for cb in range(LRU_BLOCKS):
            h0 = h0_ref[:, cb * LANES:(cb + 1) * LANES]
            st_scr[cb] = jnp.broadcast_to(h0[:, None, :], (nb, SUBLANES, LANES))

    zero = jnp.zeros((), BF16)
    c_lam = -LRU_C * jax.nn.softplus(-lam_ref[...])
    for cc in range(w // cwide):
        sl2 = slice(cc * cwide, (cc + 1) * cwide)
        for bb in range(nb):
            xe = jnp.concatenate([jnp.where(c > 0, xp_ref[bb, :, sl2], zero), x_ref[bb, :, sl2],
                                  jnp.where(c < nchunks - 1, xn_ref[bb, :, sl2], zero)], axis=0)
            sh = jnp.dot(smat_ref[...], xe, preferred_element_type=F32)
            u = cb_ref[:, sl2]
            for k in range(CONV_W):
                u = u + sh[k * tc:(k + 1) * tc] * cw_ref[k:k + 1, sl2]
            u_scr[bb, :, sl2] = u
        for cb in range(cc * (cwide // LANES), (cc + 1) * (cwide // LANES)):
            sl = slice(cb * LANES, (cb + 1) * LANES)
            u = u_scr[:, :, sl].reshape(nb * tc, LANES)
            gates = jnp.dot(u.astype(BF16), wab_ref[cb], preferred_element_type=F32)
            r = _sigmoid(gates[:, :LANES] + ba_ref[:, sl])
            i = _sigmoid(gates[:, LANES:] + bx_ref[:, sl])
            log_a = c_lam[:, sl] * r
            a = jnp.exp(log_a)
            a_scr[cb] = a
            m = -jnp.tanh(log_a) * (a * a + 1.0)
            b_scr[cb] = jnp.where(m > 0.0, m * lax.rsqrt(m), 0.0) * (i * u)

    @pl.when(d == 0)
    def _():
        _block_scan(a_scr, b_scr, h_scr, st_scr, False, nb, tc)

    @pl.when(d == 1)
    def _():
        _block_scan(a_scr, b_scr, h_scr, st_scr, True, nb, tc)

    for cb in range(LRU_BLOCKS):
        y_ref[:, :, cb * LANES:(cb + 1) * LANES] = h_scr[cb].astype(y_ref.dtype)

    @pl.when(j == nchunks - 1)
    def _():
        for cb in range(LRU_BLOCKS):
            hfin_ref[:, cb * LANES:(cb + 1) * LANES] = st_scr[cb][:, 0, :]


def _block_scan(a_scr, b_scr, h_scr, st_scr, rev, nb, tc):
    nblk = tc // SUBLANES
    row = lax.broadcasted_iota(jnp.int32, (1, SUBLANES, LANES), 1)
    for cb in range(LRU_BLOCKS):
        a = a_scr[cb].reshape(nb * nblk, SUBLANES, LANES)
        b = b_scr[cb].reshape(nb * nblk, SUBLANES, LANES)
        for s in (1, 2, 4):
            shift, keep = (SUBLANES - s, row < SUBLANES - s) if rev else (s, row >= s)
            b = b + a * jnp.where(keep, pltpu.roll(b, shift, 1), 0.0)
            a = a * jnp.where(keep, pltpu.roll(a, shift, 1), 1.0)
        a = a.reshape(nb, nblk, SUBLANES, LANES)
        b = b.reshape(nb, nblk, SUBLANES, LANES)
        carry = st_scr[cb]
        last = 0 if rev else SUBLANES - 1
        for k in (reversed(range(nblk)) if rev else range(nblk)):
            h = a[:, k] * carry + b[:, k]
            h_scr[cb, :, k * SUBLANES:(k + 1) * SUBLANES, :] = h
            carry = jnp.broadcast_to(h[:, last:last + 1, :], h.shape)
        st_scr[cb] = carry


def _lru(x, h0, lp):
    b, t, w = x.shape
    nb = SUBLANES
    tc = min(t, 128)
    nchunks = t // tc
    hb = tc // BF16_ROWS
    nhb = t // BF16_ROWS

    def chunk(di, j):
        return jnp.where(di == 0, j, nchunks - 1 - j)

    smat = np.zeros((CONV_W * tc, tc + 2 * BF16_ROWS), np.float32)
    for k in range(CONV_W):
        smat[k * tc + np.arange(tc), BF16_ROWS + np.arange(tc) + k - CONV_W // 2] = 1.0

    vec = lambda g, di, j: (0, 0)
    dvec = lambda g, di, j: (di, 0, 0)
    return pl.pallas_call(
        functools.partial(_lru_kernel, tc=tc, nchunks=nchunks, nb=nb),
        out_shape=[jax.ShapeDtypeStruct((2, b, t, w), BF16), jax.ShapeDtypeStruct((2, b, w), F32)],
        grid=(b // nb, 2, nchunks),
        in_specs=[
            pl.BlockSpec((nb, tc, w), lambda g, di, j: (g, chunk(di, j), 0)),
            pl.BlockSpec((nb, BF16_ROWS, w), lambda g, di, j: (g, jnp.maximum(chunk(di, j) * hb - 1, 0), 0)),
            pl.BlockSpec((nb, BF16_ROWS, w), lambda g, di, j: (g, jnp.minimum((chunk(di, j) + 1) * hb, nhb - 1), 0)),
            pl.BlockSpec((None, nb, w), lambda g, di, j: (di, g, 0)),
            pl.BlockSpec(smat.shape, vec),
            pl.BlockSpec((CONV_W, w), vec),
            pl.BlockSpec((1, w), vec),
            pl.BlockSpec((None, LRU_BLOCKS, LANES, 2 * LANES), lambda g, di, j: (di, 0, 0, 0)),
            pl.BlockSpec((None, 1, w), dvec),
            pl.BlockSpec((None, 1, w), dvec),
            pl.BlockSpec((None, 1, w), dvec),
        ],
        out_specs=[
            pl.BlockSpec((None, nb, tc, w), lambda g, di, j: (di, g, chunk(di, j), 0)),
            pl.BlockSpec((None, nb, w), lambda g, di, j: (di, g, 0)),
        ],
        scratch_shapes=[
            pltpu.VMEM((nb, tc, w), F32),
            pltpu.VMEM((LRU_BLOCKS, nb * tc, LANES), F32),
            pltpu.VMEM((LRU_BLOCKS, nb * tc, LANES), F32),
            pltpu.VMEM((LRU_BLOCKS, nb, tc, LANES), F32),
            pltpu.VMEM((LRU_BLOCKS, nb, SUBLANES, LANES), F32),
        ],
        compiler_params=_cp("parallel", "arbitrary", "arbitrary"),
        name="rglru",
    )(x, x, x, h0, jnp.asarray(smat, BF16), lp["conv_w"], lp["conv_b"], lp["wab"], lp["ba"], lp["bx"], lp["lam"])


def _pair_rows(x2):
    low = lax.broadcasted_iota(jnp.int32, x2.shape, 1) < HEAD_DIM
    zero = jnp.zeros((), x2.dtype)
    return jnp.concatenate([jnp.where(low, x2, zero), jnp.where(low, zero, x2)], axis=0)


def _softmax_strips(s_ref, p_ref):
    sums = []
    for r in range(s_ref.shape[0] // BF16_ROWS):
        rows = slice(r * BF16_ROWS, (r + 1) * BF16_ROWS)
        s = s_ref[rows, :]
        p = jnp.exp(s - jnp.max(s, axis=-1, keepdims=True))
        sums.append(jnp.sum(p, axis=-1, keepdims=True))
        p_ref[rows, :] = p.astype(BF16)
    return sums


def _normalise_strips(o, sums):
    return jnp.concatenate([o[r * BF16_ROWS:(r + 1) * BF16_ROWS] / l for r, l in enumerate(sums)], axis=0)


def _na_kernel(var_ref, q_ref, k_ref, v_ref, kc_ref, vc_ref, bias_ref, o_ref, s_scr, p_scr,
               *, band, win_rows, kh, rows):
    del var_ref
    r0 = pl.program_id(1) * band
    ws = jnp.clip(r0 - kh // 2, 0, rows - win_rows)
    start = pl.multiple_of(ws * GRID_W, GRID_W)
    nwin = win_rows * GRID_W
    m = band * GRID_W
    npairs = q_ref.shape[-1] // LANES
    low = lax.broadcasted_iota(jnp.int32, (m, LANES), 1) < HEAD_DIM

    def scores(hp):
        sl = slice(hp * LANES, (hp + 1) * LANES)
        qs = _pair_rows(q_ref[:, sl] * jnp.asarray(HEAD_DIM ** -0.5, BF16))
        s_lat = lax.dot_general(qs, k_ref[pl.ds(start, nwin), sl], _NT, preferred_element_type=F32)
        s_scr[hp % 2, :, 0:nwin] = s_lat + bias_ref[2 * hp:2 * hp + 2].reshape(2 * m, nwin)
        s_scr[hp % 2, :, nwin:] = lax.dot_general(qs, kc_ref[:, sl], _NT, preferred_element_type=F32)

    def finish(hp):
        sl = slice(hp * LANES, (hp + 1) * LANES)
        sums = _softmax_strips(s_scr.at[hp % 2], p_scr.at[hp % 2])
        o = jnp.dot(p_scr[hp % 2, :, 0:nwin], v_ref[pl.ds(start, nwin), sl], preferred_element_type=F32)
        o = o + jnp.dot(p_scr[hp % 2, :, nwin:], vc_ref[:, sl], preferred_element_type=F32)
        o = _normalise_strips(o, sums)
        o_ref[:, sl] = jnp.where(low, o[:m], o[m:]).astype(o_ref.dtype)

    scores(0)
    for hp in range(npairs):
        if hp + 1 < npairs:
            scores(hp + 1)
        finish(hp)


def _na_bands(rows, band, win_rows):
    kh = min(NA_KH_MAX, rows)
    variants, var_of_band = [], []
    for r0 in range(0, rows, band):
        ws = int(np.clip(r0 - kh // 2, 0, rows - win_rows))
        key = (r0 - ws, tuple(int(np.clip(r0 + rr - kh // 2, 0, rows - kh)) - ws for rr in range(band)))
        if key not in variants:
            variants.append(key)
        var_of_band.append(variants.index(key))
    return variants, np.asarray(var_of_band, np.int32)


def _na_bias_table(rpb, rows, band, win_rows):
    kh = min(NA_KH_MAX, rows)
    col = np.arange(GRID_W)
    cs = np.clip(col - NA_KW // 2, 0, GRID_W - NA_KW)
    col_mask = (col[None, :] >= cs[:, None]) & (col[None, :] < cs[:, None] + NA_KW)
    dcol = np.clip(col[None, :] - col[:, None], -(NA_KW - 1), NA_KW - 1) + NA_KW - 1
    heads, n_drow, n_dcol = rpb.shape
    onehot = jnp.asarray(dcol.reshape(-1)[None, :] == np.arange(n_dcol)[:, None], F32)
    t2 = jnp.dot(rpb.reshape(heads * n_drow, n_dcol).astype(F32), onehot, precision=lax.Precision.HIGHEST)
    t2 = t2.reshape(heads, n_drow, GRID_W, GRID_W)
    t2 = jnp.where(col_mask[None, None], t2, NEG_INF)
    masked = jnp.full((heads, GRID_W, GRID_W), NEG_INF, F32)
    variants, _ = _na_bands(rows, band, win_rows)
    tabs = []
    for delta, first in variants:
        q_rows = []
        for rr in range(band):
            blocks = [t2[:, j - delta - rr + NA_KH_MAX - 1] if first[rr] <= j < first[rr] + kh else masked
                      for j in range(win_rows)]
            q_rows.append(jnp.concatenate(blocks, axis=-1))
        tabs.append(jnp.concatenate(q_rows, axis=1))
    return jnp.stack(tabs)


def _na(q, k, v, kc, vc, rpb):
    b, s, w = q.shape
    l = kc.shape[1]
    heads = w // HEAD_DIM
    rows = s // GRID_W
    kh = min(NA_KH_MAX, rows)
    band = min(4, rows)
    win_rows = min(-(-(kh + band - 1) // 2) * 2, rows)
    m, nwin = band * GRID_W, win_rows * GRID_W
    bias = _na_bias_table(rpb, rows, band, win_rows)
    _, var_of_band = _na_bands(rows, band, win_rows)

    full = lambda bi, r, var: (bi, 0, 0)
    grid_spec = pltpu.PrefetchScalarGridSpec(
        num_scalar_prefetch=1,
        grid=(b, rows // band),
        in_specs=[
            pl.BlockSpec((None, m, w), lambda bi, r, var: (bi, r, 0)),
            pl.BlockSpec((None, s, w), full),
            pl.BlockSpec((None, s, w), full),
            pl.BlockSpec((None, l, w), full),
            pl.BlockSpec((None, l, w), full),
            pl.BlockSpec((None, heads, m, nwin), lambda bi, r, var: (var[r], 0, 0, 0)),
        ],
        out_specs=pl.BlockSpec((None, m, w), lambda bi, r, var: (bi, r, 0)),
        scratch_shapes=[pltpu.VMEM((2, 2 * m, nwin + l), F32), pltpu.VMEM((2, 2 * m, nwin + l), BF16)],
    )
    return pl.pallas_call(
        functools.partial(_na_kernel, band=band, win_rows=win_rows, kh=kh, rows=rows),
        out_shape=jax.ShapeDtypeStruct((b, s, w), BF16),
        grid_spec=grid_spec,
        compiler_params=_cp("parallel", "arbitrary"),
        name="natten",
    )(jnp.asarray(var_of_band), q, k, v, kc, vc, bias)


def _head_norm(x, g, gmat):
    cols = []
    for c in range(x.shape[1] // LANES):
        xc = x[:, c * LANES:(c + 1) * LANES]
        x2 = xc * xc
        hi = x2.astype(BF16)
        lo = (x2 - hi.astype(F32)).astype(BF16)
        ms = jnp.dot(hi, gmat, preferred_element_type=F32) + jnp.dot(lo, gmat, preferred_element_type=F32)
        cols.append(xc * lax.rsqrt(ms + EPS) * g)
    return cols[0] if len(cols) == 1 else jnp.concatenate(cols, axis=-1)


def _rope(x, cos, sin):
    half = HEAD_DIM // 2
    lane = lax.broadcasted_iota(jnp.int32, (x.shape[0], LANES), 1)
    low = (lane % HEAD_DIM) < half
    cols = []
    for c in range(x.shape[1] // LANES):
        xc = x[:, c * LANES:(c + 1) * LANES]
        sw = jnp.where(low, pltpu.roll(xc, LANES - half, 1), pltpu.roll(xc, half, 1))
        cols.append(xc * cos + sw * sin)
    return cols[0] if len(cols) == 1 else jnp.concatenate(cols, axis=-1)


def _attn_kernel(*refs, norm, group, tq):
    if norm:
        q_ref, k_ref, v_ref, qg_ref, kg_ref, gmat_ref, o_ref = refs
    else:
        q_ref, k_ref, v_ref, o_ref = refs
    q = q_ref[...].astype(F32)
    k = k_ref[...]
    if norm:
        q = _head_norm(q, qg_ref[...], gmat_ref[...])
        k = _head_norm(k.astype(F32), kg_ref[...], gmat_ref[...]).astype(BF16)
    q = (q * (HEAD_DIM ** -0.5)).astype(BF16)

    outs = []
    for g in range(k_ref.shape[1] // HEAD_DIM):
        sl = slice(g * HEAD_DIM, (g + 1) * HEAD_DIM)
        qs = [q[:, (g * group + jj) * HEAD_DIM:(g * group + jj + 1) * HEAD_DIM] for jj in range(group)]
        qs = qs[0] if group == 1 else jnp.concatenate(qs, axis=0)
        s = lax.dot_general(qs, k[:, sl], _NT, preferred_element_type=F32)
        p = jnp.exp(s - jnp.max(s, axis=-1, keepdims=True))
        o = jnp.dot(p.astype(BF16), v_ref[:, sl], preferred_element_type=F32) / jnp.sum(p, axis=-1, keepdims=True)
        for jj in range(group):
            outs.append(o[jj * tq:(jj + 1) * tq])
    o_ref[...] = jnp.concatenate(outs, axis=-1).astype(o_ref.dtype)


def _dup_heads(x):
    low = lax.broadcasted_iota(jnp.int32, x.shape, 1) < HEAD_DIM
    sw = pltpu.roll(x, HEAD_DIM, 1)
    return jnp.where(low, x, sw), jnp.where(low, sw, x)


def _gqa_kernel(q_ref, kc_ref, vc_ref, kl_ref, vl_ref, qg_ref, kg_ref, gmat_ref, cosq_ref, sinq_ref, cosk_ref,
                sink_ref, o_ref, kc_scr, kl_scr, vc_scr, vl_scr, s_scr, p_scr, *, sub):
    n_ctx = kc_ref.shape[0]
    n_kv = GQA_KV_HEADS

    @pl.when(pl.program_id(1) == 0)
    def _():
        gmat = gmat_ref[...]
        kc = _head_norm(kc_ref[...].astype(F32), kg_ref[...], gmat)
        kl = _rope(_head_norm(kl_ref[...].astype(F32), kg_ref[...], gmat), cosk_ref[...], sink_ref[...])
        for scr, val in ((kc_scr, kc), (kl_scr, kl), (vc_scr, vc_ref[...].astype(F32)),
                         (vl_scr, vl_ref[...].astype(F32))):
            for g, dup in enumerate(_dup_heads(val)):
                scr[g] = dup.astype(BF16)

    tq = q_ref.shape[0]
    chunks = []
    for c in range(q_ref.shape[1] // LANES):
        qc = q_ref[:, c * LANES:(c + 1) * LANES].astype(F32)
        qc = _rope(_head_norm(qc, qg_ref[...], gmat_ref[...]), cosq_ref[...], sinq_ref[...])
        chunks.append((qc * (HEAD_DIM ** -0.5)).astype(BF16))
    pairs_per_group = len(chunks) // n_kv
    low = lax.broadcasted_iota(jnp.int32, (sub, LANES), 1) < HEAD_DIM
    units = [(t, g) for t in range(tq // sub) for g in range(n_kv)]

    def scores(u):
        t, g = units[u]
        rows = [_pair_rows(chunks[g * pairs_per_group + j][t * sub:(t + 1) * sub]) for j in range(pairs_per_group)]
        qs = jnp.concatenate(rows, axis=0)
        s_scr[u % 2, :, 0:n_ctx] = lax.dot_general(qs, kc_scr[g], _NT, preferred_element_type=F32)
        s_scr[u % 2, :, n_ctx:] = lax.dot_general(qs, kl_scr[g], _NT, preferred_element_type=F32)

    def finish(u):
        t, g = units[u]
        sums = _softmax_strips(s_scr.at[u % 2], p_scr.at[u % 2])
        o = jnp.dot(p_scr[u % 2, :, 0:n_ctx], vc_scr[g], preferred_element_type=F32)
        o = o + jnp.dot(p_scr[u % 2, :, n_ctx:], vl_scr[g], preferred_element_type=F32)
        o = _normalise_strips(o, sums)
        for j in range(pairs_per_group):
            c = g * pairs_per_group + j
            lo, hi = o[2 * j * sub:(2 * j + 1) * sub], o[(2 * j + 1) * sub:(2 * j + 2) * sub]
            o_ref[t * sub:(t + 1) * sub, c * LANES:(c + 1) * LANES] = jnp.where(low, lo, hi).astype(o_ref.dtype)

    scores(0)
    for u in range(len(units)):
        if u + 1 < len(units):
            scores(u + 1)
        finish(u)


def _gqa(q, kc, vc, kl, vl, qg, kg):
    b, t, wq = q.shape
    l, wk = kc.shape[1], kc.shape[2]
    s = kl.shape[1]
    assert wk == GQA_KV_HEADS * HEAD_DIM == LANES
    tq = min(t, 512)
    sub = min(tq, 256)
    m_unit = 2 * (wq // LANES // GQA_KV_HEADS) * sub
    reps = LANES // HEAD_DIM
    gmat = _head_mean_matrix()
    cos, sin = _rope_tables(t)
    full = lambda bi, i: (bi, 0, 0)
    const = lambda bi, i: (0, 0)
    return pl.pallas_call(
        functools.partial(_gqa_kernel, sub=sub),
        out_shape=jax.ShapeDtypeStruct((b, t, wq), BF16),
        grid=(b, t // tq),
        in_specs=[
            pl.BlockSpec((None, tq, wq), lambda bi, i: (bi, i, 0)),
            pl.BlockSpec((None, l, wk), full), pl.BlockSpec((None, l, wk), full),
            pl.BlockSpec((None, s, wk), full), pl.BlockSpec((None, s, wk), full),
            pl.BlockSpec((1, LANES), const), pl.BlockSpec((1, LANES), const), pl.BlockSpec((LANES, LANES), const),
            pl.BlockSpec((tq, LANES), lambda bi, i: (i, 0)), pl.BlockSpec((tq, LANES), lambda bi, i: (i, 0)),
            pl.BlockSpec((s, LANES), const), pl.BlockSpec((s, LANES), const),
        ],
        out_specs=pl.BlockSpec((None, tq, wq), lambda bi, i: (bi, i, 0)),
        scratch_shapes=[
            pltpu.VMEM((GQA_KV_HEADS, l, LANES), BF16), pltpu.VMEM((GQA_KV_HEADS, s, LANES), BF16),
            pltpu.VMEM((GQA_KV_HEADS, l, LANES), BF16), pltpu.VMEM((GQA_KV_HEADS, s, LANES), BF16),
            pltpu.VMEM((2, m_unit, l + s), F32), pltpu.VMEM((2, m_unit, l + s), BF16),
        ],
        compiler_params=_cp("parallel", "arbitrary"),
        name="gqa",
    )(q, kc, vc, kl, vl, jnp.tile(qg, reps).reshape(1, LANES), jnp.tile(kg, reps).reshape(1, LANES), gmat,
      cos, sin, cos, sin)


def _rope_tables(n_tokens):
    t = np.arange(n_tokens)
    pos = np.stack([t // GRID_W, t % GRID_W], axis=-1).astype(np.float32)
    n_freq = HEAD_DIM // 4
    inv_freq = jnp.asarray(ROPE_BASE, F32) ** (-jnp.arange(n_freq, dtype=F32) / n_freq)
    ang = (jnp.asarray(pos)[:, :, None] * inv_freq).reshape(n_tokens, 2 * n_freq)
    cos, sin = jnp.cos(ang), jnp.sin(ang)
    reps = LANES // HEAD_DIM
    return jnp.tile(jnp.concatenate([cos, cos], -1), (1, reps)), jnp.tile(jnp.concatenate([-sin, sin], -1), (1, reps))


def _head_mean_matrix():
    blk = np.arange(LANES) // HEAD_DIM
    return jnp.asarray((blk[:, None] == blk[None, :]).astype(np.float32) / HEAD_DIM, BF16)


def _attn(q, k, v, qg=None, kg=None):
    b, t, wq = q.shape
    wk = k.shape[2]
    norm = qg is not None
    tok = lambda w: pl.BlockSpec((None, t, w), lambda bi: (bi, 0, 0))
    args = [q, k, v]
    in_specs = [tok(wq), tok(wk), tok(wk)]
    if norm:
        reps = LANES // HEAD_DIM
        args += [jnp.tile(qg, reps).reshape(1, LANES), jnp.tile(kg, reps).reshape(1, LANES), _head_mean_matrix()]
        in_specs += [pl.BlockSpec((1, LANES), lambda bi: (0, 0))] * 2
        in_specs += [pl.BlockSpec((LANES, LANES), lambda bi: (0, 0))]
    return pl.pallas_call(
        functools.partial(_attn_kernel, norm=norm, group=wq // wk, tq=t),
        out_shape=jax.ShapeDtypeStruct((b, t, wq), BF16),
        grid=(b,),
        in_specs=in_specs,
        out_specs=tok(wq),
        compiler_params=_cp("parallel"),
        name="attn",
    )(*args)


def _merge_kernel(hf_ref, hr_ref, gl_ref, yb_ref, yc_ref, gt_ref, x_ref, g1_ref, sh2_ref, sc2_ref, ng1_ref, ng2_ref,
                  woa_ref, wob_ref, woc_ref, wout_ref, x1_ref, h2_ref, h2p_ref):
    d = x_ref.shape[-1]
    ya = (hf_ref[...].astype(F32) + hr_ref[...].astype(F32)) * gl_ref[...].astype(F32)
    pa = jnp.dot(ya.astype(BF16), woa_ref[...], preferred_element_type=F32)
    pb = jnp.dot(yb_ref[...], wob_ref[...], preferred_element_type=F32)
    pc = jnp.dot(yc_ref[...], woc_ref[...], preferred_element_type=F32)
    ga = gt_ref[:, 0:d].astype(F32)
    gb = gt_ref[:, d:2 * d].astype(F32)
    gc = gt_ref[:, 2 * d:3 * d].astype(F32)
    m = ga * pa + gb * pb + gc * pc
    y = jnp.dot(m.astype(BF16), wout_ref[...], preferred_element_type=F32)
    x1 = x_ref[...] + g1_ref[...] * _rms(y, ng1_ref[...])
    x1_ref[...] = x1
    h2 = (_rms(x1, ng2_ref[...]) * (1.0 + sc2_ref[...]) + sh2_ref[...]).astype(BF16)
    h2_ref[...] = h2
    h2p_ref[...] = _pack_halves(h2)


def _pack_halves(xb):
    half = xb.shape[1] // 2
    bits = pltpu.bitcast(xb.astype(F32), jnp.uint32)
    return bits[:, :half] | (bits[:, half:] >> 16)


def _unpack_halves(words):
    hi = pltpu.bitcast(words & jnp.uint32(0xFFFF0000), F32)
    lo = pltpu.bitcast(words << 16, F32)
    return jnp.concatenate([hi, lo], axis=1).astype(BF16)


def _merge(h, gl, yb, yc, gates, x, g1, sh2, sc2, ng1, ng2, woa, wob, woc, wout, mod_row):
    b, t, d = x.shape
    tm = min(t, 256)
    wl, wb, wc = gl.shape[2], yb.shape[2], yc.shape[2]
    if mod_row is None:
        mod_map = lambda bi, i: (bi, 0, 0)
    else:
        mod_map = lambda bi, i: (mod_row, 0, 0)
    tok = lambda w: pl.BlockSpec((None, tm, w), lambda bi, i: (bi, i, 0))
    mod = pl.BlockSpec((None, 1, d), mod_map)
    const = lambda r, c: pl.BlockSpec((r, c), lambda bi, i: (0, 0))
    return pl.pallas_call(
        _merge_kernel,
        out_shape=[jax.ShapeDtypeStruct((b, t, d), F32), jax.ShapeDtypeStruct((b, t, d), BF16),
                   jax.ShapeDtypeStruct((b, t, d // 2), jnp.uint32)],
        grid=(b, t // tm),
        in_specs=[
            pl.BlockSpec((None, None, tm, wl), lambda bi, i: (0, bi, i, 0)),
            pl.BlockSpec((None, None, tm, wl), lambda bi, i: (1, bi, i, 0)),
            tok(wl), tok(wb), tok(wc), tok(N_BRANCH * d), tok(d),
            mod, mod, mod, const(1, d), const(1, d),
            const(wl, d), const(wb, d), const(wc, d), const(d, d),
        ],
        out_specs=[tok(d), tok(d), tok(d // 2)],
        compiler_params=_cp("parallel", "parallel"),
        name="merge",
    )(h, h, gl, yb, yc, gates, x, g1, sh2, sc2, ng1.reshape(1, d), ng2.reshape(1, d), woa, wob, woc, wout)


def _lane_cumsum(x, tri):
    e, n = x.shape
    carry = jnp.zeros((e, 1), F32)
    cols = []
    for c in range(n // LANES):
        part = jnp.dot(x[:, c * LANES:(c + 1) * LANES].astype(BF16), tri, preferred_element_type=F32) + carry
        cols.append(part)
        carry = part[:, LANES - 1:LANES]
    return cols[0] if len(cols) == 1 else jnp.concatenate(cols, axis=-1)


def _route_kernel(h_ref, wr_ref, tri_ref, tok_ref, idx_ref, gcol_ref, pos_scr, aff_scr, *, cap):
    nbs, n_exp = h_ref.shape[0], wr_ref.shape[0]
    affs = []
    for s in range(nbs):
        logits = lax.dot_general(wr_ref[...], h_ref[s], _NT, preferred_element_type=F32)
        z = jnp.exp(logits - jnp.max(logits, axis=0, keepdims=True))
        affs.append(z / jnp.sum(z, axis=0, keepdims=True))
    aff = affs[0] if nbs == 1 else jnp.concatenate(affs, axis=0)
    aff_scr[...] = aff
    bits = pltpu.bitcast(aff, jnp.int32)
    e, n = bits.shape

    def bisect(_, lohi):
        lo, hi = lohi
        mid = lo + ((hi - lo + 1) >> 1)
        cnt = jnp.sum(jnp.where(bits >= mid, 1.0, 0.0), axis=1, keepdims=True)
        ok = cnt >= cap
        return jnp.where(ok, mid, lo), jnp.where(ok, hi, mid - 1)

    one_bits = 0x3F800000
    lo, _ = lax.fori_loop(0, 31, bisect, (jnp.zeros((e, 1), jnp.int32), jnp.full((e, 1), one_bits, jnp.int32)))
    gt = jnp.where(bits > lo, 1.0, 0.0)
    eq = jnp.where(bits == lo, 1.0, 0.0)
    need = cap - jnp.sum(gt, axis=1, keepdims=True)
    tri = tri_ref[...]
    sel = gt + jnp.where(_lane_cumsum(eq, tri) <= need, eq, 0.0)
    pos = _lane_cumsum(sel, tri) - 1.0
    pos_scr[...] = jnp.where(sel > 0.0, pos, -1.0).astype(jnp.int32)

    slot = lax.broadcasted_iota(jnp.int32, (cap, n), 0)

    def compact(r, carry):
        hit = pos_scr[pl.ds(r, 1), :] == slot
        onehot = jnp.where(hit, 1.0, 0.0).astype(BF16)
        digits = lax.dot_general(tok_ref[...], onehot, _NT, preferred_element_type=F32)
        s, ex = r // n_exp, r % n_exp
        idx_ref[s, pl.ds(ex, 1), :] = (digits[0:1] * GRID_W + digits[1:2]).astype(jnp.int32)
        gcol_ref[s, ex] = jnp.sum(jnp.where(hit, aff_scr[pl.ds(r, 1), :], 0.0), axis=1, keepdims=True)
        return carry

    lax.fori_loop(0, e, compact, 0)


def _route(h2, w_router_t, cap):
    b, n, d = h2.shape
    e = w_router_t.shape[0]
    nbs = 4 if b % 4 == 0 else 1
    tri = jnp.asarray(np.triu(np.ones((LANES, LANES), np.float32)), BF16)
    t = np.arange(n)
    tok = np.zeros((SUBLANES, n), np.float32)
    tok[0], tok[1] = t // GRID_W, t % GRID_W
    return pl.pallas_call(
        functools.partial(_route_kernel, cap=cap),
        out_shape=[jax.ShapeDtypeStruct((b, e, cap), jnp.int32), jax.ShapeDtypeStruct((b, e, cap, 1), F32)],
        grid=(b // nbs,),
        in_specs=[
            pl.BlockSpec((nbs, n, d), lambda bi: (bi, 0, 0)),
            pl.BlockSpec((e, d), lambda bi: (0, 0)),
            pl.BlockSpec((LANES, LANES), lambda bi: (0, 0)),
            pl.BlockSpec((SUBLANES, n), lambda bi: (0, 0)),
        ],
        out_specs=[pl.BlockSpec((nbs, e, cap), lambda bi: (bi, 0, 0)),
                   pl.BlockSpec((nbs, e, cap, 1), lambda bi: (bi, 0, 0, 0))],
        scratch_shapes=[pltpu.VMEM((nbs * e, n), jnp.int32), pltpu.VMEM((nbs * e, n), F32)],
        compiler_params=_cp("parallel"),
        name="route",
    )(h2, w_router_t, tri, jnp.asarray(tok, BF16))


def _expert_kernel(idx_smem, idx_ref, gcol_ref, h_ref, wg_ref, wu_ref, wd_ref, x_ref, g2_ref, ng_ref, o_ref,
                   hs_scr, *, cap, nbs, per_sample_gate):
    e = pl.program_id(1)
    n_exp = pl.num_programs(1)
    n = h_ref.shape[1]

    def gather(expert, buf):
        for s in range(nbs):
            for j in range(cap):
                t = idx_smem[s, expert, j]
                hs_scr[buf, pl.ds(s * cap + j, 1), :] = h_ref[s, pl.ds(t, 1), :]

    @pl.when(e == 0)
    def _():
        o_ref[...] = jnp.zeros_like(o_ref)
        gather(0, 0)

    hs = _unpack_halves(hs_scr[e % 2])
    gather(jnp.minimum(e + 1, n_exp - 1), (e + 1) % 2)

    gcol = gcol_ref[...].reshape(nbs * cap, 1)
    idx = [idx_ref[s, pl.ds(e, 1), :] for s in range(nbs)]
    a = jnp.dot(hs, wg_ref[...], preferred_element_type=F32)
    u = (a * _sigmoid(a)) * jnp.dot(hs, wu_ref[...], preferred_element_type=F32)
    y = (jnp.dot(u.astype(BF16), wd_ref[...], preferred_element_type=F32) * gcol).astype(BF16)

    rc = min(n, 256)
    rows = lax.broadcasted_iota(jnp.int32, (rc, cap), 0)
    for s in range(nbs):
        ys = y[s * cap:(s + 1) * cap]
        for c in range(n // rc):
            pt = jnp.where(idx[s] == rows + c * rc, 1.0, 0.0).astype(BF16)
            o_ref[s, c * rc:(c + 1) * rc, :] += jnp.dot(pt, ys, preferred_element_type=F32)

    @pl.when(e == n_exp - 1)
    def _():
        for s in range(nbs):
            g2 = g2_ref[s if per_sample_gate else 0]
            for c in range(n // rc):
                sl = slice(c * rc, (c + 1) * rc)
                o_ref[s, sl, :] = x_ref[s, sl, :] + g2 * _rms(o_ref[s, sl, :], ng_ref[...])


def _experts(h2p, idx, gcol, wg, wu, wd, cap, x1, g2, ng, mod_row):
    b, n, dh = h2p.shape
    e, d, f = wg.shape
    nbs = max(1, min(b, 256 // cap))
    sample = lambda bi, ei: (bi, 0, 0)
    if mod_row is None:
        g2_spec = pl.BlockSpec((nbs, 1, d), sample)
    else:
        g2_spec = pl.BlockSpec((1, 1, d), lambda bi, ei: (mod_row, 0, 0))
    once = pl.Buffered(1)
    return pl.pallas_call(
        functools.partial(_expert_kernel, cap=cap, nbs=nbs, per_sample_gate=mod_row is None),
        out_shape=jax.ShapeDtypeStruct((b, n, d), F32),
        grid=(b // nbs, e),
        in_specs=[
            pl.BlockSpec((nbs, e, cap), sample, memory_space=pltpu.SMEM),
            pl.BlockSpec((nbs, e, cap), sample),
            pl.BlockSpec((nbs, None, cap, 1), lambda bi, ei: (bi, ei, 0, 0)),
            pl.BlockSpec((nbs, n, dh), sample, pipeline_mode=once),
            pl.BlockSpec((None, d, f), lambda bi, ei: (ei, 0, 0)),
            pl.BlockSpec((None, d, f), lambda bi, ei: (ei, 0, 0)),
            pl.BlockSpec((None, f, d), lambda bi, ei: (ei, 0, 0)),
            pl.BlockSpec((nbs, n, d), sample),
            g2_spec,
            pl.BlockSpec((1, d), lambda bi, ei: (0, 0)),
        ],
        out_specs=pl.BlockSpec((nbs, n, d), sample),
        scratch_shapes=[pltpu.VMEM((2, nbs * cap, dh), jnp.uint32)],
        compiler_params=_cp("parallel", "arbitrary"),
        name="experts",
    )(idx, idx, gcol, h2p, wg, wu, wd, x1, g2, ng.reshape(1, d))


def _split_cols(w, widths):
    offs = np.cumsum((0,) + tuple(widths))
    return [w[:, int(offs[i]):int(offs[i + 1])] for i in range(len(widths))]


def _layer(xl, xc, mods, p, need_ctx):
    bsz, s, d = xl.shape
    ctx_row = bsz
    sh1, sc1, g1, sh2, sc2, g2 = mods
    ng = p["norm_g"]
    na_w = (d // 128) * HEAD_DIM
    gq_w = (d // 128) * HEAD_DIM
    gkv_w = GQA_KV_HEADS * HEAD_DIM
    splits = (d, d, na_w, na_w, na_w, gq_w, gkv_w, gkv_w, N_BRANCH * d)
    acts = (None, "gelu", None, None, None, None, None, None, "sigmoid")
    w_in = p["w_in"].astype(BF16)

    lx, lg, nq, nk, nv, gq, gk, gv, gates = _inproj(xl, sh1, sc1, ng[0], w_in, splits, acts, None)
    if need_ctx:
        cx, cg, cnq, cnk, cnv, cgq, cgk, cgv, cgates = _inproj(xc, sh1, sc1, ng[0], w_in, splits, acts, ctx_row)
    else:
        parts = _split_cols(w_in, splits)
        kv_parts = (0, 3, 4, 6, 7)
        w_kv = jnp.concatenate([parts[i] for i in kv_parts], axis=1)
        cx, cnk, cnv, cgk, cgv = _inproj(xc, sh1, sc1, ng[0], w_kv, tuple(splits[i] for i in kv_parts),
                                         tuple(acts[i] for i in kv_parts), ctx_row)

    lp = {
        "conv_w": p["conv_w"], "conv_b": p["conv_b"].reshape(1, d),
        "wab": jnp.concatenate([p["lru_wa"], p["lru_wx"]], axis=-1).astype(BF16),
        "ba": p["lru_ba"].reshape(2, 1, d), "bx": p["lru_bx"].reshape(2, 1, d),
        "lam": p["lru_lam"].reshape(2, 1, d),
    }
    hc, hfin = _lru(cx, jnp.zeros((2, bsz, d), F32), lp)
    hl, _ = _lru(lx, hfin, lp)

    yb = _na(nq, nk, nv, cnk, cnv, p["na_rpb"])
    yc = _gqa(gq, cgk, cgv, gk, gv, p["qn_g"], p["kn_g"])

    woa, wob, woc, wout = (p[k].astype(BF16) for k in ("w_o_a", "w_o_b", "w_o_c", "w_out"))
    wr_t = p["w_router"].T.astype(BF16)
    wg, wu, wd = (p[k].astype(BF16) for k in ("w_gate", "w_up", "w_down"))

    def ffn(x1, h2, h2p, mod_row):
        n = x1.shape[1]
        cap = EC_CAPACITY * n // N_EXPERTS
        idx, gcol = _route(h2, wr_t, cap)
        return _experts(h2p, idx, gcol, wg, wu, wd, cap, x1, g2, ng[3], mod_row)

    xl = ffn(*_merge(hl, lg, yb, yc, gates, xl, g1, sh2, sc2, ng[1], ng[2], woa, wob, woc, wout, None), None)
    if need_ctx:
        ybc = _attn(cnq, cnk, cnv)
        ycc = _attn(cgq, cgk, cgv, qg=p["qn_g"], kg=p["kn_g"])
        xc = ffn(*_merge(hc, cg, ybc, ycc, cgates, xc, g1, sh2, sc2, ng[1], ng[2], woa, wob, woc, wout, ctx_row),
                 ctx_row)
    return xl, xc


def kernel(x, c, ctx, c_ctx, w_mod, b_mod, norm_g, w_in, conv_w, conv_b, lru_wa, lru_ba, lru_wx, lru_bx, lru_lam,
           na_rpb, qn_g, kn_g, w_o_a, w_o_b, w_o_c, w_out, w_router, w_gate, w_up, w_down):
    bsz, _, d = x.shape
    depth = w_mod.shape[0]
    rows = -(-(bsz + 1) // SUBLANES) * SUBLANES
    cond = jnp.zeros((rows, d), F32).at[:bsz].set(c).at[bsz].set(c_ctx)
    xl, xc = x, ctx
    for l in range(depth):
        p = {
            "norm_g": norm_g[l], "w_in": w_in[l], "conv_w": conv_w[l], "conv_b": conv_b[l],
            "lru_wa": lru_wa[l], "lru_ba": lru_ba[l], "lru_wx": lru_wx[l], "lru_bx": lru_bx[l],
            "lru_lam": lru_lam[l], "na_rpb": na_rpb[l], "qn_g": qn_g[l], "kn_g": kn_g[l],
            "w_o_a": w_o_a[l], "w_o_b": w_o_b[l], "w_o_c": w_o_c[l], "w_out": w_out[l],
            "w_router": w_router[l], "w_gate": w_gate[l], "w_up": w_up[l], "w_down": w_down[l],
        }
        m = _adaln(cond, w_mod[l], b_mod[l])
        mods = [m[:, i * d:(i + 1) * d].reshape(rows, 1, d) for i in range(6)]
        xl, xc = _layer(xl, xc, mods, p, l < depth - 1)
    return xl
```

```python
import functools

import jax
import jax.numpy as jnp
import numpy as np
from jax import lax
from jax.experimental import pallas as pl
from jax.experimental.pallas import tpu as pltpu

F32 = jnp.float32
BF16 = jnp.bfloat16

GRID_W = 64
HEAD_DIM = 64
LRU_BLOCKS = 8
CONV_W = 4
LRU_C = 8.0
NA_KH_MAX = 8
NA_KW = 16
GQA_KV_HEADS = 2
ROPE_BASE = 10000.0
N_EXPERTS = 16
EC_CAPACITY = 2
N_BRANCH = 3
EPS = 1e-6
NEG_INF = -1e30

LANES = 128
SUBLANES = 8
BF16_ROWS = 16
VMEM_LIMIT_BYTES = 56 * 1024 * 1024

_NT = (((1,), (1,)), ((), ()))


def _cp(*sem):
    return pltpu.CompilerParams(dimension_semantics=sem, vmem_limit_bytes=VMEM_LIMIT_BYTES)


def _sigmoid(x):
    return 0.5 * jnp.tanh(0.5 * x) + 0.5


def _rms(x, g):
    return x * lax.rsqrt(jnp.mean(x * x, axis=-1, keepdims=True) + EPS) * g


def _adaln_kernel(c_ref, w_ref, b_ref, o_ref):
    c = c_ref[...]
    s = (c * _sigmoid(c)).astype(BF16)
    o_ref[...] = jnp.dot(s, w_ref[...].astype(BF16), preferred_element_type=F32) + b_ref[...]


def _adaln(cond, w_mod, b_mod):
    r, d = cond.shape
    n = w_mod.shape[1]
    tn = n // 4
    return pl.pallas_call(
        _adaln_kernel,
        out_shape=jax.ShapeDtypeStruct((r, n), F32),
        grid=(n // tn,),
        in_specs=[
            pl.BlockSpec((r, d), lambda j: (0, 0)),
            pl.BlockSpec((d, tn), lambda j: (0, j)),
            pl.BlockSpec((1, tn), lambda j: (0, j)),
        ],
        out_specs=pl.BlockSpec((r, tn), lambda j: (0, j)),
        compiler_params=_cp("parallel"),
        name="adaln",
    )(cond, w_mod, b_mod.reshape(1, n))


_ACTIVATIONS = {None: lambda v: v, "gelu": jax.nn.gelu, "sigmoid": _sigmoid}


def _inproj_kernel(x_ref, sh_ref, sc_ref, g_ref, w_ref, *o_refs, splits, acts):
    h = (_rms(x_ref[...], g_ref[...]) * (1.0 + sc_ref[...]) + sh_ref[...]).astype(BF16)
    off = 0
    for o_ref, width, act in zip(o_refs, splits, acts):
        y = _ACTIVATIONS[act](jnp.dot(h, w_ref[:, off:off + width], preferred_element_type=F32))
        o_ref[...] = y.astype(o_ref.dtype)
        off += width


def _inproj(x, shift, scale, gain, w, splits, acts, mod_row):
    b, t, d = x.shape
    tm = min(t, 512)
    n = w.shape[1]
    if mod_row is None:
        mod_map = lambda bi, i: (bi, 0, 0)
    else:
        mod_map = lambda bi, i: (mod_row, 0, 0)
    return pl.pallas_call(
        functools.partial(_inproj_kernel, splits=splits, acts=acts),
        out_shape=[jax.ShapeDtypeStruct((b, t, s), BF16) for s in splits],
        grid=(b, t // tm),
        in_specs=[
            pl.BlockSpec((None, tm, d), lambda bi, i: (bi, i, 0)),
            pl.BlockSpec((None, 1, d), mod_map),
            pl.BlockSpec((None, 1, d), mod_map),
            pl.BlockSpec((1, d), lambda bi, i: (0, 0)),
            pl.BlockSpec((d, n), lambda bi, i: (0, 0)),
        ],
        out_specs=[pl.BlockSpec((None, tm, s), lambda bi, i: (bi, i, 0)) for s in splits],
        compiler_params=_cp("parallel", "parallel"),
        name="inproj",
    )(x, shift, scale, gain.reshape(1, d), w)


def _lru_kernel(x_ref, xp_ref, xn_ref, h0_ref, smat_ref, cw_ref, cb_ref, wab_ref, ba_ref, bx_ref, lam_ref,
                y_ref, hfin_ref, u_scr, a_scr, b_scr, h_scr, st_scr, *, tc, nchunks, nb):
    d = pl.program_id(1)
    j = pl.program_id(2)
    c = jnp.where(d == 0, j, nchunks - 1 - j)
    w = x_ref.shape[-1]
    cwide = 2 * LANES

    @pl.when(j == 0)
    def _():
        for cb in range(LRU_BLOCKS):
            h0 = h0_ref[:, cb * LANES:(cb + 1) * LANES]
            st_scr[cb] = jnp.broadcast_to(h0[:, None, :], (nb, SUBLANES, LANES))

    zero = jnp.zeros((), BF16)
    c_lam = -LRU_C * jax.nn.softplus(-lam_ref[...])
    for cc in range(w // cwide):
        sl2 = slice(cc * cwide, (cc + 1) * cwide)
        for bb in range(nb):
            xe = jnp.concatenate([jnp.where(c > 0, xp_ref[bb, :, sl2], zero), x_ref[bb, :, sl2],
                                  jnp.where(c < nchunks - 1, xn_ref[bb, :, sl2], zero)], axis=0)
            sh = jnp.dot(smat_ref[...], xe, preferred_element_type=F32)
            u = cb_ref[:, sl2]
            for k in range(CONV_W):
                u = u + sh[k * tc:(k + 1) * tc] * cw_ref[k:k + 1, sl2]
            u_scr[bb, :, sl2] = u
        for cb in range(cc * (cwide // LANES), (cc + 1) * (cwide // LANES)):
            sl = slice(cb * LANES, (cb + 1) * LANES)
            u = u_scr[:, :, sl].reshape(nb * tc, LANES)
            gates = jnp.dot(u.astype(BF16), wab_ref[cb], preferred_element_type=F32)
            r = _sigmoid(gates[:, :LANES] + ba_ref[:, sl])
            i = _sigmoid(gates[:, LANES:] + bx_ref[:, sl])
            log_a = c_lam[:, sl] * r
            a = jnp.exp(log_a)
            a_scr[cb] = a
            m = -jnp.tanh(log_a) * (a * a + 1.0)
            b_scr[cb] = jnp.where(m > 0.0, m * lax.rsqrt(m), 0.0) * (i * u)

    @pl.when(d == 0)
    def _():
        _block_scan(a_scr, b_scr, h_scr, st_scr, False, nb, tc)

    @pl.when(d == 1)
    def _():
        _block_scan(a_scr, b_scr, h_scr, st_scr, True, nb, tc)

    for cb in range(LRU_BLOCKS):
        y_ref[:, :, cb * LANES:(cb + 1) * LANES] = h_scr[cb].astype(y_ref.dtype)

    @pl.when(j == nchunks - 1)
    def _():
        for cb in range(LRU_BLOCKS):
            hfin_ref[:, cb * LANES:(cb + 1) * LANES] = st_scr[cb][:, 0, :]


def _block_scan(a_scr, b_scr, h_scr, st_scr, rev, nb, tc):
    nblk = tc // SUBLANES
    row = lax.broadcasted_iota(jnp.int32, (1, SUBLANES, LANES), 1)
    for cb in range(LRU_BLOCKS):
        a = a_scr[cb].reshape(nb * nblk, SUBLANES, LANES)
        b = b_scr[cb].reshape(nb * nblk, SUBLANES, LANES)
        for s in (1, 2, 4):
            shift, keep = (SUBLANES - s, row < SUBLANES - s) if rev else (s, row >= s)
            b = b + a * jnp.where(keep, pltpu.roll(b, shift, 1), 0.0)
            a = a * jnp.where(keep, pltpu.roll(a, shift, 1), 1.0)
        a = a.reshape(nb, nblk, SUBLANES, LANES)
        b = b.reshape(nb, nblk, SUBLANES, LANES)
        carry = st_scr[cb]
        last = 0 if rev else SUBLANES - 1
        for k in (reversed(range(nblk)) if rev else range(nblk)):
            h = a[:, k] * carry + b[:, k]
            h_scr[cb, :, k * SUBLANES:(k + 1) * SUBLANES, :] = h
            carry = jnp.broadcast_to(h[:, last:last + 1, :], h.shape)
        st_scr[cb] = carry


def _lru(x, h0, lp):
    b, t, w = x.shape
    nb = SUBLANES
    tc = min(t, 128)
    nchunks = t // tc
    hb = tc // BF16_ROWS
    nhb = t // BF16_ROWS

    def chunk(di, j):
        return jnp.where(di == 0, j, nchunks - 1 - j)

    smat = np.zeros((CONV_W * tc, tc + 2 * BF16_ROWS), np.float32)
    for k in range(CONV_W):
        smat[k * tc + np.arange(tc), BF16_ROWS + np.arange(tc) + k - CONV_W // 2] = 1.0

    vec = lambda g, di, j: (0, 0)
    dvec = lambda g, di, j: (di, 0, 0)
    return pl.pallas_call(
        functools.partial(_lru_kernel, tc=tc, nchunks=nchunks, nb=nb),
        out_shape=[jax.ShapeDtypeStruct((2, b, t, w), BF16), jax.ShapeDtypeStruct((2, b, w), F32)],
        grid=(b // nb, 2, nchunks),
        in_specs=[
            pl.BlockSpec((nb, tc, w), lambda g, di, j: (g, chunk(di, j), 0)),
            pl.BlockSpec((nb, BF16_ROWS, w), lambda g, di, j: (g, jnp.maximum(chunk(di, j) * hb - 1, 0), 0)),
            pl.BlockSpec((nb, BF16_ROWS, w), lambda g, di, j: (g, jnp.minimum((chunk(di, j) + 1) * hb, nhb - 1), 0)),
            pl.BlockSpec((None, nb, w), lambda g, di, j: (di, g, 0)),
            pl.BlockSpec(smat.shape, vec),
            pl.BlockSpec((CONV_W, w), vec),
            pl.BlockSpec((1, w), vec),
            pl.BlockSpec((None, LRU_BLOCKS, LANES, 2 * LANES), lambda g, di, j: (di, 0, 0, 0)),
            pl.BlockSpec((None, 1, w), dvec),
            pl.BlockSpec((None, 1, w), dvec),
            pl.BlockSpec((None, 1, w), dvec),
        ],
        out_specs=[
            pl.BlockSpec((None, nb, tc, w), lambda g, di, j: (di, g, chunk(di, j), 0)),
            pl.BlockSpec((None, nb, w), lambda g, di, j: (di, g, 0)),
        ],
        scratch_shapes=[
            pltpu.VMEM((nb, tc, w), F32),
            pltpu.VMEM((LRU_BLOCKS, nb * tc, LANES), F32),
            pltpu.VMEM((LRU_BLOCKS, nb * tc, LANES), F32),
            pltpu.VMEM((LRU_BLOCKS, nb, tc, LANES), F32),
            pltpu.VMEM((LRU_BLOCKS, nb, SUBLANES, LANES), F32),
        ],
        compiler_params=_cp("parallel", "arbitrary", "arbitrary"),
        name="rglru",
    )(x, x, x, h0, jnp.asarray(smat, BF16), lp["conv_w"], lp["conv_b"], lp["wab"], lp["ba"], lp["bx"], lp["lam"])


def _pair_rows(x2):
    low = lax.broadcasted_iota(jnp.int32, x2.shape, 1) < HEAD_DIM
    zero = jnp.zeros((), x2.dtype)
    return jnp.concatenate([jnp.where(low, x2, zero), jnp.where(low, zero, x2)], axis=0)


def _softmax_strips(s_ref, p_ref):
    sums = []
    for r in range(s_ref.shape[0] // BF16_ROWS):
        rows = slice(r * BF16_ROWS, (r + 1) * BF16_ROWS)
        s = s_ref[rows, :]
        p = jnp.exp(s - jnp.max(s, axis=-1, keepdims=True))
        sums.append(jnp.sum(p, axis=-1, keepdims=True))
        p_ref[rows, :] = p.astype(BF16)
    return sums


def _normalise_strips(o, sums):
    return jnp.concatenate([o[r * BF16_ROWS:(r + 1) * BF16_ROWS] / l for r, l in enumerate(sums)], axis=0)


def _na_kernel(var_ref, q_ref, k_ref, v_ref, kc_ref, vc_ref, bias_ref, o_ref, s_scr, p_scr,
               *, band, win_rows, kh, rows):
    del var_ref
    r0 = pl.program_id(1) * band
    ws = jnp.clip(r0 - kh // 2, 0, rows - win_rows)
    start = pl.multiple_of(ws * GRID_W, GRID_W)
    nwin = win_rows * GRID_W
    m = band * GRID_W
    npairs = q_ref.shape[-1] // LANES
    low = lax.broadcasted_iota(jnp.int32, (m, LANES), 1) < HEAD_DIM

    def scores(hp):
        sl = slice(hp * LANES, (hp + 1) * LANES)
        qs = _pair_rows(q_ref[:, sl] * jnp.asarray(HEAD_DIM ** -0.5, BF16))
        s_lat = lax.dot_general(qs, k_ref[pl.ds(start, nwin), sl], _NT, preferred_element_type=F32)
        s_scr[hp % 2, :, 0:nwin] = s_lat + bias_ref[2 * hp:2 * hp + 2].reshape(2 * m, nwin)
        s_scr[hp % 2, :, nwin:] = lax.dot_general(qs, kc_ref[:, sl], _NT, preferred_element_type=F32)

    def finish(hp):
        sl = slice(hp * LANES, (hp + 1) * LANES)
        sums = _softmax_strips(s_scr.at[hp % 2], p_scr.at[hp % 2])
        o = jnp.dot(p_scr[hp % 2, :, 0:nwin], v_ref[pl.ds(start, nwin), sl], preferred_element_type=F32)
        o = o + jnp.dot(p_scr[hp % 2, :, nwin:], vc_ref[:, sl], preferred_element_type=F32)
        o = _normalise_strips(o, sums)
        o_ref[:, sl] = jnp.where(low, o[:m], o[m:]).astype(o_ref.dtype)

    scores(0)
    for hp in range(npairs):
        if hp + 1 < npairs:
            scores(hp + 1)
        finish(hp)


def _na_bands(rows, band, win_rows):
    kh = min(NA_KH_MAX, rows)
    variants, var_of_band = [], []
    for r0 in range(0, rows, band):
        ws = int(np.clip(r0 - kh // 2, 0, rows - win_rows))
        key = (r0 - ws, tuple(int(np.clip(r0 + rr - kh // 2, 0, rows - kh)) - ws for rr in range(band)))
        if key not in variants:
            variants.append(key)
        var_of_band.append(variants.index(key))
    return variants, np.asarray(var_of_band, np.int32)


def _na_bias_table(rpb, rows, band, win_rows):
    kh = min(NA_KH_MAX, rows)
    col = np.arange(GRID_W)
    cs = np.clip(col - NA_KW // 2, 0, GRID_W - NA_KW)
    col_mask = (col[None, :] >= cs[:, None]) & (col[None, :] < cs[:, None] + NA_KW)
    dcol = np.clip(col[None, :] - col[:, None], -(NA_KW - 1), NA_KW - 1) + NA_KW - 1
    heads, n_drow, n_dcol = rpb.shape
    onehot = jnp.asarray(dcol.reshape(-1)[None, :] == np.arange(n_dcol)[:, None], F32)
    t2 = jnp.dot(rpb.reshape(heads * n_drow, n_dcol).astype(F32), onehot, precision=lax.Precision.HIGHEST)
    t2 = t2.reshape(heads, n_drow, GRID_W, GRID_W)
    t2 = jnp.where(col_mask[None, None], t2, NEG_INF)
    masked = jnp.full((heads, GRID_W, GRID_W), NEG_INF, F32)
    variants, _ = _na_bands(rows, band, win_rows)
    tabs = []
    for delta, first in variants:
        q_rows = []
        for rr in range(band):
            blocks = [t2[:, j - delta - rr + NA_KH_MAX - 1] if first[rr] <= j < first[rr] + kh else masked
                      for j in range(win_rows)]
            q_rows.append(jnp.concatenate(blocks, axis=-1))
        tabs.append(jnp.concatenate(q_rows, axis=1))
    return jnp.stack(tabs)


def _na(q, k, v, kc, vc, rpb):
    b, s, w = q.shape
    l = kc.shape[1]
    heads = w // HEAD_DIM
    rows = s // GRID_W
    kh = min(NA_KH_MAX, rows)
    band = min(4, rows)
    win_rows = min(-(-(kh + band - 1) // 2) * 2, rows)
    m, nwin = band * GRID_W, win_rows * GRID_W
    bias = _na_bias_table(rpb, rows, band, win_rows)
    _, var_of_band = _na_bands(rows, band, win_rows)

    full = lambda bi, r, var: (bi, 0, 0)
    grid_spec = pltpu.PrefetchScalarGridSpec(
        num_scalar_prefetch=1,
        grid=(b, rows // band),
        in_specs=[
            pl.BlockSpec((None, m, w), lambda bi, r, var: (bi, r, 0)),
            pl.BlockSpec((None, s, w), full),
            pl.BlockSpec((None, s, w), full),
            pl.BlockSpec((None, l, w), full),
            pl.BlockSpec((None, l, w), full),
            pl.BlockSpec((None, heads, m, nwin), lambda bi, r, var: (var[r], 0, 0, 0)),
        ],
        out_specs=pl.BlockSpec((None, m, w), lambda bi, r, var: (bi, r, 0)),
        scratch_shapes=[pltpu.VMEM((2, 2 * m, nwin + l), F32), pltpu.VMEM((2, 2 * m, nwin + l), BF16)],
    )
    return pl.pallas_call(
        functools.partial(_na_kernel, band=band, win_rows=win_rows, kh=kh, rows=rows),
        out_shape=jax.ShapeDtypeStruct((b, s, w), BF16),
        grid_spec=grid_spec,
        compiler_params=_cp("parallel", "arbitrary"),
        name="natten",
    )(jnp.asarray(var_of_band), q, k, v, kc, vc, bias)


def _head_norm(x, g, gmat):
    cols = []
    for c in range(x.shape[1] // LANES):
        xc = x[:, c * LANES:(c + 1) * LANES]
        x2 = xc * xc
        hi = x2.astype(BF16)
        lo = (x2 - hi.astype(F32)).astype(BF16)
        ms = jnp.dot(hi, gmat, preferred_element_type=F32) + jnp.dot(lo, gmat, preferred_element_type=F32)
        cols.append(xc * lax.rsqrt(ms + EPS) * g)
    return cols[0] if len(cols) == 1 else jnp.concatenate(cols, axis=-1)


def _rope(x, cos, sin):
    half = HEAD_DIM // 2
    lane = lax.broadcasted_iota(jnp.int32, (x.shape[0], LANES), 1)
    low = (lane % HEAD_DIM) < half
    cols = []
    for c in range(x.shape[1] // LANES):
        xc = x[:, c * LANES:(c + 1) * LANES]
        sw = jnp.where(low, pltpu.roll(xc, LANES - half, 1), pltpu.roll(xc, half, 1))
        cols.append(xc * cos + sw * sin)
    return cols[0] if len(cols) == 1 else jnp.concatenate(cols, axis=-1)


def _attn_kernel(*refs, norm, group, tq):
    if norm:
        q_ref, k_ref, v_ref, qg_ref, kg_ref, gmat_ref, o_ref = refs
    else:
        q_ref, k_ref, v_ref, o_ref = refs
    q = q_ref[...].astype(F32)
    k = k_ref[...]
    if norm:
        q = _head_norm(q, qg_ref[...], gmat_ref[...])
        k = _head_norm(k.astype(F32), kg_ref[...], gmat_ref[...]).astype(BF16)
    q = (q * (HEAD_DIM ** -0.5)).astype(BF16)

    outs = []
    for g in range(k_ref.shape[1] // HEAD_DIM):
        sl = slice(g * HEAD_DIM, (g + 1) * HEAD_DIM)
        qs = [q[:, (g * group + jj) * HEAD_DIM:(g * group + jj + 1) * HEAD_DIM] for jj in range(group)]
        qs = qs[0] if group == 1 else jnp.concatenate(qs, axis=0)
        s = lax.dot_general(qs, k[:, sl], _NT, preferred_element_type=F32)
        p = jnp.exp(s - jnp.max(s, axis=-1, keepdims=True))
        o = jnp.dot(p.astype(BF16), v_ref[:, sl], preferred_element_type=F32) / jnp.sum(p, axis=-1, keepdims=True)
        for jj in range(group):
            outs.append(o[jj * tq:(jj + 1) * tq])
    o_ref[...] = jnp.concatenate(outs, axis=-1).astype(o_ref.dtype)


def _dup_heads(x):
    low = lax.broadcasted_iota(jnp.int32, x.shape, 1) < HEAD_DIM
    sw = pltpu.roll(x, HEAD_DIM, 1)
    return jnp.where(low, x, sw), jnp.where(low, sw, x)


def _gqa_kernel(q_ref, kc_ref, vc_ref, kl_ref, vl_ref, qg_ref, kg_ref, gmat_ref, cosq_ref, sinq_ref, cosk_ref,
                sink_ref, o_ref, kc_scr, kl_scr, vc_scr, vl_scr, s_scr, p_scr, *, sub):
    n_ctx = kc_ref.shape[0]
    n_kv = GQA_KV_HEADS

    @pl.when(pl.program_id(1) == 0)
    def _():
        gmat = gmat_ref[...]
        kc = _head_norm(kc_ref[...].astype(F32), kg_ref[...], gmat)
        kl = _rope(_head_norm(kl_ref[...].astype(F32), kg_ref[...], gmat), cosk_ref[...], sink_ref[...])
        for scr, val in ((kc_scr, kc), (kl_scr, kl), (vc_scr, vc_ref[...].astype(F32)),
                         (vl_scr, vl_ref[...].astype(F32))):
            for g, dup in enumerate(_dup_heads(val)):
                scr[g] = dup.astype(BF16)

    tq = q_ref.shape[0]
    chunks = []
    for c in range(q_ref.shape[1] // LANES):
        qc = q_ref[:, c * LANES:(c + 1) * LANES].astype(F32)
        qc = _rope(_head_norm(qc, qg_ref[...], gmat_ref[...]), cosq_ref[...], sinq_ref[...])
        chunks.append((qc * (HEAD_DIM ** -0.5)).astype(BF16))
    pairs_per_group = len(chunks) // n_kv
    low = lax.broadcasted_iota(jnp.int32, (sub, LANES), 1) < HEAD_DIM
    units = [(t, g) for t in range(tq // sub) for g in range(n_kv)]

    def scores(u):
        t, g = units[u]
        rows = [_pair_rows(chunks[g * pairs_per_group + j][t * sub:(t + 1) * sub]) for j in range(pairs_per_group)]
        qs = jnp.concatenate(rows, axis=0)
        s_scr[u % 2, :, 0:n_ctx] = lax.dot_general(qs, kc_scr[g], _NT, preferred_element_type=F32)
        s_scr[u % 2, :, n_ctx:] = lax.dot_general(qs, kl_scr[g], _NT, preferred_element_type=F32)

    def finish(u):
        t, g = units[u]
        sums = _softmax_strips(s_scr.at[u % 2], p_scr.at[u % 2])
        o = jnp.dot(p_scr[u % 2, :, 0:n_ctx], vc_scr[g], preferred_element_type=F32)
        o = o + jnp.dot(p_scr[u % 2, :, n_ctx:], vl_scr[g], preferred_element_type=F32)
        o = _normalise_strips(o, sums)
        for j in range(pairs_per_group):
            c = g * pairs_per_group + j
            lo, hi = o[2 * j * sub:(2 * j + 1) * sub], o[(2 * j + 1) * sub:(2 * j + 2) * sub]
            o_ref[t * sub:(t + 1) * sub, c * LANES:(c + 1) * LANES] = jnp.where(low, lo, hi).astype(o_ref.dtype)

    scores(0)
    for u in range(len(units)):
        if u + 1 < len(units):
            scores(u + 1)
        finish(u)


def _gqa(q, kc, vc, kl, vl, qg, kg):
    b, t, wq = q.shape
    l, wk = kc.shape[1], kc.shape[2]
    s = kl.shape[1]
    assert wk == GQA_KV_HEADS * HEAD_DIM == LANES
    tq = min(t, 512)
    sub = min(tq, 256)
    m_unit = 2 * (wq // LANES // GQA_KV_HEADS) * sub
    reps = LANES // HEAD_DIM
    gmat = _head_mean_matrix()
    cos, sin = _rope_tables(t)
    full = lambda bi, i: (bi, 0, 0)
    const = lambda bi, i: (0, 0)
    return pl.pallas_call(
        functools.partial(_gqa_kernel, sub=sub),
        out_shape=jax.ShapeDtypeStruct((b, t, wq), BF16),
        grid=(b, t // tq),
        in_specs=[
            pl.BlockSpec((None, tq, wq), lambda bi, i: (bi, i, 0)),
            pl.BlockSpec((None, l, wk), full), pl.BlockSpec((None, l, wk), full),
            pl.BlockSpec((None, s, wk), full), pl.BlockSpec((None, s, wk), full),
            pl.BlockSpec((1, LANES), const), pl.BlockSpec((1, LANES), const), pl.BlockSpec((LANES, LANES), const),
            pl.BlockSpec((tq, LANES), lambda bi, i: (i, 0)), pl.BlockSpec((tq, LANES), lambda bi, i: (i, 0)),
            pl.BlockSpec((s, LANES), const), pl.BlockSpec((s, LANES), const),
        ],
        out_specs=pl.BlockSpec((None, tq, wq), lambda bi, i: (bi, i, 0)),
        scratch_shapes=[
            pltpu.VMEM((GQA_KV_HEADS, l, LANES), BF16), pltpu.VMEM((GQA_KV_HEADS, s, LANES), BF16),
            pltpu.VMEM((GQA_KV_HEADS, l, LANES), BF16), pltpu.VMEM((GQA_KV_HEADS, s, LANES), BF16),
            pltpu.VMEM((2, m_unit, l + s), F32), pltpu.VMEM((2, m_unit, l + s), BF16),
        ],
        compiler_params=_cp("parallel", "arbitrary"),
        name="gqa",
    )(q, kc, vc, kl, vl, jnp.tile(qg, reps).reshape(1, LANES), jnp.tile(kg, reps).reshape(1, LANES), gmat,
      cos, sin, cos, sin)


def _rope_tables(n_tokens):
    t = np.arange(n_tokens)
    pos = np.stack([t // GRID_W, t % GRID_W], axis=-1).astype(np.float32)
    n_freq = HEAD_DIM // 4
    inv_freq = jnp.asarray(ROPE_BASE, F32) ** (-jnp.arange(n_freq, dtype=F32) / n_freq)
    ang = (jnp.asarray(pos)[:, :, None] * inv_freq).reshape(n_tokens, 2 * n_freq)
    cos, sin = jnp.cos(ang), jnp.sin(ang)
    reps = LANES // HEAD_DIM
    return jnp.tile(jnp.concatenate([cos, cos], -1), (1, reps)), jnp.tile(jnp.concatenate([-sin, sin], -1), (1, reps))


def _head_mean_matrix():
    blk = np.arange(LANES) // HEAD_DIM
    return jnp.asarray((blk[:, None] == blk[None, :]).astype(np.float32) / HEAD_DIM, BF16)


def _attn(q, k, v, qg=None, kg=None):
    b, t, wq = q.shape
    wk = k.shape[2]
    norm = qg is not None
    tok = lambda w: pl.BlockSpec((None, t, w), lambda bi: (bi, 0, 0))
    args = [q, k, v]
    in_specs = [tok(wq), tok(wk), tok(wk)]
    if norm:
        reps = LANES // HEAD_DIM
        args += [jnp.tile(qg, reps).reshape(1, LANES), jnp.tile(kg, reps).reshape(1, LANES), _head_mean_matrix()]
        in_specs += [pl.BlockSpec((1, LANES), lambda bi: (0, 0))] * 2
        in_specs += [pl.BlockSpec((LANES, LANES), lambda bi: (0, 0))]
    return pl.pallas_call(
        functools.partial(_attn_kernel, norm=norm, group=wq // wk, tq=t),
        out_shape=jax.ShapeDtypeStruct((b, t, wq), BF16),
        grid=(b,),
        in_specs=in_specs,
        out_specs=tok(wq),
        compiler_params=_cp("parallel"),
        name="attn",
    )(*args)


def _merge_kernel(hf_ref, hr_ref, gl_ref, yb_ref, yc_ref, gt_ref, x_ref, g1_ref, sh2_ref, sc2_ref, ng1_ref, ng2_ref,
                  woa_ref, wob_ref, woc_ref, wout_ref, x1_ref, h2_ref, h2p_ref):
    d = x_ref.shape[-1]
    ya = (hf_ref[...].astype(F32) + hr_ref[...].astype(F32)) * gl_ref[...].astype(F32)
    pa = jnp.dot(ya.astype(BF16), woa_ref[...], preferred_element_type=F32)
    pb = jnp.dot(yb_ref[...], wob_ref[...], preferred_element_type=F32)
    pc = jnp.dot(yc_ref[...], woc_ref[...], preferred_element_type=F32)
    ga = gt_ref[:, 0:d].astype(F32)
    gb = gt_ref[:, d:2 * d].astype(F32)
    gc = gt_ref[:, 2 * d:3 * d].astype(F32)
    m = ga * pa + gb * pb + gc * pc
    y = jnp.dot(m.astype(BF16), wout_ref[...], preferred_element_type=F32)
    x1 = x_ref[...] + g1_ref[...] * _rms(y, ng1_ref[...])
    x1_ref[...] = x1
    h2 = (_rms(x1, ng2_ref[...]) * (1.0 + sc2_ref[...]) + sh2_ref[...]).astype(BF16)
    h2_ref[...] = h2
    h2p_ref[...] = _pack_halves(h2)


def _pack_halves(xb):
    half = xb.shape[1] // 2
    bits = pltpu.bitcast(xb.astype(F32), jnp.uint32)
    return bits[:, :half] | (bits[:, half:] >> 16)


def _unpack_halves(words):
    hi = pltpu.bitcast(words & jnp.uint32(0xFFFF0000), F32)
    lo = pltpu.bitcast(words << 16, F32)
    return jnp.concatenate([hi, lo], axis=1).astype(BF16)


def _merge(h, gl, yb, yc, gates, x, g1, sh2, sc2, ng1, ng2, woa, wob, woc, wout, mod_row):
    b, t, d = x.shape
    tm = min(t, 256)
    wl, wb, wc = gl.shape[2], yb.shape[2], yc.shape[2]
    if mod_row is None:
        mod_map = lambda bi, i: (bi, 0, 0)
    else:
        mod_map = lambda bi, i: (mod_row, 0, 0)
    tok = lambda w: pl.BlockSpec((None, tm, w), lambda bi, i: (bi, i, 0))
    mod = pl.BlockSpec((None, 1, d), mod_map)
    const = lambda r, c: pl.BlockSpec((r, c), lambda bi, i: (0, 0))
    return pl.pallas_call(
        _merge_kernel,
        out_shape=[jax.ShapeDtypeStruct((b, t, d), F32), jax.ShapeDtypeStruct((b, t, d), BF16),
                   jax.ShapeDtypeStruct((b, t, d // 2), jnp.uint32)],
        grid=(b, t // tm),
        in_specs=[
            pl.BlockSpec((None, None, tm, wl), lambda bi, i: (0, bi, i, 0)),
            pl.BlockSpec((None, None, tm, wl), lambda bi, i: (1, bi, i, 0)),
            tok(wl), tok(wb), tok(wc), tok(N_BRANCH * d), tok(d),
            mod, mod, mod, const(1, d), const(1, d),
            const(wl, d), const(wb, d), const(wc, d), const(d, d),
        ],
        out_specs=[tok(d), tok(d), tok(d // 2)],
        compiler_params=_cp("parallel", "parallel"),
        name="merge",
    )(h, h, gl, yb, yc, gates, x, g1, sh2, sc2, ng1.reshape(1, d), ng2.reshape(1, d), woa, wob, woc, wout)


def _lane_cumsum(x, tri):
    e, n = x.shape
    carry = jnp.zeros((e, 1), F32)
    cols = []
    for c in range(n // LANES):
        part = jnp.dot(x[:, c * LANES:(c + 1) * LANES].astype(BF16), tri, preferred_element_type=F32) + carry
        cols.append(part)
        carry = part[:, LANES - 1:LANES]
    return cols[0] if len(cols) == 1 else jnp.concatenate(cols, axis=-1)


def _route_kernel(h_ref, wr_ref, tri_ref, tok_ref, idx_ref, gcol_ref, pos_scr, aff_scr, *, cap):
    nbs, n_exp = h_ref.shape[0], wr_ref.shape[0]
    affs = []
    for s in range(nbs):
        logits = lax.dot_general(wr_ref[...], h_ref[s], _NT, preferred_element_type=F32)
        z = jnp.exp(logits - jnp.max(logits, axis=0, keepdims=True))
        affs.append(z / jnp.sum(z, axis=0, keepdims=True))
    aff = affs[0] if nbs == 1 else jnp.concatenate(affs, axis=0)
    aff_scr[...] = aff
    bits = pltpu.bitcast(aff, jnp.int32)
    e, n = bits.shape

    def bisect(_, lohi):
        lo, hi = lohi
        mid = lo + ((hi - lo + 1) >> 1)
        cnt = jnp.sum(jnp.where(bits >= mid, 1.0, 0.0), axis=1, keepdims=True)
        ok = cnt >= cap
        return jnp.where(ok, mid, lo), jnp.where(ok, hi, mid - 1)

    one_bits = 0x3F800000
    lo, _ = lax.fori_loop(0, 31, bisect, (jnp.zeros((e, 1), jnp.int32), jnp.full((e, 1), one_bits, jnp.int32)))
    gt = jnp.where(bits > lo, 1.0, 0.0)
    eq = jnp.where(bits == lo, 1.0, 0.0)
    need = cap - jnp.sum(gt, axis=1, keepdims=True)
    tri = tri_ref[...]
    sel = gt + jnp.where(_lane_cumsum(eq, tri) <= need, eq, 0.0)
    pos = _lane_cumsum(sel, tri) - 1.0
    pos_scr[...] = jnp.where(sel > 0.0, pos, -1.0).astype(jnp.int32)

    slot = lax.broadcasted_iota(jnp.int32, (cap, n), 0)

    def compact(r, carry):
        hit = pos_scr[pl.ds(r, 1), :] == slot
        onehot = jnp.where(hit, 1.0, 0.0).astype(BF16)
        digits = lax.dot_general(tok_ref[...], onehot, _NT, preferred_element_type=F32)
        s, ex = r // n_exp, r % n_exp
        idx_ref[s, pl.ds(ex, 1), :] = (digits[0:1] * GRID_W + digits[1:2]).astype(jnp.int32)
        gcol_ref[s, ex] = jnp.sum(jnp.where(hit, aff_scr[pl.ds(r, 1), :], 0.0), axis=1, keepdims=True)
        return carry

    lax.fori_loop(0, e, compact, 0)


def _route(h2, w_router_t, cap):
    b, n, d = h2.shape
    e = w_router_t.shape[0]
    nbs = 4 if b % 4 == 0 else 1
    tri = jnp.asarray(np.triu(np.ones((LANES, LANES), np.float32)), BF16)
    t = np.arange(n)
    tok = np.zeros((SUBLANES, n), np.float32)
    tok[0], tok[1] = t // GRID_W, t % GRID_W
    return pl.pallas_call(
        functools.partial(_route_kernel, cap=cap),
        out_shape=[jax.ShapeDtypeStruct((b, e, cap), jnp.int32), jax.ShapeDtypeStruct((b, e, cap, 1), F32)],
        grid=(b // nbs,),
        in_specs=[
            pl.BlockSpec((nbs, n, d), lambda bi: (bi, 0, 0)),
            pl.BlockSpec((e, d), lambda bi: (0, 0)),
            pl.BlockSpec((LANES, LANES), lambda bi: (0, 0)),
            pl.BlockSpec((SUBLANES, n), lambda bi: (0, 0)),
        ],
        out_specs=[pl.BlockSpec((nbs, e, cap), lambda bi: (bi, 0, 0)),
                   pl.BlockSpec((nbs, e, cap, 1), lambda bi: (bi, 0, 0, 0))],
        scratch_shapes=[pltpu.VMEM((nbs * e, n), jnp.int32), pltpu.VMEM((nbs * e, n), F32)],
        compiler_params=_cp("parallel"),
        name="route",
    )(h2, w_router_t, tri, jnp.asarray(tok, BF16))


def _expert_kernel(idx_smem, gcol_ref, h_ref, wg_ref, wu_ref, wd_ref, x_ref, g2_ref, ng_ref, o_ref,
                   hs_scr, y_scr, *, cap, nbs, n_exp, per_sample_gate):
    e = pl.program_id(1)
    n = h_ref.shape[1]

    def gather(expert, buf):
        for s in range(nbs):
            for j in range(cap):
                t = idx_smem[s, expert, j]
                hs_scr[buf, pl.ds(s * cap + j, 1), :] = h_ref[s, pl.ds(t, 1), :]

    def scatter_add(expert, buf):
        group = 8
        for s in range(nbs):
            for j0 in range(0, cap, group):
                ts = [idx_smem[s, expert, j0 + g] for g in range(group)]
                acc = [o_ref[s, pl.ds(t, 1), :] for t in ts]
                for g, t in enumerate(ts):
                    o_ref[s, pl.ds(t, 1), :] = acc[g] + y_scr[buf, pl.ds(s * cap + j0 + g, 1), :]

    @pl.when(e == 0)
    def _():
        o_ref[...] = jnp.zeros_like(o_ref)
        y_scr[1] = jnp.zeros(y_scr.shape[1:], F32)
        gather(0, 0)

    def step(cur):
        hs = _unpack_halves(hs_scr[cur])
        scatter_add(jnp.maximum(e - 1, 0), 1 - cur)
        gather(jnp.minimum(e + 1, n_exp - 1), 1 - cur)
        gcol = gcol_ref[...].reshape(nbs * cap, 1)
        a = jnp.dot(hs, wg_ref[...], preferred_element_type=F32)
        u = (a * _sigmoid(a)) * jnp.dot(hs, wu_ref[...], preferred_element_type=F32)
        y_scr[cur] = jnp.dot(u.astype(BF16), wd_ref[...], preferred_element_type=F32) * gcol

    for cur in range(2):
        pl.when(e % 2 == cur)(functools.partial(step, cur))

    rc = min(n, 256)

    @pl.when(e == n_exp - 1)
    def _():
        scatter_add(n_exp - 1, (n_exp - 1) % 2)
        for s in range(nbs):
            g2 = g2_ref[s if per_sample_gate else 0]
            for c in range(n // rc):
                sl = slice(c * rc, (c + 1) * rc)
                o_ref[s, sl, :] = x_ref[s, sl, :] + g2 * _rms(o_ref[s, sl, :], ng_ref[...])


def _experts(h2p, idx, gcol, wg, wu, wd, cap, x1, g2, ng, mod_row):
    b, n, dh = h2p.shape
    e, d, f = wg.shape
    nbs = max(1, min(b, 256 // cap))
    sample = lambda bi, ei: (bi, 0, 0)
    if mod_row is None:
        g2_spec = pl.BlockSpec((nbs, 1, d), sample)
    else:
        g2_spec = pl.BlockSpec((1, 1, d), lambda bi, ei: (mod_row, 0, 0))
    once = pl.Buffered(1)
    return pl.pallas_call(
        functools.partial(_expert_kernel, cap=cap, nbs=nbs, n_exp=e, per_sample_gate=mod_row is None),
        out_shape=jax.ShapeDtypeStruct((b, n, d), F32),
        grid=(b // nbs, e),
        in_specs=[
            pl.BlockSpec((nbs, e, cap), sample, memory_space=pltpu.SMEM),
            pl.BlockSpec((nbs, None, cap, 1), lambda bi, ei: (bi, ei, 0, 0)),
            pl.BlockSpec((nbs, n, dh), sample, pipeline_mode=once),
            pl.BlockSpec((None, d, f), lambda bi, ei: (ei, 0, 0)),
            pl.BlockSpec((None, d, f), lambda bi, ei: (ei, 0, 0)),
            pl.BlockSpec((None, f, d), lambda bi, ei: (ei, 0, 0)),
            pl.BlockSpec((nbs, n, d), sample),
            g2_spec,
            pl.BlockSpec((1, d), lambda bi, ei: (0, 0)),
        ],
        out_specs=pl.BlockSpec((nbs, n, d), sample),
        scratch_shapes=[pltpu.VMEM((2, nbs * cap, dh), jnp.uint32), pltpu.VMEM((2, nbs * cap, d), F32)],
        compiler_params=_cp("parallel", "arbitrary"),
        name="experts",
    )(idx, gcol, h2p, wg, wu, wd, x1, g2, ng.reshape(1, d))


def _split_cols(w, widths):
    offs = np.cumsum((0,) + tuple(widths))
    return [w[:, int(offs[i]):int(offs[i + 1])] for i in range(len(widths))]


def _layer(xl, xc, mods, p, need_ctx):
    bsz, s, d = xl.shape
    ctx_row = bsz
    sh1, sc1, g1, sh2, sc2, g2 = mods
    ng = p["norm_g"]
    na_w = (d // 128) * HEAD_DIM
    gq_w = (d // 128) * HEAD_DIM
    gkv_w = GQA_KV_HEADS * HEAD_DIM
    splits = (d, d, na_w, na_w, na_w, gq_w, gkv_w, gkv_w, N_BRANCH * d)
    acts = (None, "gelu", None, None, None, None, None, None, "sigmoid")
    w_in = p["w_in"].astype(BF16)

    lx, lg, nq, nk, nv, gq, gk, gv, gates = _inproj(xl, sh1, sc1, ng[0], w_in, splits, acts, None)
    if need_ctx:
        cx, cg, cnq, cnk, cnv, cgq, cgk, cgv, cgates = _inproj(xc, sh1, sc1, ng[0], w_in, splits, acts, ctx_row)
    else:
        parts = _split_cols(w_in, splits)
        kv_parts = (0, 3, 4, 6, 7)
        w_kv = jnp.concatenate([parts[i] for i in kv_parts], axis=1)
        cx, cnk, cnv, cgk, cgv = _inproj(xc, sh1, sc1, ng[0], w_kv, tuple(splits[i] for i in kv_parts),
                                         tuple(acts[i] for i in kv_parts), ctx_row)

    lp = {
        "conv_w": p["conv_w"], "conv_b": p["conv_b"].reshape(1, d),
        "wab": jnp.concatenate([p["lru_wa"], p["lru_wx"]], axis=-1).astype(BF16),
        "ba": p["lru_ba"].reshape(2, 1, d), "bx": p["lru_bx"].reshape(2, 1, d),
        "lam": p["lru_lam"].reshape(2, 1, d),
    }
    hc, hfin = _lru(cx, jnp.zeros((2, bsz, d), F32), lp)
    hl, _ = _lru(lx, hfin, lp)

    yb = _na(nq, nk, nv, cnk, cnv, p["na_rpb"])
    yc = _gqa(gq, cgk, cgv, gk, gv, p["qn_g"], p["kn_g"])

    woa, wob, woc, wout = (p[k].astype(BF16) for k in ("w_o_a", "w_o_b", "w_o_c", "w_out"))
    wr_t = p["w_router"].T.astype(BF16)
    wg, wu, wd = (p[k].astype(BF16) for k in ("w_gate", "w_up", "w_down"))

    def ffn(x1, h2, h2p, mod_row):
        n = x1.shape[1]
        cap = EC_CAPACITY * n // N_EXPERTS
        idx, gcol = _route(h2, wr_t, cap)
        return _experts(h2p, idx, gcol, wg, wu, wd, cap, x1, g2, ng[3], mod_row)

    xl = ffn(*_merge(hl, lg, yb, yc, gates, xl, g1, sh2, sc2, ng[1], ng[2], woa, wob, woc, wout, None), None)
    if need_ctx:
        ybc = _attn(cnq, cnk, cnv)
        ycc = _attn(cgq, cgk, cgv, qg=p["qn_g"], kg=p["kn_g"])
        xc = ffn(*_merge(hc, cg, ybc, ycc, cgates, xc, g1, sh2, sc2, ng[1], ng[2], woa, wob, woc, wout, ctx_row),
                 ctx_row)
    return xl, xc


def kernel(x, c, ctx, c_ctx, w_mod, b_mod, norm_g, w_in, conv_w, conv_b, lru_wa, lru_ba, lru_wx, lru_bx, lru_lam,
           na_rpb, qn_g, kn_g, w_o_a, w_o_b, w_o_c, w_out, w_router, w_gate, w_up, w_down):
    bsz, _, d = x.shape
    depth = w_mod.shape[0]
    rows = -(-(bsz + 1) // SUBLANES) * SUBLANES
    cond = jnp.zeros((rows, d), F32).at[:bsz].set(c).at[bsz].set(c_ctx)
    xl, xc = x, ctx
    for l in range(depth):
        p = {
            "norm_g": norm_g[l], "w_in": w_in[l], "conv_w": conv_w[l], "conv_b": conv_b[l],
            "lru_wa": lru_wa[l], "lru_ba": lru_ba[l], "lru_wx": lru_wx[l], "lru_bx": lru_bx[l],
            "lru_lam": lru_lam[l], "na_rpb": na_rpb[l], "qn_g": qn_g[l], "kn_g": kn_g[l],
            "w_o_a": w_o_a[l], "w_o_b": w_o_b[l], "w_o_c": w_o_c[l], "w_out": w_out[l],
            "w_router": w_router[l], "w_gate": w_gate[l], "w_up": w_up[l], "w_down": w_down[l],
        }
        m = _adaln(cond, w_mod[l], b_mod[l])
        mods = [m[:, i * d:(i + 1) * d].reshape(rows, 1, d) for i in range(6)]
        xl, xc = _layer(xl, xc, mods, p, l < depth - 1)
    return xl
```

```python
import functools

import jax
import jax.numpy as jnp
import numpy as np
from jax import lax
from jax.experimental import pallas as pl
from jax.experimental.pallas import tpu as pltpu

F32 = jnp.float32
BF16 = jnp.bfloat16

GRID_W = 64
HEAD_DIM = 64
LRU_BLOCKS = 8
CONV_W = 4
LRU_C = 8.0
NA_KH_MAX = 8
NA_KW = 16
GQA_KV_HEADS = 2
ROPE_BASE = 10000.0
N_EXPERTS = 16
EC_CAPACITY = 2
N_BRANCH = 3
EPS = 1e-6
NEG_INF = -1e30

LANES = 128
SUBLANES = 8
BF16_ROWS = 16
VMEM_LIMIT_BYTES = 56 * 1024 * 1024

_NT = (((1,), (1,)), ((), ()))


def _cp(*sem):
    return pltpu.CompilerParams(dimension_semantics=sem, vmem_limit_bytes=VMEM_LIMIT_BYTES)


def _sigmoid(x):
    return 0.5 * jnp.tanh(0.5 * x) + 0.5


def _rms(x, g):
    return x * lax.rsqrt(jnp.mean(x * x, axis=-1, keepdims=True) + EPS) * g


def _adaln_kernel(c_ref, w_ref, b_ref, o_ref):
    c = c_ref[...]
    s = (c * _sigmoid(c)).astype(BF16)
    o_ref[...] = jnp.dot(s, w_ref[...].astype(BF16), preferred_element_type=F32) + b_ref[...]


def _adaln(cond, w_mod, b_mod):
    r, d = cond.shape
    n = w_mod.shape[1]
    tn = n // 4
    return pl.pallas_call(
        _adaln_kernel,
        out_shape=jax.ShapeDtypeStruct((r, n), F32),
        grid=(n // tn,),
        in_specs=[
            pl.BlockSpec((r, d), lambda j: (0, 0)),
            pl.BlockSpec((d, tn), lambda j: (0, j)),
            pl.BlockSpec((1, tn), lambda j: (0, j)),
        ],
        out_specs=pl.BlockSpec((r, tn), lambda j: (0, j)),
        compiler_params=_cp("parallel"),
        name="adaln",
    )(cond, w_mod, b_mod.reshape(1, n))


_ACTIVATIONS = {None: lambda v: v, "gelu": jax.nn.gelu, "sigmoid": _sigmoid}


def _inproj_kernel(x_ref, sh_ref, sc_ref, g_ref, w_ref, *o_refs, splits, acts):
    h = (_rms(x_ref[...], g_ref[...]) * (1.0 + sc_ref[...]) + sh_ref[...]).astype(BF16)
    off = 0
    for o_ref, width, act in zip(o_refs, splits, acts):
        y = _ACTIVATIONS[act](jnp.dot(h, w_ref[:, off:off + width], preferred_element_type=F32))
        o_ref[...] = y.astype(o_ref.dtype)
        off += width


def _inproj(x, shift, scale, gain, w, splits, acts, mod_row):
    b, t, d = x.shape
    tm = min(t, 512)
    n = w.shape[1]
    if mod_row is None:
        mod_map = lambda bi, i: (bi, 0, 0)
    else:
        mod_map = lambda bi, i: (mod_row, 0, 0)
    return pl.pallas_call(
        functools.partial(_inproj_kernel, splits=splits, acts=acts),
        out_shape=[jax.ShapeDtypeStruct((b, t, s), BF16) for s in splits],
        grid=(b, t // tm),
        in_specs=[
            pl.BlockSpec((None, tm, d), lambda bi, i: (bi, i, 0)),
            pl.BlockSpec((None, 1, d), mod_map),
            pl.BlockSpec((None, 1, d), mod_map),
            pl.BlockSpec((1, d), lambda bi, i: (0, 0)),
            pl.BlockSpec((d, n), lambda bi, i: (0, 0)),
        ],
        out_specs=[pl.BlockSpec((None, tm, s), lambda bi, i: (bi, i, 0)) for s in splits],
        compiler_params=_cp("parallel", "parallel"),
        name="inproj",
    )(x, shift, scale, gain.reshape(1, d), w)


def _lru_kernel(x_ref, xp_ref, xn_ref, h0_ref, smat_ref, cw_ref, cb_ref, wab_ref, ba_ref, bx_ref, lam_ref,
                y_ref, hfin_ref, u_scr, a_scr, b_scr, h_scr, st_scr, *, tc, nchunks, nb):
    d = pl.program_id(1)
    j = pl.program_id(2)
    c = jnp.where(d == 0, j, nchunks - 1 - j)
    w = x_ref.shape[-1]
    cwide = 2 * LANES

    @pl.when(j == 0)
    def _():
        for cb in range(LRU_BLOCKS):
            h0 = h0_ref[:, cb * LANES:(cb + 1) * LANES]
            st_scr[cb] = jnp.broadcast_to(h0[:, None, :], (nb, SUBLANES, LANES))

    zero = jnp.zeros((), BF16)
    c_lam = -LRU_C * jax.nn.softplus(-lam_ref[...])
    for cc in range(w // cwide):
        sl2 = slice(cc * cwide, (cc + 1) * cwide)
        for bb in range(nb):
            xe = jnp.concatenate([jnp.where(c > 0, xp_ref[bb, :, sl2], zero), x_ref[bb, :, sl2],
                                  jnp.where(c < nchunks - 1, xn_ref[bb, :, sl2], zero)], axis=0)
            sh = jnp.dot(smat_ref[...], xe, preferred_element_type=F32)
            u = cb_ref[:, sl2]
            for k in range(CONV_W):
                u = u + sh[k * tc:(k + 1) * tc] * cw_ref[k:k + 1, sl2]
            u_scr[bb, :, sl2] = u
        for cb in range(cc * (cwide // LANES), (cc + 1) * (cwide // LANES)):
            sl = slice(cb * LANES, (cb + 1) * LANES)
            u = u_scr[:, :, sl].reshape(nb * tc, LANES)
            gates = jnp.dot(u.astype(BF16), wab_ref[cb], preferred_element_type=F32)
            r = _sigmoid(gates[:, :LANES] + ba_ref[:, sl])
            i = _sigmoid(gates[:, LANES:] + bx_ref[:, sl])
            log_a = c_lam[:, sl] * r
            a = jnp.exp(log_a)
            a_scr[cb] = a
            m = -jnp.tanh(log_a) * (a * a + 1.0)
            b_scr[cb] = jnp.where(m > 0.0, m * lax.rsqrt(m), 0.0) * (i * u)

    @pl.when(d == 0)
    def _():
        _block_scan(a_scr, b_scr, h_scr, st_scr, False, nb, tc)

    @pl.when(d == 1)
    def _():
        _block_scan(a_scr, b_scr, h_scr, st_scr, True, nb, tc)

    for cb in range(LRU_BLOCKS):
        y_ref[:, :, cb * LANES:(cb + 1) * LANES] = h_scr[cb].astype(y_ref.dtype)

    @pl.when(j == nchunks - 1)
    def _():
        for cb in range(LRU_BLOCKS):
            hfin_ref[:, cb * LANES:(cb + 1) * LANES] = st_scr[cb][:, 0, :]


def _block_scan(a_scr, b_scr, h_scr, st_scr, rev, nb, tc):
    nblk = tc // SUBLANES
    row = lax.broadcasted_iota(jnp.int32, (1, SUBLANES, LANES), 1)
    for cb in range(LRU_BLOCKS):
        a = a_scr[cb].reshape(nb * nblk, SUBLANES, LANES)
        b = b_scr[cb].reshape(nb * nblk, SUBLANES, LANES)
        for s in (1, 2, 4):
            shift, keep = (SUBLANES - s, row < SUBLANES - s) if rev else (s, row >= s)
            b = b + a * jnp.where(keep, pltpu.roll(b, shift, 1), 0.0)
            a = a * jnp.where(keep, pltpu.roll(a, shift, 1), 1.0)
        a = a.reshape(nb, nblk, SUBLANES, LANES)
        b = b.reshape(nb, nblk, SUBLANES, LANES)
        carry = st_scr[cb]
        last = 0 if rev else SUBLANES - 1
        for k in (reversed(range(nblk)) if rev else range(nblk)):
            h = a[:, k] * carry + b[:, k]
            h_scr[cb, :, k * SUBLANES:(k + 1) * SUBLANES, :] = h
            carry = jnp.broadcast_to(h[:, last:last + 1, :], h.shape)
        st_scr[cb] = carry


def _lru(x, h0, lp):
    b, t, w = x.shape
    nb = SUBLANES
    tc = min(t, 128)
    nchunks = t // tc
    hb = tc // BF16_ROWS
    nhb = t // BF16_ROWS

    def chunk(di, j):
        return jnp.where(di == 0, j, nchunks - 1 - j)

    smat = np.zeros((CONV_W * tc, tc + 2 * BF16_ROWS), np.float32)
    for k in range(CONV_W):
        smat[k * tc + np.arange(tc), BF16_ROWS + np.arange(tc) + k - CONV_W // 2] = 1.0

    vec = lambda g, di, j: (0, 0)
    dvec = lambda g, di, j: (di, 0, 0)
    return pl.pallas_call(
        functools.partial(_lru_kernel, tc=tc, nchunks=nchunks, nb=nb),
        out_shape=[jax.ShapeDtypeStruct((2, b, t, w), BF16), jax.ShapeDtypeStruct((2, b, w), F32)],
        grid=(b // nb, 2, nchunks),
        in_specs=[
            pl.BlockSpec((nb, tc, w), lambda g, di, j: (g, chunk(di, j), 0)),
            pl.BlockSpec((nb, BF16_ROWS, w), lambda g, di, j: (g, jnp.maximum(chunk(di, j) * hb - 1, 0), 0)),
            pl.BlockSpec((nb, BF16_ROWS, w), lambda g, di, j: (g, jnp.minimum((chunk(di, j) + 1) * hb, nhb - 1), 0)),
            pl.BlockSpec((None, nb, w), lambda g, di, j: (di, g, 0)),
            pl.BlockSpec(smat.shape, vec),
            pl.BlockSpec((CONV_W, w), vec),
            pl.BlockSpec((1, w), vec),
            pl.BlockSpec((None, LRU_BLOCKS, LANES, 2 * LANES), lambda g, di, j: (di, 0, 0, 0)),
            pl.BlockSpec((None, 1, w), dvec),
            pl.BlockSpec((None, 1, w), dvec),
            pl.BlockSpec((None, 1, w), dvec),
        ],
        out_specs=[
            pl.BlockSpec((None, nb, tc, w), lambda g, di, j: (di, g, chunk(di, j), 0)),
            pl.BlockSpec((None, nb, w), lambda g, di, j: (di, g, 0)),
        ],
        scratch_shapes=[
            pltpu.VMEM((nb, tc, w), F32),
            pltpu.VMEM((LRU_BLOCKS, nb * tc, LANES), F32),
            pltpu.VMEM((LRU_BLOCKS, nb * tc, LANES), F32),
            pltpu.VMEM((LRU_BLOCKS, nb, tc, LANES), F32),
            pltpu.VMEM((LRU_BLOCKS, nb, SUBLANES, LANES), F32),
        ],
        compiler_params=_cp("parallel", "arbitrary", "arbitrary"),
        name="rglru",
    )(x, x, x, h0, jnp.asarray(smat, BF16), lp["conv_w"], lp["conv_b"], lp["wab"], lp["ba"], lp["bx"], lp["lam"])


def _pair_rows(x2):
    low = lax.broadcasted_iota(jnp.int32, x2.shape, 1) < HEAD_DIM
    zero = jnp.zeros((), x2.dtype)
    return jnp.concatenate([jnp.where(low, x2, zero), jnp.where(low, zero, x2)], axis=0)


def _softmax_strips(s_ref, p_ref):
    sums = []
    for r in range(s_ref.shape[0] // BF16_ROWS):
        rows = slice(r * BF16_ROWS, (r + 1) * BF16_ROWS)
        s = s_ref[rows, :]
        p = jnp.exp(s - jnp.max(s, axis=-1, keepdims=True))
        sums.append(jnp.sum(p, axis=-1, keepdims=True))
        p_ref[rows, :] = p.astype(BF16)
    return sums


def _normalise_strips(o, sums):
    return jnp.concatenate([o[r * BF16_ROWS:(r + 1) * BF16_ROWS] / l for r, l in enumerate(sums)], axis=0)


def _na_kernel(var_ref, q_ref, k_ref, v_ref, kc_ref, vc_ref, bias_ref, o_ref, s_scr, p_scr,
               *, band, win_rows, kh, rows):
    del var_ref
    r0 = pl.program_id(1) * band
    ws = jnp.clip(r0 - kh // 2, 0, rows - win_rows)
    start = pl.multiple_of(ws * GRID_W, GRID_W)
    nwin = win_rows * GRID_W
    m = band * GRID_W
    npairs = q_ref.shape[-1] // LANES
    low = lax.broadcasted_iota(jnp.int32, (m, LANES), 1) < HEAD_DIM

    def scores(hp):
        sl = slice(hp * LANES, (hp + 1) * LANES)
        qs = _pair_rows(q_ref[:, sl] * jnp.asarray(HEAD_DIM ** -0.5, BF16))
        s_lat = lax.dot_general(qs, k_ref[pl.ds(start, nwin), sl], _NT, preferred_element_type=F32)
        s_scr[hp % 2, :, 0:nwin] = s_lat + bias_ref[2 * hp:2 * hp + 2].reshape(2 * m, nwin)
        s_scr[hp % 2, :, nwin:] = lax.dot_general(qs, kc_ref[:, sl], _NT, preferred_element_type=F32)

    def finish(hp):
        sl = slice(hp * LANES, (hp + 1) * LANES)
        sums = _softmax_strips(s_scr.at[hp % 2], p_scr.at[hp % 2])
        o = jnp.dot(p_scr[hp % 2, :, 0:nwin], v_ref[pl.ds(start, nwin), sl], preferred_element_type=F32)
        o = o + jnp.dot(p_scr[hp % 2, :, nwin:], vc_ref[:, sl], preferred_element_type=F32)
        o = _normalise_strips(o, sums)
        o_ref[:, sl] = jnp.where(low, o[:m], o[m:]).astype(o_ref.dtype)

    scores(0)
    for hp in range(npairs):
        if hp + 1 < npairs:
            scores(hp + 1)
        finish(hp)


def _na_bands(rows, band, win_rows):
    kh = min(NA_KH_MAX, rows)
    variants, var_of_band = [], []
    for r0 in range(0, rows, band):
        ws = int(np.clip(r0 - kh // 2, 0, rows - win_rows))
        key = (r0 - ws, tuple(int(np.clip(r0 + rr - kh // 2, 0, rows - kh)) - ws for rr in range(band)))
        if key not in variants:
            variants.append(key)
        var_of_band.append(variants.index(key))
    return variants, np.asarray(var_of_band, np.int32)


def _na_bias_table(rpb, rows, band, win_rows):
    kh = min(NA_KH_MAX, rows)
    col = np.arange(GRID_W)
    cs = np.clip(col - NA_KW // 2, 0, GRID_W - NA_KW)
    col_mask = (col[None, :] >= cs[:, None]) & (col[None, :] < cs[:, None] + NA_KW)
    dcol = np.clip(col[None, :] - col[:, None], -(NA_KW - 1), NA_KW - 1) + NA_KW - 1
    heads, n_drow, n_dcol = rpb.shape
    onehot = jnp.asarray(dcol.reshape(-1)[None, :] == np.arange(n_dcol)[:, None], F32)
    t2 = jnp.dot(rpb.reshape(heads * n_drow, n_dcol).astype(F32), onehot, precision=lax.Precision.HIGHEST)
    t2 = t2.reshape(heads, n_drow, GRID_W, GRID_W)
    t2 = jnp.where(col_mask[None, None], t2, NEG_INF)
    masked = jnp.full((heads, GRID_W, GRID_W), NEG_INF, F32)
    variants, _ = _na_bands(rows, band, win_rows)
    tabs = []
    for delta, first in variants:
        q_rows = []
        for rr in range(band):
            blocks = [t2[:, j - delta - rr + NA_KH_MAX - 1] if first[rr] <= j < first[rr] + kh else masked
                      for j in range(win_rows)]
            q_rows.append(jnp.concatenate(blocks, axis=-1))
        tabs.append(jnp.concatenate(q_rows, axis=1))
    return jnp.stack(tabs)


def _na(q, k, v, kc, vc, rpb):
    b, s, w = q.shape
    l = kc.shape[1]
    heads = w // HEAD_DIM
    rows = s // GRID_W
    kh = min(NA_KH_MAX, rows)
    band = min(4, rows)
    win_rows = min(-(-(kh + band - 1) // 2) * 2, rows)
    m, nwin = band * GRID_W, win_rows * GRID_W
    bias = _na_bias_table(rpb, rows, band, win_rows)
    _, var_of_band = _na_bands(rows, band, win_rows)

    full = lambda bi, r, var: (bi, 0, 0)
    grid_spec = pltpu.PrefetchScalarGridSpec(
        num_scalar_prefetch=1,
        grid=(b, rows // band),
        in_specs=[
            pl.BlockSpec((None, m, w), lambda bi, r, var: (bi, r, 0)),
            pl.BlockSpec((None, s, w), full),
            pl.BlockSpec((None, s, w), full),
            pl.BlockSpec((None, l, w), full),
            pl.BlockSpec((None, l, w), full),
            pl.BlockSpec((None, heads, m, nwin), lambda bi, r, var: (var[r], 0, 0, 0)),
        ],
        out_specs=pl.BlockSpec((None, m, w), lambda bi, r, var: (bi, r, 0)),
        scratch_shapes=[pltpu.VMEM((2, 2 * m, nwin + l), F32), pltpu.VMEM((2, 2 * m, nwin + l), BF16)],
    )
    return pl.pallas_call(
        functools.partial(_na_kernel, band=band, win_rows=win_rows, kh=kh, rows=rows),
        out_shape=jax.ShapeDtypeStruct((b, s, w), BF16),
        grid_spec=grid_spec,
        compiler_params=_cp("parallel", "arbitrary"),
        name="natten",
    )(jnp.asarray(var_of_band), q, k, v, kc, vc, bias)


def _head_norm(x, g, gmat):
    cols = []
    for c in range(x.shape[1] // LANES):
        xc = x[:, c * LANES:(c + 1) * LANES]
        x2 = xc * xc
        hi = x2.astype(BF16)
        lo = (x2 - hi.astype(F32)).astype(BF16)
        ms = jnp.dot(hi, gmat, preferred_element_type=F32) + jnp.dot(lo, gmat, preferred_element_type=F32)
        cols.append(xc * lax.rsqrt(ms + EPS) * g)
    return cols[0] if len(cols) == 1 else jnp.concatenate(cols, axis=-1)


def _rope(x, cos, sin):
    half = HEAD_DIM // 2
    lane = lax.broadcasted_iota(jnp.int32, (x.shape[0], LANES), 1)
    low = (lane % HEAD_DIM) < half
    cols = []
    for c in range(x.shape[1] // LANES):
        xc = x[:, c * LANES:(c + 1) * LANES]
        sw = jnp.where(low, pltpu.roll(xc, LANES - half, 1), pltpu.roll(xc, half, 1))
        cols.append(xc * cos + sw * sin)
    return cols[0] if len(cols) == 1 else jnp.concatenate(cols, axis=-1)


def _attn_kernel(*refs, norm, group, tq):
    if norm:
        q_ref, k_ref, v_ref, qg_ref, kg_ref, gmat_ref, o_ref = refs
    else:
        q_ref, k_ref, v_ref, o_ref = refs
    q = q_ref[...].astype(F32)
    k = k_ref[...]
    if norm:
        q = _head_norm(q, qg_ref[...], gmat_ref[...])
        k = _head_norm(k.astype(F32), kg_ref[...], gmat_ref[...]).astype(BF16)
    q = (q * (HEAD_DIM ** -0.5)).astype(BF16)

    outs = []
    for g in range(k_ref.shape[1] // HEAD_DIM):
        sl = slice(g * HEAD_DIM, (g + 1) * HEAD_DIM)
        qs = [q[:, (g * group + jj) * HEAD_DIM:(g * group + jj + 1) * HEAD_DIM] for jj in range(group)]
        qs = qs[0] if group == 1 else jnp.concatenate(qs, axis=0)
        s = lax.dot_general(qs, k[:, sl], _NT, preferred_element_type=F32)
        p = jnp.exp(s - jnp.max(s, axis=-1, keepdims=True))
        o = jnp.dot(p.astype(BF16), v_ref[:, sl], preferred_element_type=F32) / jnp.sum(p, axis=-1, keepdims=True)
        for jj in range(group):
            outs.append(o[jj * tq:(jj + 1) * tq])
    o_ref[...] = jnp.concatenate(outs, axis=-1).astype(o_ref.dtype)


def _dup_heads(x):
    low = lax.broadcasted_iota(jnp.int32, x.shape, 1) < HEAD_DIM
    sw = pltpu.roll(x, HEAD_DIM, 1)
    return jnp.where(low, x, sw), jnp.where(low, sw, x)


def _gqa_kernel(q_ref, kc_ref, vc_ref, kl_ref, vl_ref, qg_ref, kg_ref, gmat_ref, cosq_ref, sinq_ref, cosk_ref,
                sink_ref, o_ref, kc_scr, kl_scr, vc_scr, vl_scr, s_scr, p_scr, *, sub):
    n_ctx = kc_ref.shape[0]
    n_kv = GQA_KV_HEADS

    @pl.when(pl.program_id(1) == 0)
    def _():
        gmat = gmat_ref[...]
        kc = _head_norm(kc_ref[...].astype(F32), kg_ref[...], gmat)
        kl = _rope(_head_norm(kl_ref[...].astype(F32), kg_ref[...], gmat), cosk_ref[...], sink_ref[...])
        for scr, val in ((kc_scr, kc), (kl_scr, kl), (vc_scr, vc_ref[...].astype(F32)),
                         (vl_scr, vl_ref[...].astype(F32))):
            for g, dup in enumerate(_dup_heads(val)):
                scr[g] = dup.astype(BF16)

    tq = q_ref.shape[0]
    chunks = []
    for c in range(q_ref.shape[1] // LANES):
        qc = q_ref[:, c * LANES:(c + 1) * LANES].astype(F32)
        qc = _rope(_head_norm(qc, qg_ref[...], gmat_ref[...]), cosq_ref[...], sinq_ref[...])
        chunks.append((qc * (HEAD_DIM ** -0.5)).astype(BF16))
    pairs_per_group = len(chunks) // n_kv
    low = lax.broadcasted_iota(jnp.int32, (sub, LANES), 1) < HEAD_DIM
    units = [(t, g) for t in range(tq // sub) for g in range(n_kv)]

    def scores(u):
        t, g = units[u]
        rows = [_pair_rows(chunks[g * pairs_per_group + j][t * sub:(t + 1) * sub]) for j in range(pairs_per_group)]
        qs = jnp.concatenate(rows, axis=0)
        s_scr[u % 2, :, 0:n_ctx] = lax.dot_general(qs, kc_scr[g], _NT, preferred_element_type=F32)
        s_scr[u % 2, :, n_ctx:] = lax.dot_general(qs, kl_scr[g], _NT, preferred_element_type=F32)

    def finish(u):
        t, g = units[u]
        sums = _softmax_strips(s_scr.at[u % 2], p_scr.at[u % 2])
        o = jnp.dot(p_scr[u % 2, :, 0:n_ctx], vc_scr[g], preferred_element_type=F32)
        o = o + jnp.dot(p_scr[u % 2, :, n_ctx:], vl_scr[g], preferred_element_type=F32)
        o = _normalise_strips(o, sums)
        for j in range(pairs_per_group):
            c = g * pairs_per_group + j
            lo, hi = o[2 * j * sub:(2 * j + 1) * sub], o[(2 * j + 1) * sub:(2 * j + 2) * sub]
            o_ref[t * sub:(t + 1) * sub, c * LANES:(c + 1) * LANES] = jnp.where(low, lo, hi).astype(o_ref.dtype)

    scores(0)
    for u in range(len(units)):
        if u + 1 < len(units):
            scores(u + 1)
        finish(u)


def _gqa(q, kc, vc, kl, vl, qg, kg):
    b, t, wq = q.shape
    l, wk = kc.shape[1], kc.shape[2]
    s = kl.shape[1]
    assert wk == GQA_KV_HEADS * HEAD_DIM == LANES
    tq = min(t, 512)
    sub = min(tq, 256)
    m_unit = 2 * (wq // LANES // GQA_KV_HEADS) * sub
    reps = LANES // HEAD_DIM
    gmat = _head_mean_matrix()
    cos, sin = _rope_tables(t)
    full = lambda bi, i: (bi, 0, 0)
    const = lambda bi, i: (0, 0)
    return pl.pallas_call(
        functools.partial(_gqa_kernel, sub=sub),
        out_shape=jax.ShapeDtypeStruct((b, t, wq), BF16),
        grid=(b, t // tq),
        in_specs=[
            pl.BlockSpec((None, tq, wq), lambda bi, i: (bi, i, 0)),
            pl.BlockSpec((None, l, wk), full), pl.BlockSpec((None, l, wk), full),
            pl.BlockSpec((None, s, wk), full), pl.BlockSpec((None, s, wk), full),
            pl.BlockSpec((1, LANES), const), pl.BlockSpec((1, LANES), const), pl.BlockSpec((LANES, LANES), const),
            pl.BlockSpec((tq, LANES), lambda bi, i: (i, 0)), pl.BlockSpec((tq, LANES), lambda bi, i: (i, 0)),
            pl.BlockSpec((s, LANES), const), pl.BlockSpec((s, LANES), const),
        ],
        out_specs=pl.BlockSpec((None, tq, wq), lambda bi, i: (bi, i, 0)),
        scratch_shapes=[
            pltpu.VMEM((GQA_KV_HEADS, l, LANES), BF16), pltpu.VMEM((GQA_KV_HEADS, s, LANES), BF16),
            pltpu.VMEM((GQA_KV_HEADS, l, LANES), BF16), pltpu.VMEM((GQA_KV_HEADS, s, LANES), BF16),
            pltpu.VMEM((2, m_unit, l + s), F32), pltpu.VMEM((2, m_unit, l + s), BF16),
        ],
        compiler_params=_cp("parallel", "arbitrary"),
        name="gqa",
    )(q, kc, vc, kl, vl, jnp.tile(qg, reps).reshape(1, LANES), jnp.tile(kg, reps).reshape(1, LANES), gmat,
      cos, sin, cos, sin)


def _rope_tables(n_tokens):
    t = np.arange(n_tokens)
    pos = np.stack([t // GRID_W, t % GRID_W], axis=-1).astype(np.float32)
    n_freq = HEAD_DIM // 4
    inv_freq = jnp.asarray(ROPE_BASE, F32) ** (-jnp.arange(n_freq, dtype=F32) / n_freq)
    ang = (jnp.asarray(pos)[:, :, None] * inv_freq).reshape(n_tokens, 2 * n_freq)
    cos, sin = jnp.cos(ang), jnp.sin(ang)
    reps = LANES // HEAD_DIM
    return jnp.tile(jnp.concatenate([cos, cos], -1), (1, reps)), jnp.tile(jnp.concatenate([-sin, sin], -1), (1, reps))


def _head_mean_matrix():
    blk = np.arange(LANES) // HEAD_DIM
    return jnp.asarray((blk[:, None] == blk[None, :]).astype(np.float32) / HEAD_DIM, BF16)


def _attn(q, k, v, qg=None, kg=None):
    b, t, wq = q.shape
    wk = k.shape[2]
    norm = qg is not None
    tok = lambda w: pl.BlockSpec((None, t, w), lambda bi: (bi, 0, 0))
    args = [q, k, v]
    in_specs = [tok(wq), tok(wk), tok(wk)]
    if norm:
        reps = LANES // HEAD_DIM
        args += [jnp.tile(qg, reps).reshape(1, LANES), jnp.tile(kg, reps).reshape(1, LANES), _head_mean_matrix()]
        in_specs += [pl.BlockSpec((1, LANES), lambda bi: (0, 0))] * 2
        in_specs += [pl.BlockSpec((LANES, LANES), lambda bi: (0, 0))]
    return pl.pallas_call(
        functools.partial(_attn_kernel, norm=norm, group=wq // wk, tq=t),
        out_shape=jax.ShapeDtypeStruct((b, t, wq), BF16),
        grid=(b,),
        in_specs=in_specs,
        out_specs=tok(wq),
        compiler_params=_cp("parallel"),
        name="attn",
    )(*args)


def _merge_kernel(hf_ref, hr_ref, gl_ref, yb_ref, yc_ref, gt_ref, x_ref, g1_ref, sh2_ref, sc2_ref, ng1_ref, ng2_ref,
                  woa_ref, wob_ref, woc_ref, wout_ref, x1_ref, h2_ref, h2p_ref):
    d = x_ref.shape[-1]
    ya = (hf_ref[...].astype(F32) + hr_ref[...].astype(F32)) * gl_ref[...].astype(F32)
    pa = jnp.dot(ya.astype(BF16), woa_ref[...], preferred_element_type=F32)
    pb = jnp.dot(yb_ref[...], wob_ref[...], preferred_element_type=F32)
    pc = jnp.dot(yc_ref[...], woc_ref[...], preferred_element_type=F32)
    ga = gt_ref[:, 0:d].astype(F32)
    gb = gt_ref[:, d:2 * d].astype(F32)
    gc = gt_ref[:, 2 * d:3 * d].astype(F32)
    m = ga * pa + gb * pb + gc * pc
    y = jnp.dot(m.astype(BF16), wout_ref[...], preferred_element_type=F32)
    x1 = x_ref[...] + g1_ref[...] * _rms(y, ng1_ref[...])
    x1_ref[...] = x1
    h2 = (_rms(x1, ng2_ref[...]) * (1.0 + sc2_ref[...]) + sh2_ref[...]).astype(BF16)
    h2_ref[...] = h2
    h2p_ref[...] = _pack_halves(h2)


def _pack_halves(xb):
    half = xb.shape[1] // 2
    bits = pltpu.bitcast(xb.astype(F32), jnp.uint32)
    return bits[:, :half] | (bits[:, half:] >> 16)


def _unpack_halves(words):
    hi = pltpu.bitcast(words & jnp.uint32(0xFFFF0000), F32)
    lo = pltpu.bitcast(words << 16, F32)
    return jnp.concatenate([hi, lo], axis=1).astype(BF16)


def _merge(h, gl, yb, yc, gates, x, g1, sh2, sc2, ng1, ng2, woa, wob, woc, wout, mod_row):
    b, t, d = x.shape
    tm = min(t, 256)
    wl, wb, wc = gl.shape[2], yb.shape[2], yc.shape[2]
    if mod_row is None:
        mod_map = lambda bi, i: (bi, 0, 0)
    else:
        mod_map = lambda bi, i: (mod_row, 0, 0)
    tok = lambda w: pl.BlockSpec((None, tm, w), lambda bi, i: (bi, i, 0))
    mod = pl.BlockSpec((None, 1, d), mod_map)
    const = lambda r, c: pl.BlockSpec((r, c), lambda bi, i: (0, 0))
    return pl.pallas_call(
        _merge_kernel,
        out_shape=[jax.ShapeDtypeStruct((b, t, d), F32), jax.ShapeDtypeStruct((b, t, d), BF16),
                   jax.ShapeDtypeStruct((b, t, d // 2), jnp.uint32)],
        grid=(b, t // tm),
        in_specs=[
            pl.BlockSpec((None, None, tm, wl), lambda bi, i: (0, bi, i, 0)),
            pl.BlockSpec((None, None, tm, wl), lambda bi, i: (1, bi, i, 0)),
            tok(wl), tok(wb), tok(wc), tok(N_BRANCH * d), tok(d),
            mod, mod, mod, const(1, d), const(1, d),
            const(wl, d), const(wb, d), const(wc, d), const(d, d),
        ],
        out_specs=[tok(d), tok(d), tok(d // 2)],
        compiler_params=_cp("parallel", "parallel"),
        name="merge",
    )(h, h, gl, yb, yc, gates, x, g1, sh2, sc2, ng1.reshape(1, d), ng2.reshape(1, d), woa, wob, woc, wout)


def _lane_cumsum(x, tri):
    e, n = x.shape
    carry = jnp.zeros((e, 1), F32)
    cols = []
    for c in range(n // LANES):
        part = jnp.dot(x[:, c * LANES:(c + 1) * LANES].astype(BF16), tri, preferred_element_type=F32) + carry
        cols.append(part)
        carry = part[:, LANES - 1:LANES]
    return cols[0] if len(cols) == 1 else jnp.concatenate(cols, axis=-1)


def _route_kernel(h_ref, wr_ref, tri_ref, tok_ref, idx_ref, gcol_ref, pos_scr, aff_scr, *, cap):
    nbs, n_exp = h_ref.shape[0], wr_ref.shape[0]
    affs = []
    for s in range(nbs):
        logits = lax.dot_general(wr_ref[...], h_ref[s], _NT, preferred_element_type=F32)
        z = jnp.exp(logits - jnp.max(logits, axis=0, keepdims=True))
        affs.append(z / jnp.sum(z, axis=0, keepdims=True))
    aff = affs[0] if nbs == 1 else jnp.concatenate(affs, axis=0)
    aff_scr[...] = aff
    bits = pltpu.bitcast(aff, jnp.int32)
    e, n = bits.shape

    def bisect(_, lohi):
        lo, hi = lohi
        mid = lo + ((hi - lo + 1) >> 1)
        cnt = jnp.sum(jnp.where(bits >= mid, 1.0, 0.0), axis=1, keepdims=True)
        ok = cnt >= cap
        return jnp.where(ok, mid, lo), jnp.where(ok, hi, mid - 1)

    one_bits = 0x3F800000
    lo, _ = lax.fori_loop(0, 31, bisect, (jnp.zeros((e, 1), jnp.int32), jnp.full((e, 1), one_bits, jnp.int32)))
    gt = jnp.where(bits > lo, 1.0, 0.0)
    eq = jnp.where(bits == lo, 1.0, 0.0)
    need = cap - jnp.sum(gt, axis=1, keepdims=True)
    tri = tri_ref[...]
    sel = gt + jnp.where(_lane_cumsum(eq, tri) <= need, eq, 0.0)
    pos = _lane_cumsum(sel, tri) - 1.0
    pos_scr[...] = jnp.where(sel > 0.0, pos, -1.0).astype(jnp.int32)

    slot = lax.broadcasted_iota(jnp.int32, (cap, n), 0)

    def compact(r, carry):
        hit = pos_scr[pl.ds(r, 1), :] == slot
        onehot = jnp.where(hit, 1.0, 0.0).astype(BF16)
        digits = lax.dot_general(tok_ref[...], onehot, _NT, preferred_element_type=F32)
        s, ex = r // n_exp, r % n_exp
        idx_ref[s, pl.ds(ex, 1), :] = (digits[0:1] * GRID_W + digits[1:2]).astype(jnp.int32)
        gcol_ref[s, ex] = jnp.sum(jnp.where(hit, aff_scr[pl.ds(r, 1), :], 0.0), axis=1, keepdims=True)
        return carry

    lax.fori_loop(0, e, compact, 0)


def _route(h2, w_router_t, cap):
    b, n, d = h2.shape
    e = w_router_t.shape[0]
    nbs = 4 if b % 4 == 0 else 1
    tri = jnp.asarray(np.triu(np.ones((LANES, LANES), np.float32)), BF16)
    t = np.arange(n)
    tok = np.zeros((SUBLANES, n), np.float32)
    tok[0], tok[1] = t // GRID_W, t % GRID_W
    return pl.pallas_call(
        functools.partial(_route_kernel, cap=cap),
        out_shape=[jax.ShapeDtypeStruct((b, e, cap), jnp.int32), jax.ShapeDtypeStruct((b, e, cap, 1), F32)],
        grid=(b // nbs,),
        in_specs=[
            pl.BlockSpec((nbs, n, d), lambda bi: (bi, 0, 0)),
            pl.BlockSpec((e, d), lambda bi: (0, 0)),
            pl.BlockSpec((LANES, LANES), lambda bi: (0, 0)),
            pl.BlockSpec((SUBLANES, n), lambda bi: (0, 0)),
        ],
        out_specs=[pl.BlockSpec((nbs, e, cap), lambda bi: (bi, 0, 0)),
                   pl.BlockSpec((nbs, e, cap, 1), lambda bi: (bi, 0, 0, 0))],
        scratch_shapes=[pltpu.VMEM((nbs * e, n), jnp.int32), pltpu.VMEM((nbs * e, n), F32)],
        compiler_params=_cp("parallel"),
        name="route",
    )(h2, w_router_t, tri, jnp.asarray(tok, BF16))


def _expert_kernel(idx_smem, gcol_ref, h_ref, wg_ref, wu_ref, wd_ref, x_hbm, g2_ref, ng_ref, o_ref,
                   hs_scr, y_scr, xbuf, sem, *, cap, nbs, n_exp, per_sample_gate):
    e = pl.program_id(1)
    n = h_ref.shape[1]

    def gather(expert, buf):
        for s in range(nbs):
            for j in range(cap):
                t = idx_smem[s, expert, j]
                hs_scr[buf, pl.ds(s * cap + j, 1), :] = h_ref[s, pl.ds(t, 1), :]

    def scatter_add(expert, buf):
        group = 8
        for s in range(nbs):
            for j0 in range(0, cap, group):
                ts = [idx_smem[s, expert, j0 + g] for g in range(group)]
                acc = [o_ref[s, pl.ds(t, 1), :] for t in ts]
                for g, t in enumerate(ts):
                    o_ref[s, pl.ds(t, 1), :] = acc[g] + y_scr[buf, pl.ds(s * cap + j0 + g, 1), :]

    @pl.when(e == 0)
    def _():
        o_ref[...] = jnp.zeros_like(o_ref)
        y_scr[1] = jnp.zeros(y_scr.shape[1:], F32)
        gather(0, 0)

    def step(cur):
        hs = _unpack_halves(hs_scr[cur])
        scatter_add(jnp.maximum(e - 1, 0), 1 - cur)
        gather(jnp.minimum(e + 1, n_exp - 1), 1 - cur)
        gcol = gcol_ref[...].reshape(nbs * cap, 1)
        a = jnp.dot(hs, wg_ref[...], preferred_element_type=F32)
        u = (a * _sigmoid(a)) * jnp.dot(hs, wu_ref[...], preferred_element_type=F32)
        y_scr[cur] = jnp.dot(u.astype(BF16), wd_ref[...], preferred_element_type=F32) * gcol

    for cur in range(2):
        pl.when(e % 2 == cur)(functools.partial(step, cur))

    rc = xbuf.shape[1]
    chunks = [(s, c) for s in range(nbs) for c in range(n // rc)]
    first = pl.program_id(0) * nbs

    def x_copy(k):
        s, c = chunks[k]
        return pltpu.make_async_copy(x_hbm.at[first + s, pl.ds(c * rc, rc), :], xbuf.at[k % 2], sem.at[k % 2])

    @pl.when(e == n_exp - 1)
    def _():
        x_copy(0).start()
        scatter_add(n_exp - 1, (n_exp - 1) % 2)
        for k, (s, c) in enumerate(chunks):
            if k + 1 < len(chunks):
                x_copy(k + 1).start()
            x_copy(k).wait()
            g2 = g2_ref[s if per_sample_gate else 0]
            sl = slice(c * rc, (c + 1) * rc)
            o_ref[s, sl, :] = xbuf[k % 2] + g2 * _rms(o_ref[s, sl, :], ng_ref[...])


def _experts(h2p, idx, gcol, wg, wu, wd, cap, x1, g2, ng, mod_row):
    b, n, dh = h2p.shape
    e, d, f = wg.shape
    nbs = max(1, min(b, 512 // cap))
    rc = min(n, 256)
    sample = lambda bi, ei: (bi, 0, 0)
    if mod_row is None:
        g2_spec = pl.BlockSpec((nbs, 1, d), sample)
    else:
        g2_spec = pl.BlockSpec((1, 1, d), lambda bi, ei: (mod_row, 0, 0))
    once = pl.Buffered(1)
    return pl.pallas_call(
        functools.partial(_expert_kernel, cap=cap, nbs=nbs, n_exp=e, per_sample_gate=mod_row is None),
        out_shape=jax.ShapeDtypeStruct((b, n, d), F32),
        grid=(b // nbs, e),
        in_specs=[
            pl.BlockSpec((nbs, e, cap), sample, memory_space=pltpu.SMEM),
            pl.BlockSpec((nbs, None, cap, 1), lambda bi, ei: (bi, ei, 0, 0)),
            pl.BlockSpec((nbs, n, dh), sample, pipeline_mode=once),
            pl.BlockSpec((None, d, f), lambda bi, ei: (ei, 0, 0)),
            pl.BlockSpec((None, d, f), lambda bi, ei: (ei, 0, 0)),
            pl.BlockSpec((None, f, d), lambda bi, ei: (ei, 0, 0)),
            pl.BlockSpec(memory_space=pl.ANY),
            g2_spec,
            pl.BlockSpec((1, d), lambda bi, ei: (0, 0)),
        ],
        out_specs=pl.BlockSpec((nbs, n, d), sample, pipeline_mode=once),
        scratch_shapes=[pltpu.VMEM((2, nbs * cap, dh), jnp.uint32), pltpu.VMEM((2, nbs * cap, d), F32),
                        pltpu.VMEM((2, rc, d), F32), pltpu.SemaphoreType.DMA((2,))],
        compiler_params=_cp("parallel", "arbitrary"),
        name="experts",
    )(idx, gcol, h2p, wg, wu, wd, x1, g2, ng.reshape(1, d))


def _split_cols(w, widths):
    offs = np.cumsum((0,) + tuple(widths))
    return [w[:, int(offs[i]):int(offs[i + 1])] for i in range(len(widths))]


def _layer(xl, xc, mods, p, need_ctx):
    bsz, s, d = xl.shape
    ctx_row = bsz
    sh1, sc1, g1, sh2, sc2, g2 = mods
    ng = p["norm_g"]
    na_w = (d // 128) * HEAD_DIM
    gq_w = (d // 128) * HEAD_DIM
    gkv_w = GQA_KV_HEADS * HEAD_DIM
    splits = (d, d, na_w, na_w, na_w, gq_w, gkv_w, gkv_w, N_BRANCH * d)
    acts = (None, "gelu", None, None, None, None, None, None, "sigmoid")
    w_in = p["w_in"].astype(BF16)

    lx, lg, nq, nk, nv, gq, gk, gv, gates = _inproj(xl, sh1, sc1, ng[0], w_in, splits, acts, None)
    if need_ctx:
        cx, cg, cnq, cnk, cnv, cgq, cgk, cgv, cgates = _inproj(xc, sh1, sc1, ng[0], w_in, splits, acts, ctx_row)
    else:
        parts = _split_cols(w_in, splits)
        kv_parts = (0, 3, 4, 6, 7)
        w_kv = jnp.concatenate([parts[i] for i in kv_parts], axis=1)
        cx, cnk, cnv, cgk, cgv = _inproj(xc, sh1, sc1, ng[0], w_kv, tuple(splits[i] for i in kv_parts),
                                         tuple(acts[i] for i in kv_parts), ctx_row)

    lp = {
        "conv_w": p["conv_w"], "conv_b": p["conv_b"].reshape(1, d),
        "wab": jnp.concatenate([p["lru_wa"], p["lru_wx"]], axis=-1).astype(BF16),
        "ba": p["lru_ba"].reshape(2, 1, d), "bx": p["lru_bx"].reshape(2, 1, d),
        "lam": p["lru_lam"].reshape(2, 1, d),
    }
    hc, hfin = _lru(cx, jnp.zeros((2, bsz, d), F32), lp)
    hl, _ = _lru(lx, hfin, lp)

    yb = _na(nq, nk, nv, cnk, cnv, p["na_rpb"])
    yc = _gqa(gq, cgk, cgv, gk, gv, p["qn_g"], p["kn_g"])

    woa, wob, woc, wout = (p[k].astype(BF16) for k in ("w_o_a", "w_o_b", "w_o_c", "w_out"))
    wr_t = p["w_router"].T.astype(BF16)
    wg, wu, wd = (p[k].astype(BF16) for k in ("w_gate", "w_up", "w_down"))

    def ffn(x1, h2, h2p, mod_row):
        n = x1.shape[1]
        cap = EC_CAPACITY * n // N_EXPERTS
        idx, gcol = _route(h2, wr_t, cap)
        return _experts(h2p, idx, gcol, wg, wu, wd, cap, x1, g2, ng[3], mod_row)

    xl = ffn(*_merge(hl, lg, yb, yc, gates, xl, g1, sh2, sc2, ng[1], ng[2], woa, wob, woc, wout, None), None)
    if need_ctx:
        ybc = _attn(cnq, cnk, cnv)
        ycc = _attn(cgq, cgk, cgv, qg=p["qn_g"], kg=p["kn_g"])
        xc = ffn(*_merge(hc, cg, ybc, ycc, cgates, xc, g1, sh2, sc2, ng[1], ng[2], woa, wob, woc, wout, ctx_row),
                 ctx_row)
    return xl, xc


def kernel(x, c, ctx, c_ctx, w_mod, b_mod, norm_g, w_in, conv_w, conv_b, lru_wa, lru_ba, lru_wx, lru_bx, lru_lam,
           na_rpb, qn_g, kn_g, w_o_a, w_o_b, w_o_c, w_out, w_router, w_gate, w_up, w_down):
    bsz, _, d = x.shape
    depth = w_mod.shape[0]
    rows = -(-(bsz + 1) // SUBLANES) * SUBLANES
    cond = jnp.zeros((rows, d), F32).at[:bsz].set(c).at[bsz].set(c_ctx)
    xl, xc = x, ctx
    for l in range(depth):
        p = {
            "norm_g": norm_g[l], "w_in": w_in[l], "conv_w": conv_w[l], "conv_b": conv_b[l],
            "lru_wa": lru_wa[l], "lru_ba": lru_ba[l], "lru_wx": lru_wx[l], "lru_bx": lru_bx[l],
            "lru_lam": lru_lam[l], "na_rpb": na_rpb[l], "qn_g": qn_g[l], "kn_g": kn_g[l],
            "w_o_a": w_o_a[l], "w_o_b": w_o_b[l], "w_o_c": w_o_c[l], "w_out": w_out[l],
            "w_router": w_router[l], "w_gate": w_gate[l], "w_up": w_up[l], "w_down": w_down[l],
        }
        m = _adaln(cond, w_mod[l], b_mod[l])
        mods = [m[:, i * d:(i + 1) * d].reshape(rows, 1, d) for i in range(6)]
        xl, xc = _layer(xl, xc, mods, p, l < depth - 1)
    return xl
```

```python
import functools

import jax
import jax.numpy as jnp
import numpy as np
from jax import lax
from jax.experimental import pallas as pl
from jax.experimental.pallas import tpu as pltpu

F32 = jnp.float32
BF16 = jnp.bfloat16

GRID_W = 64
HEAD_DIM = 64
LRU_BLOCKS = 8
CONV_W = 4
LRU_C = 8.0
NA_KH_MAX = 8
NA_KW = 16
GQA_KV_HEADS = 2
ROPE_BASE = 10000.0
N_EXPERTS = 16
EC_CAPACITY = 2
N_BRANCH = 3
EPS = 1e-6
NEG_INF = -1e30

LANES = 128
SUBLANES = 8
BF16_ROWS = 16
VMEM_LIMIT_BYTES = 56 * 1024 * 1024

_NT = (((1,), (1,)), ((), ()))


def _cp(*sem):
    return pltpu.CompilerParams(dimension_semantics=sem, vmem_limit_bytes=VMEM_LIMIT_BYTES)


def _sigmoid(x):
    return 0.5 * jnp.tanh(0.5 * x) + 0.5


def _rms(x, g):
    return x * lax.rsqrt(jnp.mean(x * x, axis=-1, keepdims=True) + EPS) * g


def _adaln_kernel(c_ref, w_ref, b_ref, o_ref):
    c = c_ref[...]
    s = (c * _sigmoid(c)).astype(BF16)
    o_ref[...] = jnp.dot(s, w_ref[...].astype(BF16), preferred_element_type=F32) + b_ref[...]


def _adaln(cond, w_mod, b_mod):
    r, d = cond.shape
    n = w_mod.shape[1]
    tn = n // 4
    return pl.pallas_call(
        _adaln_kernel,
        out_shape=jax.ShapeDtypeStruct((r, n), F32),
        grid=(n // tn,),
        in_specs=[
            pl.BlockSpec((r, d), lambda j: (0, 0)),
            pl.BlockSpec((d, tn), lambda j: (0, j)),
            pl.BlockSpec((1, tn), lambda j: (0, j)),
        ],
        out_specs=pl.BlockSpec((r, tn), lambda j: (0, j)),
        compiler_params=_cp("parallel"),
        name="adaln",
    )(cond, w_mod, b_mod.reshape(1, n))


_ACTIVATIONS = {None: lambda v: v, "gelu": jax.nn.gelu, "sigmoid": _sigmoid}


def _inproj_kernel(x_ref, sh_ref, sc_ref, g_ref, w_ref, *o_refs, splits, acts):
    h = (_rms(x_ref[...], g_ref[...]) * (1.0 + sc_ref[...]) + sh_ref[...]).astype(BF16)
    off = 0
    for o_ref, width, act in zip(o_refs, splits, acts):
        y = _ACTIVATIONS[act](jnp.dot(h, w_ref[:, off:off + width], preferred_element_type=F32))
        o_ref[...] = y.astype(o_ref.dtype)
        off += width


def _inproj(x, shift, scale, gain, w, splits, acts, mod_row):
    b, t, d = x.shape
    tm = min(t, 512)
    n = w.shape[1]
    if mod_row is None:
        mod_map = lambda bi, i: (bi, 0, 0)
    else:
        mod_map = lambda bi, i: (mod_row, 0, 0)
    return pl.pallas_call(
        functools.partial(_inproj_kernel, splits=splits, acts=acts),
        out_shape=[jax.ShapeDtypeStruct((b, t, s), BF16) for s in splits],
        grid=(b, t // tm),
        in_specs=[
            pl.BlockSpec((None, tm, d), lambda bi, i: (bi, i, 0)),
            pl.BlockSpec((None, 1, d), mod_map),
            pl.BlockSpec((None, 1, d), mod_map),
            pl.BlockSpec((1, d), lambda bi, i: (0, 0)),
            pl.BlockSpec((d, n), lambda bi, i: (0, 0)),
        ],
        out_specs=[pl.BlockSpec((None, tm, s), lambda bi, i: (bi, i, 0)) for s in splits],
        compiler_params=_cp("parallel", "parallel"),
        name="inproj",
    )(x, shift, scale, gain.reshape(1, d), w)


def _lru_kernel(x_ref, xp_ref, xn_ref, h0_ref, smat_ref, cw_ref, cb_ref, wab_ref, ba_ref, bx_ref, lam_ref,
                y_ref, hfin_ref, u_scr, a_scr, b_scr, h_scr, st_scr, *, tc, nchunks, nb):
    d = pl.program_id(1)
    j = pl.program_id(2)
    c = jnp.where(d == 0, j, nchunks - 1 - j)
    w = x_ref.shape[-1]
    cwide = 2 * LANES

    @pl.when(j == 0)
    def _():
        for cb in range(LRU_BLOCKS):
            h0 = h0_ref[:, cb * LANES:(cb + 1) * LANES]
            st_scr[cb] = jnp.broadcast_to(h0[:, None, :], (nb, SUBLANES, LANES))

    zero = jnp.zeros((), BF16)
    c_lam = -LRU_C * jax.nn.softplus(-lam_ref[...])
    for cc in range(w // cwide):
        sl2 = slice(cc * cwide, (cc + 1) * cwide)
        for bb in range(nb):
            xe = jnp.concatenate([jnp.where(c > 0, xp_ref[bb, :, sl2], zero), x_ref[bb, :, sl2],
                                  jnp.where(c < nchunks - 1, xn_ref[bb, :, sl2], zero)], axis=0)
            sh = jnp.dot(smat_ref[...], xe, preferred_element_type=F32)
            u = cb_ref[:, sl2]
            for k in range(CONV_W):
                u = u + sh[k * tc:(k + 1) * tc] * cw_ref[k:k + 1, sl2]
            u_scr[bb, :, sl2] = u
        for cb in range(cc * (cwide // LANES), (cc + 1) * (cwide // LANES)):
            sl = slice(cb * LANES, (cb + 1) * LANES)
            u = u_scr[:, :, sl].reshape(nb * tc, LANES)
            gates = jnp.dot(u.astype(BF16), wab_ref[cb], preferred_element_type=F32)
            r = _sigmoid(gates[:, :LANES] + ba_ref[:, sl])
            i = _sigmoid(gates[:, LANES:] + bx_ref[:, sl])
            log_a = c_lam[:, sl] * r
            a = jnp.exp(log_a)
            a_scr[cb] = a
            m = -jnp.tanh(log_a) * (a * a + 1.0)
            b_scr[cb] = jnp.where(m > 0.0, m * lax.rsqrt(m), 0.0) * (i * u)

    @pl.when(d == 0)
    def _():
        _block_scan(a_scr, b_scr, h_scr, st_scr, False, nb, tc)

    @pl.when(d == 1)
    def _():
        _block_scan(a_scr, b_scr, h_scr, st_scr, True, nb, tc)

    for cb in range(LRU_BLOCKS):
        y_ref[:, :, cb * LANES:(cb + 1) * LANES] = h_scr[cb].astype(y_ref.dtype)

    @pl.when(j == nchunks - 1)
    def _():
        for cb in range(LRU_BLOCKS):
            hfin_ref[:, cb * LANES:(cb + 1) * LANES] = st_scr[cb][:, 0, :]


def _block_scan(a_scr, b_scr, h_scr, st_scr, rev, nb, tc):
    nblk = tc // SUBLANES
    row = lax.broadcasted_iota(jnp.int32, (1, SUBLANES, LANES), 1)
    for cb in range(LRU_BLOCKS):
        a = a_scr[cb].reshape(nb * nblk, SUBLANES, LANES)
        b = b_scr[cb].reshape(nb * nblk, SUBLANES, LANES)
        for s in (1, 2, 4):
            shift, keep = (SUBLANES - s, row < SUBLANES - s) if rev else (s, row >= s)
            b = b + a * jnp.where(keep, pltpu.roll(b, shift, 1), 0.0)
            a = a * jnp.where(keep, pltpu.roll(a, shift, 1), 1.0)
        a = a.reshape(nb, nblk, SUBLANES, LANES)
        b = b.reshape(nb, nblk, SUBLANES, LANES)
        carry = st_scr[cb]
        last = 0 if rev else SUBLANES - 1
        for k in (reversed(range(nblk)) if rev else range(nblk)):
            h = a[:, k] * carry + b[:, k]
            h_scr[cb, :, k * SUBLANES:(k + 1) * SUBLANES, :] = h
            carry = jnp.broadcast_to(h[:, last:last + 1, :], h.shape)
        st_scr[cb] = carry


def _lru(x, h0, lp):
    b, t, w = x.shape
    nb = SUBLANES
    tc = min(t, 128)
    nchunks = t // tc
    hb = tc // BF16_ROWS
    nhb = t // BF16_ROWS

    def chunk(di, j):
        return jnp.where(di == 0, j, nchunks - 1 - j)

    smat = np.zeros((CONV_W * tc, tc + 2 * BF16_ROWS), np.float32)
    for k in range(CONV_W):
        smat[k * tc + np.arange(tc), BF16_ROWS + np.arange(tc) + k - CONV_W // 2] = 1.0

    vec = lambda g, di, j: (0, 0)
    dvec = lambda g, di, j: (di, 0, 0)
    return pl.pallas_call(
        functools.partial(_lru_kernel, tc=tc, nchunks=nchunks, nb=nb),
        out_shape=[jax.ShapeDtypeStruct((2, b, t, w), BF16), jax.ShapeDtypeStruct((2, b, w), F32)],
        grid=(b // nb, 2, nchunks),
        in_specs=[
            pl.BlockSpec((nb, tc, w), lambda g, di, j: (g, chunk(di, j), 0)),
            pl.BlockSpec((nb, BF16_ROWS, w), lambda g, di, j: (g, jnp.maximum(chunk(di, j) * hb - 1, 0), 0)),
            pl.BlockSpec((nb, BF16_ROWS, w), lambda g, di, j: (g, jnp.minimum((chunk(di, j) + 1) * hb, nhb - 1), 0)),
            pl.BlockSpec((None, nb, w), lambda g, di, j: (di, g, 0)),
            pl.BlockSpec(smat.shape, vec),
            pl.BlockSpec((CONV_W, w), vec),
            pl.BlockSpec((1, w), vec),
            pl.BlockSpec((None, LRU_BLOCKS, LANES, 2 * LANES), lambda g, di, j: (di, 0, 0, 0)),
            pl.BlockSpec((None, 1, w), dvec),
            pl.BlockSpec((None, 1, w), dvec),
            pl.BlockSpec((None, 1, w), dvec),
        ],
        out_specs=[
            pl.BlockSpec((None, nb, tc, w), lambda g, di, j: (di, g, chunk(di, j), 0)),
            pl.BlockSpec((None, nb, w), lambda g, di, j: (di, g, 0)),
        ],
        scratch_shapes=[
            pltpu.VMEM((nb, tc, w), F32),
            pltpu.VMEM((LRU_BLOCKS, nb * tc, LANES), F32),
            pltpu.VMEM((LRU_BLOCKS, nb * tc, LANES), F32),
            pltpu.VMEM((LRU_BLOCKS, nb, tc, LANES), F32),
            pltpu.VMEM((LRU_BLOCKS, nb, SUBLANES, LANES), F32),
        ],
        compiler_params=_cp("parallel", "arbitrary", "arbitrary"),
        name="rglru",
    )(x, x, x, h0, jnp.asarray(smat, BF16), lp["conv_w"], lp["conv_b"], lp["wab"], lp["ba"], lp["bx"], lp["lam"])


def _pair_rows(x2):
    low = lax.broadcasted_iota(jnp.int32, x2.shape, 1) < HEAD_DIM
    zero = jnp.zeros((), x2.dtype)
    return jnp.concatenate([jnp.where(low, x2, zero), jnp.where(low, zero, x2)], axis=0)


def _softmax_strips(s_ref, p_ref):
    sums = []
    for r in range(s_ref.shape[0] // BF16_ROWS):
        rows = slice(r * BF16_ROWS, (r + 1) * BF16_ROWS)
        s = s_ref[rows, :]
        p = jnp.exp(s - jnp.max(s, axis=-1, keepdims=True))
        sums.append(jnp.sum(p, axis=-1, keepdims=True))
        p_ref[rows, :] = p.astype(BF16)
    return sums


def _normalise_strips(o, sums):
    return jnp.concatenate([o[r * BF16_ROWS:(r + 1) * BF16_ROWS] / l for r, l in enumerate(sums)], axis=0)


def _na_kernel(var_ref, q_ref, k_ref, v_ref, kc_ref, vc_ref, bias_ref, o_ref, s_scr, p_scr,
               *, band, win_rows, kh, rows):
    del var_ref
    r0 = pl.program_id(1) * band
    ws = jnp.clip(r0 - kh // 2, 0, rows - win_rows)
    start = pl.multiple_of(ws * GRID_W, GRID_W)
    nwin = win_rows * GRID_W
    m = band * GRID_W
    npairs = q_ref.shape[-1] // LANES
    low = lax.broadcasted_iota(jnp.int32, (m, LANES), 1) < HEAD_DIM

    def scores(hp):
        sl = slice(hp * LANES, (hp + 1) * LANES)
        qs = _pair_rows(q_ref[:, sl] * jnp.asarray(HEAD_DIM ** -0.5, BF16))
        s_lat = lax.dot_general(qs, k_ref[pl.ds(start, nwin), sl], _NT, preferred_element_type=F32)
        s_scr[hp % 2, :, 0:nwin] = s_lat + bias_ref[2 * hp:2 * hp + 2].reshape(2 * m, nwin)
        s_scr[hp % 2, :, nwin:] = lax.dot_general(qs, kc_ref[:, sl], _NT, preferred_element_type=F32)

    def finish(hp):
        sl = slice(hp * LANES, (hp + 1) * LANES)
        sums = _softmax_strips(s_scr.at[hp % 2], p_scr.at[hp % 2])
        o = jnp.dot(p_scr[hp % 2, :, 0:nwin], v_ref[pl.ds(start, nwin), sl], preferred_element_type=F32)
        o = o + jnp.dot(p_scr[hp % 2, :, nwin:], vc_ref[:, sl], preferred_element_type=F32)
        o = _normalise_strips(o, sums)
        o_ref[:, sl] = jnp.where(low, o[:m], o[m:]).astype(o_ref.dtype)

    scores(0)
    for hp in range(npairs):
        if hp + 1 < npairs:
            scores(hp + 1)
        finish(hp)


def _na_bands(rows, band, win_rows):
    kh = min(NA_KH_MAX, rows)
    variants, var_of_band = [], []
    for r0 in range(0, rows, band):
        ws = int(np.clip(r0 - kh // 2, 0, rows - win_rows))
        key = (r0 - ws, tuple(int(np.clip(r0 + rr - kh // 2, 0, rows - kh)) - ws for rr in range(band)))
        if key not in variants:
            variants.append(key)
        var_of_band.append(variants.index(key))
    return variants, np.asarray(var_of_band, np.int32)


def _na_bias_table(rpb, rows, band, win_rows):
    kh = min(NA_KH_MAX, rows)
    col = np.arange(GRID_W)
    cs = np.clip(col - NA_KW // 2, 0, GRID_W - NA_KW)
    col_mask = (col[None, :] >= cs[:, None]) & (col[None, :] < cs[:, None] + NA_KW)
    dcol = np.clip(col[None, :] - col[:, None], -(NA_KW - 1), NA_KW - 1) + NA_KW - 1
    heads, n_drow, n_dcol = rpb.shape
    onehot = jnp.asarray(dcol.reshape(-1)[None, :] == np.arange(n_dcol)[:, None], F32)
    t2 = jnp.dot(rpb.reshape(heads * n_drow, n_dcol).astype(F32), onehot, precision=lax.Precision.HIGHEST)
    t2 = t2.reshape(heads, n_drow, GRID_W, GRID_W)
    t2 = jnp.where(col_mask[None, None], t2, NEG_INF)
    masked = jnp.full((heads, GRID_W, GRID_W), NEG_INF, F32)
    variants, _ = _na_bands(rows, band, win_rows)
    tabs = []
    for delta, first in variants:
        q_rows = []
        for rr in range(band):
            blocks = [t2[:, j - delta - rr + NA_KH_MAX - 1] if first[rr] <= j < first[rr] + kh else masked
                      for j in range(win_rows)]
            q_rows.append(jnp.concatenate(blocks, axis=-1))
        tabs.append(jnp.concatenate(q_rows, axis=1))
    return jnp.stack(tabs)


def _na(q, k, v, kc, vc, rpb):
    b, s, w = q.shape
    l = kc.shape[1]
    heads = w // HEAD_DIM
    rows = s // GRID_W
    kh = min(NA_KH_MAX, rows)
    band = min(4, rows)
    win_rows = min(-(-(kh + band - 1) // 2) * 2, rows)
    m, nwin = band * GRID_W, win_rows * GRID_W
    bias = _na_bias_table(rpb, rows, band, win_rows)
    _, var_of_band = _na_bands(rows, band, win_rows)

    full = lambda bi, r, var: (bi, 0, 0)
    grid_spec = pltpu.PrefetchScalarGridSpec(
        num_scalar_prefetch=1,
        grid=(b, rows // band),
        in_specs=[
            pl.BlockSpec((None, m, w), lambda bi, r, var: (bi, r, 0)),
            pl.BlockSpec((None, s, w), full),
            pl.BlockSpec((None, s, w), full),
            pl.BlockSpec((None, l, w), full),
            pl.BlockSpec((None, l, w), full),
            pl.BlockSpec((None, heads, m, nwin), lambda bi, r, var: (var[r], 0, 0, 0)),
        ],
        out_specs=pl.BlockSpec((None, m, w), lambda bi, r, var: (bi, r, 0)),
        scratch_shapes=[pltpu.VMEM((2, 2 * m, nwin + l), F32), pltpu.VMEM((2, 2 * m, nwin + l), BF16)],
    )
    return pl.pallas_call(
        functools.partial(_na_kernel, band=band, win_rows=win_rows, kh=kh, rows=rows),
        out_shape=jax.ShapeDtypeStruct((b, s, w), BF16),
        grid_spec=grid_spec,
        compiler_params=_cp("parallel", "arbitrary"),
        name="natten",
    )(jnp.asarray(var_of_band), q, k, v, kc, vc, bias)


def _head_norm(x, g, gmat):
    cols = []
    for c in range(x.shape[1] // LANES):
        xc = x[:, c * LANES:(c + 1) * LANES]
        x2 = xc * xc
        hi = x2.astype(BF16)
        lo = (x2 - hi.astype(F32)).astype(BF16)
        ms = jnp.dot(hi, gmat, preferred_element_type=F32) + jnp.dot(lo, gmat, preferred_element_type=F32)
        cols.append(xc * lax.rsqrt(ms + EPS) * g)
    return cols[0] if len(cols) == 1 else jnp.concatenate(cols, axis=-1)


def _rope(x, cos, sin):
    half = HEAD_DIM // 2
    lane = lax.broadcasted_iota(jnp.int32, (x.shape[0], LANES), 1)
    low = (lane % HEAD_DIM) < half
    cols = []
    for c in range(x.shape[1] // LANES):
        xc = x[:, c * LANES:(c + 1) * LANES]
        sw = jnp.where(low, pltpu.roll(xc, LANES - half, 1), pltpu.roll(xc, half, 1))
        cols.append(xc * cos + sw * sin)
    return cols[0] if len(cols) == 1 else jnp.concatenate(cols, axis=-1)


def _attn_kernel(*refs, norm, group, tq):
    if norm:
        q_ref, k_ref, v_ref, qg_ref, kg_ref, gmat_ref, o_ref = refs
    else:
        q_ref, k_ref, v_ref, o_ref = refs
    q = q_ref[...].astype(F32)
    k = k_ref[...]
    if norm:
        q = _head_norm(q, qg_ref[...], gmat_ref[...])
        k = _head_norm(k.astype(F32), kg_ref[...], gmat_ref[...]).astype(BF16)
    q = (q * (HEAD_DIM ** -0.5)).astype(BF16)

    outs = []
    for g in range(k_ref.shape[1] // HEAD_DIM):
        sl = slice(g * HEAD_DIM, (g + 1) * HEAD_DIM)
        qs = [q[:, (g * group + jj) * HEAD_DIM:(g * group + jj + 1) * HEAD_DIM] for jj in range(group)]
        qs = qs[0] if group == 1 else jnp.concatenate(qs, axis=0)
        s = lax.dot_general(qs, k[:, sl], _NT, preferred_element_type=F32)
        p = jnp.exp(s - jnp.max(s, axis=-1, keepdims=True))
        o = jnp.dot(p.astype(BF16), v_ref[:, sl], preferred_element_type=F32) / jnp.sum(p, axis=-1, keepdims=True)
        for jj in range(group):
            outs.append(o[jj * tq:(jj + 1) * tq])
    o_ref[...] = jnp.concatenate(outs, axis=-1).astype(o_ref.dtype)


def _dup_heads(x):
    low = lax.broadcasted_iota(jnp.int32, x.shape, 1) < HEAD_DIM
    sw = pltpu.roll(x, HEAD_DIM, 1)
    return jnp.where(low, x, sw), jnp.where(low, sw, x)


def _gqa_kernel(q_ref, kc_ref, vc_ref, kl_ref, vl_ref, qg_ref, kg_ref, gmat_ref, cosq_ref, sinq_ref, cosk_ref,
                sink_ref, o_ref, kc_scr, kl_scr, vc_scr, vl_scr, s_scr, p_scr, *, sub):
    n_ctx = kc_ref.shape[0]
    n_kv = GQA_KV_HEADS

    @pl.when(pl.program_id(1) == 0)
    def _():
        gmat = gmat_ref[...]
        kc = _head_norm(kc_ref[...].astype(F32), kg_ref[...], gmat)
        kl = _rope(_head_norm(kl_ref[...].astype(F32), kg_ref[...], gmat), cosk_ref[...], sink_ref[...])
        for scr, val in ((kc_scr, kc), (kl_scr, kl), (vc_scr, vc_ref[...].astype(F32)),
                         (vl_scr, vl_ref[...].astype(F32))):
            for g, dup in enumerate(_dup_heads(val)):
                scr[g] = dup.astype(BF16)

    tq = q_ref.shape[0]
    chunks = []
    for c in range(q_ref.shape[1] // LANES):
        qc = q_ref[:, c * LANES:(c + 1) * LANES].astype(F32)
        qc = _rope(_head_norm(qc, qg_ref[...], gmat_ref[...]), cosq_ref[...], sinq_ref[...])
        chunks.append((qc * (HEAD_DIM ** -0.5)).astype(BF16))
    pairs_per_group = len(chunks) // n_kv
    low = lax.broadcasted_iota(jnp.int32, (sub, LANES), 1) < HEAD_DIM
    units = [(t, g) for t in range(tq // sub) for g in range(n_kv)]

    def scores(u):
        t, g = units[u]
        rows = [_pair_rows(chunks[g * pairs_per_group + j][t * sub:(t + 1) * sub]) for j in range(pairs_per_group)]
        qs = jnp.concatenate(rows, axis=0)
        s_scr[u % 2, :, 0:n_ctx] = lax.dot_general(qs, kc_scr[g], _NT, preferred_element_type=F32)
        s_scr[u % 2, :, n_ctx:] = lax.dot_general(qs, kl_scr[g], _NT, preferred_element_type=F32)

    def finish(u):
        t, g = units[u]
        sums = _softmax_strips(s_scr.at[u % 2], p_scr.at[u % 2])
        o = jnp.dot(p_scr[u % 2, :, 0:n_ctx], vc_scr[g], preferred_element_type=F32)
        o = o + jnp.dot(p_scr[u % 2, :, n_ctx:], vl_scr[g], preferred_element_type=F32)
        o = _normalise_strips(o, sums)
        for j in range(pairs_per_group):
            c = g * pairs_per_group + j
            lo, hi = o[2 * j * sub:(2 * j + 1) * sub], o[(2 * j + 1) * sub:(2 * j + 2) * sub]
            o_ref[t * sub:(t + 1) * sub, c * LANES:(c + 1) * LANES] = jnp.where(low, lo, hi).astype(o_ref.dtype)

    scores(0)
    for u in range(len(units)):
        if u + 1 < len(units):
            scores(u + 1)
        finish(u)


def _gqa(q, kc, vc, kl, vl, qg, kg):
    b, t, wq = q.shape
    l, wk = kc.shape[1], kc.shape[2]
    s = kl.shape[1]
    assert wk == GQA_KV_HEADS * HEAD_DIM == LANES
    tq = min(t, 512)
    sub = min(tq, 256)
    m_unit = 2 * (wq // LANES // GQA_KV_HEADS) * sub
    reps = LANES // HEAD_DIM
    gmat = _head_mean_matrix()
    cos, sin = _rope_tables(t)
    full = lambda bi, i: (bi, 0, 0)
    const = lambda bi, i: (0, 0)
    return pl.pallas_call(
        functools.partial(_gqa_kernel, sub=sub),
        out_shape=jax.ShapeDtypeStruct((b, t, wq), BF16),
        grid=(b, t // tq),
        in_specs=[
            pl.BlockSpec((None, tq, wq), lambda bi, i: (bi, i, 0)),
            pl.BlockSpec((None, l, wk), full), pl.BlockSpec((None, l, wk), full),
            pl.BlockSpec((None, s, wk), full), pl.BlockSpec((None, s, wk), full),
            pl.BlockSpec((1, LANES), const), pl.BlockSpec((1, LANES), const), pl.BlockSpec((LANES, LANES), const),
            pl.BlockSpec((tq, LANES), lambda bi, i: (i, 0)), pl.BlockSpec((tq, LANES), lambda bi, i: (i, 0)),
            pl.BlockSpec((s, LANES), const), pl.BlockSpec((s, LANES), const),
        ],
        out_specs=pl.BlockSpec((None, tq, wq), lambda bi, i: (bi, i, 0)),
        scratch_shapes=[
            pltpu.VMEM((GQA_KV_HEADS, l, LANES), BF16), pltpu.VMEM((GQA_KV_HEADS, s, LANES), BF16),
            pltpu.VMEM((GQA_KV_HEADS, l, LANES), BF16), pltpu.VMEM((GQA_KV_HEADS, s, LANES), BF16),
            pltpu.VMEM((2, m_unit, l + s), F32), pltpu.VMEM((2, m_unit, l + s), BF16),
        ],
        compiler_params=_cp("parallel", "arbitrary"),
        name="gqa",
    )(q, kc, vc, kl, vl, jnp.tile(qg, reps).reshape(1, LANES), jnp.tile(kg, reps).reshape(1, LANES), gmat,
      cos, sin, cos, sin)


def _rope_tables(n_tokens):
    t = np.arange(n_tokens)
    pos = np.stack([t // GRID_W, t % GRID_W], axis=-1).astype(np.float32)
    n_freq = HEAD_DIM // 4
    inv_freq = jnp.asarray(ROPE_BASE, F32) ** (-jnp.arange(n_freq, dtype=F32) / n_freq)
    ang = (jnp.asarray(pos)[:, :, None] * inv_freq).reshape(n_tokens, 2 * n_freq)
    cos, sin = jnp.cos(ang), jnp.sin(ang)
    reps = LANES // HEAD_DIM
    return jnp.tile(jnp.concatenate([cos, cos], -1), (1, reps)), jnp.tile(jnp.concatenate([-sin, sin], -1), (1, reps))


def _head_mean_matrix():
    blk = np.arange(LANES) // HEAD_DIM
    return jnp.asarray((blk[:, None] == blk[None, :]).astype(np.float32) / HEAD_DIM, BF16)


def _attn(q, k, v, qg=None, kg=None):
    b, t, wq = q.shape
    wk = k.shape[2]
    norm = qg is not None
    tok = lambda w: pl.BlockSpec((None, t, w), lambda bi: (bi, 0, 0))
    args = [q, k, v]
    in_specs = [tok(wq), tok(wk), tok(wk)]
    if norm:
        reps = LANES // HEAD_DIM
        args += [jnp.tile(qg, reps).reshape(1, LANES), jnp.tile(kg, reps).reshape(1, LANES), _head_mean_matrix()]
        in_specs += [pl.BlockSpec((1, LANES), lambda bi: (0, 0))] * 2
        in_specs += [pl.BlockSpec((LANES, LANES), lambda bi: (0, 0))]
    return pl.pallas_call(
        functools.partial(_attn_kernel, norm=norm, group=wq // wk, tq=t),
        out_shape=jax.ShapeDtypeStruct((b, t, wq), BF16),
        grid=(b,),
        in_specs=in_specs,
        out_specs=tok(wq),
        compiler_params=_cp("parallel"),
        name="attn",
    )(*args)


def _merge_kernel(hf_ref, hr_ref, gl_ref, yb_ref, yc_ref, gt_ref, x_ref, g1_ref, sh2_ref, sc2_ref, ng1_ref, ng2_ref,
                  woa_ref, wob_ref, woc_ref, wout_ref, x1_ref, h2_ref, h2p_ref):
    d = x_ref.shape[-1]
    ya = (hf_ref[...].astype(F32) + hr_ref[...].astype(F32)) * gl_ref[...].astype(F32)
    pa = jnp.dot(ya.astype(BF16), woa_ref[...], preferred_element_type=F32)
    pb = jnp.dot(yb_ref[...], wob_ref[...], preferred_element_type=F32)
    pc = jnp.dot(yc_ref[...], woc_ref[...], preferred_element_type=F32)
    ga = gt_ref[:, 0:d].astype(F32)
    gb = gt_ref[:, d:2 * d].astype(F32)
    gc = gt_ref[:, 2 * d:3 * d].astype(F32)
    m = ga * pa + gb * pb + gc * pc
    y = jnp.dot(m.astype(BF16), wout_ref[...], preferred_element_type=F32)
    x1 = x_ref[...] + g1_ref[...] * _rms(y, ng1_ref[...])
    x1_ref[...] = x1
    h2 = (_rms(x1, ng2_ref[...]) * (1.0 + sc2_ref[...]) + sh2_ref[...]).astype(BF16)
    h2_ref[...] = h2
    h2p_ref[...] = _pack_halves(h2)


def _pack_halves(xb):
    half = xb.shape[1] // 2
    bits = pltpu.bitcast(xb.astype(F32), jnp.uint32)
    return bits[:, :half] | (bits[:, half:] >> 16)


def _unpack_halves(words):
    hi = pltpu.bitcast(words & jnp.uint32(0xFFFF0000), F32)
    lo = pltpu.bitcast(words << 16, F32)
    return jnp.concatenate([hi, lo], axis=1).astype(BF16)


def _merge(h, gl, yb, yc, gates, x, g1, sh2, sc2, ng1, ng2, woa, wob, woc, wout, mod_row):
    b, t, d = x.shape
    tm = min(t, 512)
    wl, wb, wc = gl.shape[2], yb.shape[2], yc.shape[2]
    if mod_row is None:
        mod_map = lambda bi, i: (bi, 0, 0)
    else:
        mod_map = lambda bi, i: (mod_row, 0, 0)
    tok = lambda w: pl.BlockSpec((None, tm, w), lambda bi, i: (bi, i, 0))
    mod = pl.BlockSpec((None, 1, d), mod_map)
    const = lambda r, c: pl.BlockSpec((r, c), lambda bi, i: (0, 0))
    return pl.pallas_call(
        _merge_kernel,
        out_shape=[jax.ShapeDtypeStruct((b, t, d), F32), jax.ShapeDtypeStruct((b, t, d), BF16),
                   jax.ShapeDtypeStruct((b, t, d // 2), jnp.uint32)],
        grid=(b, t // tm),
        in_specs=[
            pl.BlockSpec((None, None, tm, wl), lambda bi, i: (0, bi, i, 0)),
            pl.BlockSpec((None, None, tm, wl), lambda bi, i: (1, bi, i, 0)),
            tok(wl), tok(wb), tok(wc), tok(N_BRANCH * d), tok(d),
            mod, mod, mod, const(1, d), const(1, d),
            const(wl, d), const(wb, d), const(wc, d), const(d, d),
        ],
        out_specs=[tok(d), tok(d), tok(d // 2)],
        compiler_params=_cp("parallel", "parallel"),
        name="merge",
    )(h, h, gl, yb, yc, gates, x, g1, sh2, sc2, ng1.reshape(1, d), ng2.reshape(1, d), woa, wob, woc, wout)


def _lane_cumsum(x, tri):
    e, n = x.shape
    carry = jnp.zeros((e, 1), F32)
    cols = []
    for c in range(n // LANES):
        part = jnp.dot(x[:, c * LANES:(c + 1) * LANES].astype(BF16), tri, preferred_element_type=F32) + carry
        cols.append(part)
        carry = part[:, LANES - 1:LANES]
    return cols[0] if len(cols) == 1 else jnp.concatenate(cols, axis=-1)


def _route_kernel(h_ref, wr_ref, tri_ref, tok_ref, idx_ref, gcol_ref, pos_scr, aff_scr, *, cap):
    nbs, n_exp = h_ref.shape[0], wr_ref.shape[0]
    affs = []
    for s in range(nbs):
        logits = lax.dot_general(wr_ref[...], h_ref[s], _NT, preferred_element_type=F32)
        z = jnp.exp(logits - jnp.max(logits, axis=0, keepdims=True))
        affs.append(z / jnp.sum(z, axis=0, keepdims=True))
    aff = affs[0] if nbs == 1 else jnp.concatenate(affs, axis=0)
    aff_scr[...] = aff
    bits = pltpu.bitcast(aff, jnp.int32)
    e, n = bits.shape

    def bisect(_, lohi):
        lo, hi = lohi
        mid = lo + ((hi - lo + 1) >> 1)
        cnt = jnp.sum(jnp.where(bits >= mid, 1.0, 0.0), axis=1, keepdims=True)
        ok = cnt >= cap
        return jnp.where(ok, mid, lo), jnp.where(ok, hi, mid - 1)

    one_bits = 0x3F800000
    lo, _ = lax.fori_loop(0, 31, bisect, (jnp.zeros((e, 1), jnp.int32), jnp.full((e, 1), one_bits, jnp.int32)))
    gt = jnp.where(bits > lo, 1.0, 0.0)
    eq = jnp.where(bits == lo, 1.0, 0.0)
    need = cap - jnp.sum(gt, axis=1, keepdims=True)
    tri = tri_ref[...]
    sel = gt + jnp.where(_lane_cumsum(eq, tri) <= need, eq, 0.0)
    pos = _lane_cumsum(sel, tri) - 1.0
    pos_scr[...] = jnp.where(sel > 0.0, pos, -1.0).astype(jnp.int32)

    slot = lax.broadcasted_iota(jnp.int32, (cap, n), 0)

    def compact(r, carry):
        hit = pos_scr[pl.ds(r, 1), :] == slot
        onehot = jnp.where(hit, 1.0, 0.0).astype(BF16)
        digits = lax.dot_general(tok_ref[...], onehot, _NT, preferred_element_type=F32)
        s, ex = r // n_exp, r % n_exp
        idx_ref[s, pl.ds(ex, 1), :] = (digits[0:1] * GRID_W + digits[1:2]).astype(jnp.int32)
        gcol_ref[s, ex] = jnp.sum(jnp.where(hit, aff_scr[pl.ds(r, 1), :], 0.0), axis=1, keepdims=True)
        return carry

    lax.fori_loop(0, e, compact, 0)


def _route(h2, w_router_t, cap):
    b, n, d = h2.shape
    e = w_router_t.shape[0]
    nbs = 4 if b % 4 == 0 else 1
    tri = jnp.asarray(np.triu(np.ones((LANES, LANES), np.float32)), BF16)
    t = np.arange(n)
    tok = np.zeros((SUBLANES, n), np.float32)
    tok[0], tok[1] = t // GRID_W, t % GRID_W
    return pl.pallas_call(
        functools.partial(_route_kernel, cap=cap),
        out_shape=[jax.ShapeDtypeStruct((b, e, cap), jnp.int32), jax.ShapeDtypeStruct((b, e, cap, 1), F32)],
        grid=(b // nbs,),
        in_specs=[
            pl.BlockSpec((nbs, n, d), lambda bi: (bi, 0, 0)),
            pl.BlockSpec((e, d), lambda bi: (0, 0)),
            pl.BlockSpec((LANES, LANES), lambda bi: (0, 0)),
            pl.BlockSpec((SUBLANES, n), lambda bi: (0, 0)),
        ],
        out_specs=[pl.BlockSpec((nbs, e, cap), lambda bi: (bi, 0, 0)),
                   pl.BlockSpec((nbs, e, cap, 1), lambda bi: (bi, 0, 0, 0))],
        scratch_shapes=[pltpu.VMEM((nbs * e, n), jnp.int32), pltpu.VMEM((nbs * e, n), F32)],
        compiler_params=_cp("parallel"),
        name="route",
    )(h2, w_router_t, tri, jnp.asarray(tok, BF16))


def _expert_kernel(idx_smem, gcol_ref, h_ref, wg_ref, wu_ref, wd_ref, x_hbm, g2_ref, ng_ref, o_ref,
                   hs_scr, y_scr, xbuf, sem, *, cap, nbs, n_exp, per_sample_gate):
    e = pl.program_id(1)
    n = h_ref.shape[1]

    def gather(expert, buf):
        for s in range(nbs):
            for j in range(cap):
                t = idx_smem[s, expert, j]
                hs_scr[buf, pl.ds(s * cap + j, 1), :] = h_ref[s, pl.ds(t, 1), :]

    def scatter_add(expert, buf):
        group = 8
        for s in range(nbs):
            for j0 in range(0, cap, group):
                ts = [idx_smem[s, expert, j0 + g] for g in range(group)]
                acc = [o_ref[s, pl.ds(t, 1), :] for t in ts]
                for g, t in enumerate(ts):
                    o_ref[s, pl.ds(t, 1), :] = acc[g] + y_scr[buf, pl.ds(s * cap + j0 + g, 1), :]

    @pl.when(e == 0)
    def _():
        o_ref[...] = jnp.zeros_like(o_ref)
        y_scr[1] = jnp.zeros(y_scr.shape[1:], F32)
        gather(0, 0)

    def step(cur):
        hs = _unpack_halves(hs_scr[cur])
        scatter_add(jnp.maximum(e - 1, 0), 1 - cur)
        gather(jnp.minimum(e + 1, n_exp - 1), 1 - cur)
        gcol = gcol_ref[...].reshape(nbs * cap, 1)
        a = jnp.dot(hs, wg_ref[...], preferred_element_type=F32)
        u = (a * _sigmoid(a)) * jnp.dot(hs, wu_ref[...], preferred_element_type=F32)
        y_scr[cur] = jnp.dot(u.astype(BF16), wd_ref[...], preferred_element_type=F32) * gcol

    for cur in range(2):
        pl.when(e % 2 == cur)(functools.partial(step, cur))

    rc = xbuf.shape[1]
    chunks = [(s, c) for s in range(nbs) for c in range(n // rc)]
    first = pl.program_id(0) * nbs

    def x_copy(k):
        s, c = chunks[k]
        return pltpu.make_async_copy(x_hbm.at[first + s, pl.ds(c * rc, rc), :], xbuf.at[k % 2], sem.at[k % 2])

    @pl.when(e == n_exp - 1)
    def _():
        x_copy(0).start()
        scatter_add(n_exp - 1, (n_exp - 1) % 2)
        for k, (s, c) in enumerate(chunks):
            if k + 1 < len(chunks):
                x_copy(k + 1).start()
            x_copy(k).wait()
            g2 = g2_ref[s if per_sample_gate else 0]
            sl = slice(c * rc, (c + 1) * rc)
            o_ref[s, sl, :] = xbuf[k % 2] + g2 * _rms(o_ref[s, sl, :], ng_ref[...])


def _experts(h2p, idx, gcol, wg, wu, wd, cap, x1, g2, ng, mod_row):
    b, n, dh = h2p.shape
    e, d, f = wg.shape
    nbs = max(1, min(b, 512 // cap))
    rc = min(n, 256)
    sample = lambda bi, ei: (bi, 0, 0)
    if mod_row is None:
        g2_spec = pl.BlockSpec((nbs, 1, d), sample)
    else:
        g2_spec = pl.BlockSpec((1, 1, d), lambda bi, ei: (mod_row, 0, 0))
    once = pl.Buffered(1)
    return pl.pallas_call(
        functools.partial(_expert_kernel, cap=cap, nbs=nbs, n_exp=e, per_sample_gate=mod_row is None),
        out_shape=jax.ShapeDtypeStruct((b, n, d), F32),
        grid=(b // nbs, e),
        in_specs=[
            pl.BlockSpec((nbs, e, cap), sample, memory_space=pltpu.SMEM),
            pl.BlockSpec((nbs, None, cap, 1), lambda bi, ei: (bi, ei, 0, 0)),
            pl.BlockSpec((nbs, n, dh), sample, pipeline_mode=once),
            pl.BlockSpec((None, d, f), lambda bi, ei: (ei, 0, 0)),
            pl.BlockSpec((None, d, f), lambda bi, ei: (ei, 0, 0)),
            pl.BlockSpec((None, f, d), lambda bi, ei: (ei, 0, 0)),
            pl.BlockSpec(memory_space=pl.ANY),
            g2_spec,
            pl.BlockSpec((1, d), lambda bi, ei: (0, 0)),
        ],
        out_specs=pl.BlockSpec((nbs, n, d), sample, pipeline_mode=once),
        scratch_shapes=[pltpu.VMEM((2, nbs * cap, dh), jnp.uint32), pltpu.VMEM((2, nbs * cap, d), F32),
                        pltpu.VMEM((2, rc, d), F32), pltpu.SemaphoreType.DMA((2,))],
        compiler_params=_cp("parallel", "arbitrary"),
        name="experts",
    )(idx, gcol, h2p, wg, wu, wd, x1, g2, ng.reshape(1, d))


def _split_cols(w, widths):
    offs = np.cumsum((0,) + tuple(widths))
    return [w[:, int(offs[i]):int(offs[i + 1])] for i in range(len(widths))]


def _layer(xl, xc, mods, p, need_ctx):
    bsz, s, d = xl.shape
    ctx_row = bsz
    sh1, sc1, g1, sh2, sc2, g2 = mods
    ng = p["norm_g"]
    na_w = (d // 128) * HEAD_DIM
    gq_w = (d // 128) * HEAD_DIM
    gkv_w = GQA_KV_HEADS * HEAD_DIM
    splits = (d, d, na_w, na_w, na_w, gq_w, gkv_w, gkv_w, N_BRANCH * d)
    acts = (None, "gelu", None, None, None, None, None, None, "sigmoid")
    w_in = p["w_in"].astype(BF16)

    lx, lg, nq, nk, nv, gq, gk, gv, gates = _inproj(xl, sh1, sc1, ng[0], w_in, splits, acts, None)
    if need_ctx:
        cx, cg, cnq, cnk, cnv, cgq, cgk, cgv, cgates = _inproj(xc, sh1, sc1, ng[0], w_in, splits, acts, ctx_row)
    else:
        parts = _split_cols(w_in, splits)
        kv_parts = (0, 3, 4, 6, 7)
        w_kv = jnp.concatenate([parts[i] for i in kv_parts], axis=1)
        cx, cnk, cnv, cgk, cgv = _inproj(xc, sh1, sc1, ng[0], w_kv, tuple(splits[i] for i in kv_parts),
                                         tuple(acts[i] for i in kv_parts), ctx_row)

    lp = {
        "conv_w": p["conv_w"], "conv_b": p["conv_b"].reshape(1, d),
        "wab": jnp.concatenate([p["lru_wa"], p["lru_wx"]], axis=-1).astype(BF16),
        "ba": p["lru_ba"].reshape(2, 1, d), "bx": p["lru_bx"].reshape(2, 1, d),
        "lam": p["lru_lam"].reshape(2, 1, d),
    }
    hc, hfin = _lru(cx, jnp.zeros((2, bsz, d), F32), lp)
    hl, _ = _lru(lx, hfin, lp)

    yb = _na(nq, nk, nv, cnk, cnv, p["na_rpb"])
    yc = _gqa(gq, cgk, cgv, gk, gv, p["qn_g"], p["kn_g"])

    woa, wob, woc, wout = (p[k].astype(BF16) for k in ("w_o_a", "w_o_b", "w_o_c", "w_out"))
    wr_t = p["w_router"].T.astype(BF16)
    wg, wu, wd = (p[k].astype(BF16) for k in ("w_gate", "w_up", "w_down"))

    def ffn(x1, h2, h2p, mod_row):
        n = x1.shape[1]
        cap = EC_CAPACITY * n // N_EXPERTS
        idx, gcol = _route(h2, wr_t, cap)
        return _experts(h2p, idx, gcol, wg, wu, wd, cap, x1, g2, ng[3], mod_row)

    xl = ffn(*_merge(hl, lg, yb, yc, gates, xl, g1, sh2, sc2, ng[1], ng[2], woa, wob, woc, wout, None), None)
    if need_ctx:
        ybc = _attn(cnq, cnk, cnv)
        ycc = _attn(cgq, cgk, cgv, qg=p["qn_g"], kg=p["kn_g"])
        xc = ffn(*_merge(hc, cg, ybc, ycc, cgates, xc, g1, sh2, sc2, ng[1], ng[2], woa, wob, woc, wout, ctx_row),
                 ctx_row)
    return xl, xc


def kernel(x, c, ctx, c_ctx, w_mod, b_mod, norm_g, w_in, conv_w, conv_b, lru_wa, lru_ba, lru_wx, lru_bx, lru_lam,
           na_rpb, qn_g, kn_g, w_o_a, w_o_b, w_o_c, w_out, w_router, w_gate, w_up, w_down):
    bsz, _, d = x.shape
    depth = w_mod.shape[0]
    rows = -(-(bsz + 1) // SUBLANES) * SUBLANES
    cond = jnp.zeros((rows, d), F32).at[:bsz].set(c).at[bsz].set(c_ctx)
    xl, xc = x, ctx
    for l in range(depth):
        p = {
            "norm_g": norm_g[l], "w_in": w_in[l], "conv_w": conv_w[l], "conv_b": conv_b[l],
            "lru_wa": lru_wa[l], "lru_ba": lru_ba[l], "lru_wx": lru_wx[l], "lru_bx": lru_bx[l],
            "lru_lam": lru_lam[l], "na_rpb": na_rpb[l], "qn_g": qn_g[l], "kn_g": kn_g[l],
            "w_o_a": w_o_a[l], "w_o_b": w_o_b[l], "w_o_c": w_o_c[l], "w_out": w_out[l],
            "w_router": w_router[l], "w_gate": w_gate[l], "w_up": w_up[l], "w_down": w_down[l],
        }
        m = _adaln(cond, w_mod[l], b_mod[l])
        mods = [m[:, i * d:(i + 1) * d].reshape(rows, 1, d) for i in range(6)]
        xl, xc = _layer(xl, xc, mods, p, l < depth - 1)
    return xl
```

```python
import functools

import jax
import jax.numpy as jnp
import numpy as np
from jax import lax
from jax.experimental import pallas as pl
from jax.experimental.pallas import tpu as pltpu

F32 = jnp.float32
BF16 = jnp.bfloat16

GRID_W = 64
HEAD_DIM = 64
LRU_BLOCKS = 8
CONV_W = 4
LRU_C = 8.0
NA_KH_MAX = 8
NA_KW = 16
GQA_KV_HEADS = 2
ROPE_BASE = 10000.0
N_EXPERTS = 16
EC_CAPACITY = 2
N_BRANCH = 3
EPS = 1e-6
NEG_INF = -1e30

LANES = 128
SUBLANES = 8
BF16_ROWS = 16
VMEM_LIMIT_BYTES = 56 * 1024 * 1024

_NT = (((1,), (1,)), ((), ()))


def _cp(*sem):
    return pltpu.CompilerParams(dimension_semantics=sem, vmem_limit_bytes=VMEM_LIMIT_BYTES)


def _sigmoid(x):
    return 0.5 * jnp.tanh(0.5 * x) + 0.5


def _rms(x, g):
    return x * lax.rsqrt(jnp.mean(x * x, axis=-1, keepdims=True) + EPS) * g


def _adaln_kernel(c_ref, w_ref, b_ref, o_ref):
    c = c_ref[...]
    s = (c * _sigmoid(c)).astype(BF16)
    o_ref[...] = jnp.dot(s, w_ref[...].astype(BF16), preferred_element_type=F32) + b_ref[...]


def _adaln(cond, w_mod, b_mod):
    r, d = cond.shape
    n = w_mod.shape[1]
    tn = n // 4
    return pl.pallas_call(
        _adaln_kernel,
        out_shape=jax.ShapeDtypeStruct((r, n), F32),
        grid=(n // tn,),
        in_specs=[
            pl.BlockSpec((r, d), lambda j: (0, 0)),
            pl.BlockSpec((d, tn), lambda j: (0, j)),
            pl.BlockSpec((1, tn), lambda j: (0, j)),
        ],
        out_specs=pl.BlockSpec((r, tn), lambda j: (0, j)),
        compiler_params=_cp("parallel"),
        name="adaln",
    )(cond, w_mod, b_mod.reshape(1, n))


_ACTIVATIONS = {None: lambda v: v, "gelu": jax.nn.gelu, "sigmoid": _sigmoid}


def _inproj_kernel(x_ref, sh_ref, sc_ref, g_ref, w_ref, *o_refs, splits, acts):
    h = (_rms(x_ref[...], g_ref[...]) * (1.0 + sc_ref[...]) + sh_ref[...]).astype(BF16)
    off = 0
    for o_ref, width, act in zip(o_refs, splits, acts):
        y = _ACTIVATIONS[act](jnp.dot(h, w_ref[:, off:off + width], preferred_element_type=F32))
        o_ref[...] = y.astype(o_ref.dtype)
        off += width


def _inproj(x, shift, scale, gain, w, splits, acts, mod_row):
    b, t, d = x.shape
    tm = min(t, 512)
    n = w.shape[1]
    if mod_row is None:
        mod_map = lambda bi, i: (bi, 0, 0)
    else:
        mod_map = lambda bi, i: (mod_row, 0, 0)
    return pl.pallas_call(
        functools.partial(_inproj_kernel, splits=splits, acts=acts),
        out_shape=[jax.ShapeDtypeStruct((b, t, s), BF16) for s in splits],
        grid=(b, t // tm),
        in_specs=[
            pl.BlockSpec((None, tm, d), lambda bi, i: (bi, i, 0)),
            pl.BlockSpec((None, 1, d), mod_map),
            pl.BlockSpec((None, 1, d), mod_map),
            pl.BlockSpec((1, d), lambda bi, i: (0, 0)),
            pl.BlockSpec((d, n), lambda bi, i: (0, 0)),
        ],
        out_specs=[pl.BlockSpec((None, tm, s), lambda bi, i: (bi, i, 0)) for s in splits],
        compiler_params=_cp("parallel", "parallel"),
        name="inproj",
    )(x, shift, scale, gain.reshape(1, d), w)


def _lru_kernel(x_ref, xp_ref, xn_ref, h0_ref, smat_ref, cw_ref, cb_ref, wab_ref, ba_ref, bx_ref, lam_ref,
                y_ref, hfin_ref, u_scr, a_scr, b_scr, h_scr, st_scr, *, tc, nchunks, nb):
    d = pl.program_id(1)
    j = pl.program_id(2)
    c = jnp.where(d == 0, j, nchunks - 1 - j)
    w = x_ref.shape[-1]
    cwide = 2 * LANES

    @pl.when(j == 0)
    def _():
        for cb in range(LRU_BLOCKS):
            h0 = h0_ref[:, cb * LANES:(cb + 1) * LANES]
            st_scr[cb] = jnp.broadcast_to(h0[:, None, :], (nb, SUBLANES, LANES))

    zero = jnp.zeros((), BF16)
    c_lam = -LRU_C * jax.nn.softplus(-lam_ref[...])
    for cc in range(w // cwide):
        sl2 = slice(cc * cwide, (cc + 1) * cwide)
        for bb in range(nb):
            xe = jnp.concatenate([jnp.where(c > 0, xp_ref[bb, :, sl2], zero), x_ref[bb, :, sl2],
                                  jnp.where(c < nchunks - 1, xn_ref[bb, :, sl2], zero)], axis=0)
            sh = jnp.dot(smat_ref[...], xe, preferred_element_type=F32)
            u = cb_ref[:, sl2]
            for k in range(CONV_W):
                u = u + sh[k * tc:(k + 1) * tc] * cw_ref[k:k + 1, sl2]
            u_scr[bb, :, sl2] = u
        for cb in range(cc * (cwide // LANES), (cc + 1) * (cwide // LANES)):
            sl = slice(cb * LANES, (cb + 1) * LANES)
            u = u_scr[:, :, sl].reshape(nb * tc, LANES)
            gates = jnp.dot(u.astype(BF16), wab_ref[cb], preferred_element_type=F32)
            r = _sigmoid(gates[:, :LANES] + ba_ref[:, sl])
            i = _sigmoid(gates[:, LANES:] + bx_ref[:, sl])
            log_a = c_lam[:, sl] * r
            a = jnp.exp(log_a)
            a_scr[cb] = a
            m = -jnp.tanh(log_a) * (a * a + 1.0)
            b_scr[cb] = jnp.where(m > 0.0, m * lax.rsqrt(m), 0.0) * (i * u)

    @pl.when(d == 0)
    def _():
        _block_scan(a_scr, b_scr, h_scr, st_scr, False, nb, tc)

    @pl.when(d == 1)
    def _():
        _block_scan(a_scr, b_scr, h_scr, st_scr, True, nb, tc)

    for cb in range(LRU_BLOCKS):
        y_ref[:, :, cb * LANES:(cb + 1) * LANES] = h_scr[cb].astype(y_ref.dtype)

    @pl.when(j == nchunks - 1)
    def _():
        for cb in range(LRU_BLOCKS):
            hfin_ref[:, cb * LANES:(cb + 1) * LANES] = st_scr[cb][:, 0, :]


def _block_scan(a_scr, b_scr, h_scr, st_scr, rev, nb, tc):
    nblk = tc // SUBLANES
    row = lax.broadcasted_iota(jnp.int32, (1, SUBLANES, LANES), 1)
    for cb in range(LRU_BLOCKS):
        a = a_scr[cb].reshape(nb * nblk, SUBLANES, LANES)
        b = b_scr[cb].reshape(nb * nblk, SUBLANES, LANES)
        for s in (1, 2, 4):
            shift, keep = (SUBLANES - s, row < SUBLANES - s) if rev else (s, row >= s)
            b = b + a * jnp.where(keep, pltpu.roll(b, shift, 1), 0.0)
            a = a * jnp.where(keep, pltpu.roll(a, shift, 1), 1.0)
        a = a.reshape(nb, nblk, SUBLANES, LANES)
        b = b.reshape(nb, nblk, SUBLANES, LANES)
        carry = st_scr[cb]
        last = 0 if rev else SUBLANES - 1
        for k in (reversed(range(nblk)) if rev else range(nblk)):
            h = a[:, k] * carry + b[:, k]
            h_scr[cb, :, k * SUBLANES:(k + 1) * SUBLANES, :] = h
            carry = jnp.broadcast_to(h[:, last:last + 1, :], h.shape)
        st_scr[cb] = carry


def _lru(x, h0, lp):
    b, t, w = x.shape
    nb = SUBLANES
    tc = min(t, 128)
    nchunks = t // tc
    hb = tc // BF16_ROWS
    nhb = t // BF16_ROWS

    def chunk(di, j):
        return jnp.where(di == 0, j, nchunks - 1 - j)

    smat = np.zeros((CONV_W * tc, tc + 2 * BF16_ROWS), np.float32)
    for k in range(CONV_W):
        smat[k * tc + np.arange(tc), BF16_ROWS + np.arange(tc) + k - CONV_W // 2] = 1.0

    vec = lambda g, di, j: (0, 0)
    dvec = lambda g, di, j: (di, 0, 0)
    return pl.pallas_call(
        functools.partial(_lru_kernel, tc=tc, nchunks=nchunks, nb=nb),
        out_shape=[jax.ShapeDtypeStruct((2, b, t, w), BF16), jax.ShapeDtypeStruct((2, b, w), F32)],
        grid=(b // nb, 2, nchunks),
        in_specs=[
            pl.BlockSpec((nb, tc, w), lambda g, di, j: (g, chunk(di, j), 0)),
            pl.BlockSpec((nb, BF16_ROWS, w), lambda g, di, j: (g, jnp.maximum(chunk(di, j) * hb - 1, 0), 0)),
            pl.BlockSpec((nb, BF16_ROWS, w), lambda g, di, j: (g, jnp.minimum((chunk(di, j) + 1) * hb, nhb - 1), 0)),
            pl.BlockSpec((None, nb, w), lambda g, di, j: (di, g, 0)),
            pl.BlockSpec(smat.shape, vec),
            pl.BlockSpec((CONV_W, w), vec),
            pl.BlockSpec((1, w), vec),
            pl.BlockSpec((None, LRU_BLOCKS, LANES, 2 * LANES), lambda g, di, j: (di, 0, 0, 0)),
            pl.BlockSpec((None, 1, w), dvec),
            pl.BlockSpec((None, 1, w), dvec),
            pl.BlockSpec((None, 1, w), dvec),
        ],
        out_specs=[
            pl.BlockSpec((None, nb, tc, w), lambda g, di, j: (di, g, chunk(di, j), 0)),
            pl.BlockSpec((None, nb, w), lambda g, di, j: (di, g, 0)),
        ],
        scratch_shapes=[
            pltpu.VMEM((nb, tc, w), F32),
            pltpu.VMEM((LRU_BLOCKS, nb * tc, LANES), F32),
            pltpu.VMEM((LRU_BLOCKS, nb * tc, LANES), F32),
            pltpu.VMEM((LRU_BLOCKS, nb, tc, LANES), F32),
            pltpu.VMEM((LRU_BLOCKS, nb, SUBLANES, LANES), F32),
        ],
        compiler_params=_cp("parallel", "arbitrary", "arbitrary"),
        name="rglru",
    )(x, x, x, h0, jnp.asarray(smat, BF16), lp["conv_w"], lp["conv_b"], lp["wab"], lp["ba"], lp["bx"], lp["lam"])


def _pair_rows(x2):
    low = lax.broadcasted_iota(jnp.int32, x2.shape, 1) < HEAD_DIM
    zero = jnp.zeros((), x2.dtype)
    return jnp.concatenate([jnp.where(low, x2, zero), jnp.where(low, zero, x2)], axis=0)


def _softmax_strips(s_ref, p_ref):
    sums = []
    for r in range(s_ref.shape[0] // BF16_ROWS):
        rows = slice(r * BF16_ROWS, (r + 1) * BF16_ROWS)
        s = s_ref[rows, :]
        p = jnp.exp(s - jnp.max(s, axis=-1, keepdims=True))
        sums.append(jnp.sum(p, axis=-1, keepdims=True))
        p_ref[rows, :] = p.astype(BF16)
    return sums


def _normalise_strips(o, sums):
    return jnp.concatenate([o[r * BF16_ROWS:(r + 1) * BF16_ROWS] / l for r, l in enumerate(sums)], axis=0)


def _na_kernel(var_ref, q_ref, k_ref, v_ref, kc_ref, vc_ref, bias_ref, o_ref, s_scr, p_scr,
               *, band, win_rows, kh, rows):
    del var_ref
    r0 = pl.program_id(1) * band
    ws = jnp.clip(r0 - kh // 2, 0, rows - win_rows)
    start = pl.multiple_of(ws * GRID_W, GRID_W)
    nwin = win_rows * GRID_W
    m = band * GRID_W
    npairs = q_ref.shape[-1] // LANES
    low = lax.broadcasted_iota(jnp.int32, (m, LANES), 1) < HEAD_DIM

    def scores(hp):
        sl = slice(hp * LANES, (hp + 1) * LANES)
        qs = _pair_rows(q_ref[:, sl] * jnp.asarray(HEAD_DIM ** -0.5, BF16))
        s_lat = lax.dot_general(qs, k_ref[pl.ds(start, nwin), sl], _NT, preferred_element_type=F32)
        s_scr[hp % 2, :, 0:nwin] = s_lat + bias_ref[2 * hp:2 * hp + 2].reshape(2 * m, nwin)
        s_scr[hp % 2, :, nwin:] = lax.dot_general(qs, kc_ref[:, sl], _NT, preferred_element_type=F32)

    def finish(hp):
        sl = slice(hp * LANES, (hp + 1) * LANES)
        sums = _softmax_strips(s_scr.at[hp % 2], p_scr.at[hp % 2])
        o = jnp.dot(p_scr[hp % 2, :, 0:nwin], v_ref[pl.ds(start, nwin), sl], preferred_element_type=F32)
        o = o + jnp.dot(p_scr[hp % 2, :, nwin:], vc_ref[:, sl], preferred_element_type=F32)
        o = _normalise_strips(o, sums)
        o_ref[:, sl] = jnp.where(low, o[:m], o[m:]).astype(o_ref.dtype)

    scores(0)
    for hp in range(npairs):
        if hp + 1 < npairs:
            scores(hp + 1)
        finish(hp)


def _na_bands(rows, band, win_rows):
    kh = min(NA_KH_MAX, rows)
    variants, var_of_band = [], []
    for r0 in range(0, rows, band):
        ws = int(np.clip(r0 - kh // 2, 0, rows - win_rows))
        key = (r0 - ws, tuple(int(np.clip(r0 + rr - kh // 2, 0, rows - kh)) - ws for rr in range(band)))
        if key not in variants:
            variants.append(key)
        var_of_band.append(variants.index(key))
    return variants, np.asarray(var_of_band, np.int32)


def _na_bias_table(rpb, rows, band, win_rows):
    kh = min(NA_KH_MAX, rows)
    col = np.arange(GRID_W)
    cs = np.clip(col - NA_KW // 2, 0, GRID_W - NA_KW)
    col_mask = (col[None, :] >= cs[:, None]) & (col[None, :] < cs[:, None] + NA_KW)
    dcol = np.clip(col[None, :] - col[:, None], -(NA_KW - 1), NA_KW - 1) + NA_KW - 1
    heads, n_drow, n_dcol = rpb.shape
    onehot = jnp.asarray(dcol.reshape(-1)[None, :] == np.arange(n_dcol)[:, None], F32)
    t2 = jnp.dot(rpb.reshape(heads * n_drow, n_dcol).astype(F32), onehot, precision=lax.Precision.HIGHEST)
    t2 = t2.reshape(heads, n_drow, GRID_W, GRID_W)
    t2 = jnp.where(col_mask[None, None], t2, NEG_INF)
    masked = jnp.full((heads, GRID_W, GRID_W), NEG_INF, F32)
    variants, _ = _na_bands(rows, band, win_rows)
    tabs = []
    for delta, first in variants:
        q_rows = []
        for rr in range(band):
            blocks = [t2[:, j - delta - rr + NA_KH_MAX - 1] if first[rr] <= j < first[rr] + kh else masked
                      for j in range(win_rows)]
            q_rows.append(jnp.concatenate(blocks, axis=-1))
        tabs.append(jnp.concatenate(q_rows, axis=1))
    return jnp.stack(tabs)


def _na(q, k, v, kc, vc, rpb):
    b, s, w = q.shape
    l = kc.shape[1]
    heads = w // HEAD_DIM
    rows = s // GRID_W
    kh = min(NA_KH_MAX, rows)
    band = min(4, rows)
    win_rows = min(-(-(kh + band - 1) // 2) * 2, rows)
    m, nwin = band * GRID_W, win_rows * GRID_W
    bias = _na_bias_table(rpb, rows, band, win_rows)
    _, var_of_band = _na_bands(rows, band, win_rows)

    full = lambda bi, r, var: (bi, 0, 0)
    grid_spec = pltpu.PrefetchScalarGridSpec(
        num_scalar_prefetch=1,
        grid=(b, rows // band),
        in_specs=[
            pl.BlockSpec((None, m, w), lambda bi, r, var: (bi, r, 0)),
            pl.BlockSpec((None, s, w), full),
            pl.BlockSpec((None, s, w), full),
            pl.BlockSpec((None, l, w), full),
            pl.BlockSpec((None, l, w), full),
            pl.BlockSpec((None, heads, m, nwin), lambda bi, r, var: (var[r], 0, 0, 0)),
        ],
        out_specs=pl.BlockSpec((None, m, w), lambda bi, r, var: (bi, r, 0)),
        scratch_shapes=[pltpu.VMEM((2, 2 * m, nwin + l), F32), pltpu.VMEM((2, 2 * m, nwin + l), BF16)],
    )
    return pl.pallas_call(
        functools.partial(_na_kernel, band=band, win_rows=win_rows, kh=kh, rows=rows),
        out_shape=jax.ShapeDtypeStruct((b, s, w), BF16),
        grid_spec=grid_spec,
        compiler_params=_cp("parallel", "arbitrary"),
        name="natten",
    )(jnp.asarray(var_of_band), q, k, v, kc, vc, bias)


def _head_norm(x, g, gmat):
    cols = []
    for c in range(x.shape[1] // LANES):
        xc = x[:, c * LANES:(c + 1) * LANES]
        x2 = xc * xc
        hi = x2.astype(BF16)
        lo = (x2 - hi.astype(F32)).astype(BF16)
        ms = jnp.dot(hi, gmat, preferred_element_type=F32) + jnp.dot(lo, gmat, preferred_element_type=F32)
        cols.append(xc * lax.rsqrt(ms + EPS) * g)
    return cols[0] if len(cols) == 1 else jnp.concatenate(cols, axis=-1)


def _rope(x, cos, sin):
    half = HEAD_DIM // 2
    lane = lax.broadcasted_iota(jnp.int32, (x.shape[0], LANES), 1)
    low = (lane % HEAD_DIM) < half
    cols = []
    for c in range(x.shape[1] // LANES):
        xc = x[:, c * LANES:(c + 1) * LANES]
        sw = jnp.where(low, pltpu.roll(xc, LANES - half, 1), pltpu.roll(xc, half, 1))
        cols.append(xc * cos + sw * sin)
    return cols[0] if len(cols) == 1 else jnp.concatenate(cols, axis=-1)


def _attn_kernel(*refs, norm, group, tq):
    if norm:
        q_ref, k_ref, v_ref, qg_ref, kg_ref, gmat_ref, o_ref = refs
    else:
        q_ref, k_ref, v_ref, o_ref = refs
    q = q_ref[...].astype(F32)
    k = k_ref[...]
    if norm:
        q = _head_norm(q, qg_ref[...], gmat_ref[...])
        k = _head_norm(k.astype(F32), kg_ref[...], gmat_ref[...]).astype(BF16)
    q = (q * (HEAD_DIM ** -0.5)).astype(BF16)

    outs = []
    for g in range(k_ref.shape[1] // HEAD_DIM):
        sl = slice(g * HEAD_DIM, (g + 1) * HEAD_DIM)
        qs = [q[:, (g * group + jj) * HEAD_DIM:(g * group + jj + 1) * HEAD_DIM] for jj in range(group)]
        qs = qs[0] if group == 1 else jnp.concatenate(qs, axis=0)
        s = lax.dot_general(qs, k[:, sl], _NT, preferred_element_type=F32)
        p = jnp.exp(s - jnp.max(s, axis=-1, keepdims=True))
        o = jnp.dot(p.astype(BF16), v_ref[:, sl], preferred_element_type=F32) / jnp.sum(p, axis=-1, keepdims=True)
        for jj in range(group):
            outs.append(o[jj * tq:(jj + 1) * tq])
    o_ref[...] = jnp.concatenate(outs, axis=-1).astype(o_ref.dtype)


def _dup_heads(x):
    low = lax.broadcasted_iota(jnp.int32, x.shape, 1) < HEAD_DIM
    sw = pltpu.roll(x, HEAD_DIM, 1)
    return jnp.where(low, x, sw), jnp.where(low, sw, x)


def _gqa_kernel(q_ref, kc_ref, vc_ref, kl_ref, vl_ref, qg_ref, kg_ref, gmat_ref, cosq_ref, sinq_ref, cosk_ref,
                sink_ref, o_ref, kc_scr, kl_scr, vc_scr, vl_scr, s_scr, p_scr, *, sub):
    n_ctx = kc_ref.shape[0]
    n_kv = GQA_KV_HEADS

    @pl.when(pl.program_id(1) == 0)
    def _():
        gmat = gmat_ref[...]
        kc = _head_norm(kc_ref[...].astype(F32), kg_ref[...], gmat)
        kl = _rope(_head_norm(kl_ref[...].astype(F32), kg_ref[...], gmat), cosk_ref[...], sink_ref[...])
        for scr, val in ((kc_scr, kc), (kl_scr, kl), (vc_scr, vc_ref[...].astype(F32)),
                         (vl_scr, vl_ref[...].astype(F32))):
            for g, dup in enumerate(_dup_heads(val)):
                scr[g] = dup.astype(BF16)

    tq = q_ref.shape[0]
    chunks = []
    for c in range(q_ref.shape[1] // LANES):
        qc = q_ref[:, c * LANES:(c + 1) * LANES].astype(F32)
        qc = _rope(_head_norm(qc, qg_ref[...], gmat_ref[...]), cosq_ref[...], sinq_ref[...])
        chunks.append((qc * (HEAD_DIM ** -0.5)).astype(BF16))
    pairs_per_group = len(chunks) // n_kv
    low = lax.broadcasted_iota(jnp.int32, (sub, LANES), 1) < HEAD_DIM
    units = [(t, g) for t in range(tq // sub) for g in range(n_kv)]

    def scores(u):
        t, g = units[u]
        rows = [_pair_rows(chunks[g * pairs_per_group + j][t * sub:(t + 1) * sub]) for j in range(pairs_per_group)]
        qs = jnp.concatenate(rows, axis=0)
        s_scr[u % 2, :, 0:n_ctx] = lax.dot_general(qs, kc_scr[g], _NT, preferred_element_type=F32)
        s_scr[u % 2, :, n_ctx:] = lax.dot_general(qs, kl_scr[g], _NT, preferred_element_type=F32)

    def finish(u):
        t, g = units[u]
        sums = _softmax_strips(s_scr.at[u % 2], p_scr.at[u % 2])
        o = jnp.dot(p_scr[u % 2, :, 0:n_ctx], vc_scr[g], preferred_element_type=F32)
        o = o + jnp.dot(p_scr[u % 2, :, n_ctx:], vl_scr[g], preferred_element_type=F32)
        o = _normalise_strips(o, sums)
        for j in range(pairs_per_group):
            c = g * pairs_per_group + j
            lo, hi = o[2 * j * sub:(2 * j + 1) * sub], o[(2 * j + 1) * sub:(2 * j + 2) * sub]
            o_ref[t * sub:(t + 1) * sub, c * LANES:(c + 1) * LANES] = jnp.where(low, lo, hi).astype(o_ref.dtype)

    scores(0)
    for u in range(len(units)):
        if u + 1 < len(units):
            scores(u + 1)
        finish(u)


def _gqa(q, kc, vc, kl, vl, qg, kg):
    b, t, wq = q.shape
    l, wk = kc.shape[1], kc.shape[2]
    s = kl.shape[1]
    assert wk == GQA_KV_HEADS * HEAD_DIM == LANES
    tq = min(t, 512)
    sub = min(tq, 256)
    m_unit = 2 * (wq // LANES // GQA_KV_HEADS) * sub
    reps = LANES // HEAD_DIM
    gmat = _head_mean_matrix()
    cos, sin = _rope_tables(t)
    full = lambda bi, i: (bi, 0, 0)
    const = lambda bi, i: (0, 0)
    return pl.pallas_call(
        functools.partial(_gqa_kernel, sub=sub),
        out_shape=jax.ShapeDtypeStruct((b, t, wq), BF16),
        grid=(b, t // tq),
        in_specs=[
            pl.BlockSpec((None, tq, wq), lambda bi, i: (bi, i, 0)),
            pl.BlockSpec((None, l, wk), full), pl.BlockSpec((None, l, wk), full),
            pl.BlockSpec((None, s, wk), full), pl.BlockSpec((None, s, wk), full),
            pl.BlockSpec((1, LANES), const), pl.BlockSpec((1, LANES), const), pl.BlockSpec((LANES, LANES), const),
            pl.BlockSpec((tq, LANES), lambda bi, i: (i, 0)), pl.BlockSpec((tq, LANES), lambda bi, i: (i, 0)),
            pl.BlockSpec((s, LANES), const), pl.BlockSpec((s, LANES), const),
        ],
        out_specs=pl.BlockSpec((None, tq, wq), lambda bi, i: (bi, i, 0)),
        scratch_shapes=[
            pltpu.VMEM((GQA_KV_HEADS, l, LANES), BF16), pltpu.VMEM((GQA_KV_HEADS, s, LANES), BF16),
            pltpu.VMEM((GQA_KV_HEADS, l, LANES), BF16), pltpu.VMEM((GQA_KV_HEADS, s, LANES), BF16),
            pltpu.VMEM((2, m_unit, l + s), F32), pltpu.VMEM((2, m_unit, l + s), BF16),
        ],
        compiler_params=_cp("parallel", "arbitrary"),
        name="gqa",
    )(q, kc, vc, kl, vl, jnp.tile(qg, reps).reshape(1, LANES), jnp.tile(kg, reps).reshape(1, LANES), gmat,
      cos, sin, cos, sin)


def _rope_tables(n_tokens):
    t = np.arange(n_tokens)
    pos = np.stack([t // GRID_W, t % GRID_W], axis=-1).astype(np.float32)
    n_freq = HEAD_DIM // 4
    inv_freq = jnp.asarray(ROPE_BASE, F32) ** (-jnp.arange(n_freq, dtype=F32) / n_freq)
    ang = (jnp.asarray(pos)[:, :, None] * inv_freq).reshape(n_tokens, 2 * n_freq)
    cos, sin = jnp.cos(ang), jnp.sin(ang)
    reps = LANES // HEAD_DIM
    return jnp.tile(jnp.concatenate([cos, cos], -1), (1, reps)), jnp.tile(jnp.concatenate([-sin, sin], -1), (1, reps))


def _head_mean_matrix():
    blk = np.arange(LANES) // HEAD_DIM
    return jnp.asarray((blk[:, None] == blk[None, :]).astype(np.float32) / HEAD_DIM, BF16)


def _attn(q, k, v, qg=None, kg=None):
    b, t, wq = q.shape
    wk = k.shape[2]
    norm = qg is not None
    tok = lambda w: pl.BlockSpec((None, t, w), lambda bi: (bi, 0, 0))
    args = [q, k, v]
    in_specs = [tok(wq), tok(wk), tok(wk)]
    if norm:
        reps = LANES // HEAD_DIM
        args += [jnp.tile(qg, reps).reshape(1, LANES), jnp.tile(kg, reps).reshape(1, LANES), _head_mean_matrix()]
        in_specs += [pl.BlockSpec((1, LANES), lambda bi: (0, 0))] * 2
        in_specs += [pl.BlockSpec((LANES, LANES), lambda bi: (0, 0))]
    return pl.pallas_call(
        functools.partial(_attn_kernel, norm=norm, group=wq // wk, tq=t),
        out_shape=jax.ShapeDtypeStruct((b, t, wq), BF16),
        grid=(b,),
        in_specs=in_specs,
        out_specs=tok(wq),
        compiler_params=_cp("parallel"),
        name="attn",
    )(*args)


def _merge_kernel(hf_ref, hr_ref, gl_ref, yb_ref, yc_ref, gt_ref, x_ref, g1_ref, sh2_ref, sc2_ref, ng1_ref, ng2_ref,
                  woa_ref, wob_ref, woc_ref, wout_ref, x1_ref, h2_ref, h2p_ref):
    d = x_ref.shape[-1]
    ya = (hf_ref[...].astype(F32) + hr_ref[...].astype(F32)) * gl_ref[...].astype(F32)
    pa = jnp.dot(ya.astype(BF16), woa_ref[...], preferred_element_type=F32)
    pb = jnp.dot(yb_ref[...], wob_ref[...], preferred_element_type=F32)
    pc = jnp.dot(yc_ref[...], woc_ref[...], preferred_element_type=F32)
    ga = gt_ref[:, 0:d].astype(F32)
    gb = gt_ref[:, d:2 * d].astype(F32)
    gc = gt_ref[:, 2 * d:3 * d].astype(F32)
    m = ga * pa + gb * pb + gc * pc
    y = jnp.dot(m.astype(BF16), wout_ref[...], preferred_element_type=F32)
    x1 = x_ref[...] + g1_ref[...] * _rms(y, ng1_ref[...])
    x1_ref[...] = x1
    h2 = (_rms(x1, ng2_ref[...]) * (1.0 + sc2_ref[...]) + sh2_ref[...]).astype(BF16)
    h2_ref[...] = h2
    h2p_ref[...] = _pack_halves(h2)


def _pack_halves(xb):
    half = xb.shape[1] // 2
    bits = pltpu.bitcast(xb.astype(F32), jnp.uint32)
    return bits[:, :half] | (bits[:, half:] >> 16)


def _unpack_halves(words):
    hi = pltpu.bitcast(words & jnp.uint32(0xFFFF0000), F32)
    lo = pltpu.bitcast(words << 16, F32)
    return jnp.concatenate([hi, lo], axis=1).astype(BF16)


def _merge(h, gl, yb, yc, gates, x, g1, sh2, sc2, ng1, ng2, woa, wob, woc, wout, mod_row):
    b, t, d = x.shape
    tm = min(t, 512)
    wl, wb, wc = gl.shape[2], yb.shape[2], yc.shape[2]
    if mod_row is None:
        mod_map = lambda bi, i: (bi, 0, 0)
    else:
        mod_map = lambda bi, i: (mod_row, 0, 0)
    tok = lambda w: pl.BlockSpec((None, tm, w), lambda bi, i: (bi, i, 0))
    mod = pl.BlockSpec((None, 1, d), mod_map)
    const = lambda r, c: pl.BlockSpec((r, c), lambda bi, i: (0, 0))
    return pl.pallas_call(
        _merge_kernel,
        out_shape=[jax.ShapeDtypeStruct((b, t, d), F32), jax.ShapeDtypeStruct((b, t, d), BF16),
                   jax.ShapeDtypeStruct((b, t, d // 2), jnp.uint32)],
        grid=(b, t // tm),
        in_specs=[
            pl.BlockSpec((None, None, tm, wl), lambda bi, i: (0, bi, i, 0)),
            pl.BlockSpec((None, None, tm, wl), lambda bi, i: (1, bi, i, 0)),
            tok(wl), tok(wb), tok(wc), tok(N_BRANCH * d), tok(d),
            mod, mod, mod, const(1, d), const(1, d),
            const(wl, d), const(wb, d), const(wc, d), const(d, d),
        ],
        out_specs=[tok(d), tok(d), tok(d // 2)],
        compiler_params=_cp("parallel", "parallel"),
        name="merge",
    )(h, h, gl, yb, yc, gates, x, g1, sh2, sc2, ng1.reshape(1, d), ng2.reshape(1, d), woa, wob, woc, wout)


def _lane_cumsum(x, tri):
    e, n = x.shape
    carry = jnp.zeros((e, 1), F32)
    cols = []
    for c in range(n // LANES):
        part = jnp.dot(x[:, c * LANES:(c + 1) * LANES].astype(BF16), tri, preferred_element_type=F32) + carry
        cols.append(part)
        carry = part[:, LANES - 1:LANES]
    return cols[0] if len(cols) == 1 else jnp.concatenate(cols, axis=-1)


def _route_kernel(h_ref, wr_ref, tri_ref, tok_ref, idx_ref, gcol_ref, pos_scr, aff_scr, *, cap):
    nbs, n_exp = h_ref.shape[0], wr_ref.shape[0]
    affs = []
    for s in range(nbs):
        logits = lax.dot_general(wr_ref[...], h_ref[s], _NT, preferred_element_type=F32)
        z = jnp.exp(logits - jnp.max(logits, axis=0, keepdims=True))
        affs.append(z / jnp.sum(z, axis=0, keepdims=True))
    aff = affs[0] if nbs == 1 else jnp.concatenate(affs, axis=0)
    aff_scr[...] = aff
    bits = pltpu.bitcast(aff, jnp.int32)
    e, n = bits.shape

    def bisect(_, lohi):
        lo, hi = lohi
        mid = lo + ((hi - lo + 1) >> 1)
        cnt = jnp.sum(jnp.where(bits >= mid, 1.0, 0.0), axis=1, keepdims=True)
        ok = cnt >= cap
        return jnp.where(ok, mid, lo), jnp.where(ok, hi, mid - 1)

    one_bits = 0x3F800000
    lo, _ = lax.fori_loop(0, 31, bisect, (jnp.zeros((e, 1), jnp.int32), jnp.full((e, 1), one_bits, jnp.int32)))
    gt = jnp.where(bits > lo, 1.0, 0.0)
    eq = jnp.where(bits == lo, 1.0, 0.0)
    need = cap - jnp.sum(gt, axis=1, keepdims=True)
    tri = tri_ref[...]
    sel = gt + jnp.where(_lane_cumsum(eq, tri) <= need, eq, 0.0)
    pos = _lane_cumsum(sel, tri) - 1.0
    pos_scr[...] = jnp.where(sel > 0.0, pos, -1.0).astype(jnp.int32)

    slot = lax.broadcasted_iota(jnp.int32, (cap, n), 0)

    def compact(r, carry):
        hit = pos_scr[pl.ds(r, 1), :] == slot
        onehot = jnp.where(hit, 1.0, 0.0).astype(BF16)
        digits = lax.dot_general(tok_ref[...], onehot, _NT, preferred_element_type=F32)
        s, ex = r // n_exp, r % n_exp
        idx_ref[s, pl.ds(ex, 1), :] = (digits[0:1] * GRID_W + digits[1:2]).astype(jnp.int32)
        gcol_ref[s, ex] = jnp.sum(jnp.where(hit, aff_scr[pl.ds(r, 1), :], 0.0), axis=1, keepdims=True)
        return carry

    lax.fori_loop(0, e, compact, 0, unroll=2)


def _route(h2, w_router_t, cap):
    b, n, d = h2.shape
    e = w_router_t.shape[0]
    nbs = 4 if b % 4 == 0 else 1
    tri = jnp.asarray(np.triu(np.ones((LANES, LANES), np.float32)), BF16)
    t = np.arange(n)
    tok = np.zeros((SUBLANES, n), np.float32)
    tok[0], tok[1] = t // GRID_W, t % GRID_W
    return pl.pallas_call(
        functools.partial(_route_kernel, cap=cap),
        out_shape=[jax.ShapeDtypeStruct((b, e, cap), jnp.int32), jax.ShapeDtypeStruct((b, e, cap, 1), F32)],
        grid=(b // nbs,),
        in_specs=[
            pl.BlockSpec((nbs, n, d), lambda bi: (bi, 0, 0)),
            pl.BlockSpec((e, d), lambda bi: (0, 0)),
            pl.BlockSpec((LANES, LANES), lambda bi: (0, 0)),
            pl.BlockSpec((SUBLANES, n), lambda bi: (0, 0)),
        ],
        out_specs=[pl.BlockSpec((nbs, e, cap), lambda bi: (bi, 0, 0)),
                   pl.BlockSpec((nbs, e, cap, 1), lambda bi: (bi, 0, 0, 0))],
        scratch_shapes=[pltpu.VMEM((nbs * e, n), jnp.int32), pltpu.VMEM((nbs * e, n), F32)],
        compiler_params=_cp("parallel"),
        name="route",
    )(h2, w_router_t, tri, jnp.asarray(tok, BF16))


def _expert_kernel(idx_smem, gcol_ref, h_ref, wg_ref, wu_ref, wd_ref, x_hbm, g2_ref, ng_ref, o_hbm,
                   o_ref, hs_scr, y_scr, xbuf, sem, osem, *, cap, nbs, n_exp, per_sample_gate):
    e = pl.program_id(1)
    n = h_ref.shape[1]

    def gather(expert, buf):
        for s in range(nbs):
            for j in range(cap):
                t = idx_smem[s, expert, j]
                hs_scr[buf, pl.ds(s * cap + j, 1), :] = h_ref[s, pl.ds(t, 1), :]

    def scatter_add(expert, buf):
        group = 8
        for s in range(nbs):
            for j0 in range(0, cap, group):
                ts = [idx_smem[s, expert, j0 + g] for g in range(group)]
                acc = [o_ref[s, pl.ds(t, 1), :] for t in ts]
                for g, t in enumerate(ts):
                    o_ref[s, pl.ds(t, 1), :] = acc[g] + y_scr[buf, pl.ds(s * cap + j0 + g, 1), :]

    @pl.when(e == 0)
    def _():
        o_ref[...] = jnp.zeros_like(o_ref)
        y_scr[1] = jnp.zeros(y_scr.shape[1:], F32)
        gather(0, 0)

    def step(cur):
        hs = _unpack_halves(hs_scr[cur])
        scatter_add(jnp.maximum(e - 1, 0), 1 - cur)
        gather(jnp.minimum(e + 1, n_exp - 1), 1 - cur)
        gcol = gcol_ref[...].reshape(nbs * cap, 1)
        a = jnp.dot(hs, wg_ref[...], preferred_element_type=F32)
        u = (a * _sigmoid(a)) * jnp.dot(hs, wu_ref[...], preferred_element_type=F32)
        y_scr[cur] = jnp.dot(u.astype(BF16), wd_ref[...], preferred_element_type=F32) * gcol

    for cur in range(2):
        pl.when(e % 2 == cur)(functools.partial(step, cur))

    rc = xbuf.shape[1]
    chunks = [(s, c) for s in range(nbs) for c in range(n // rc)]
    first = pl.program_id(0) * nbs

    def x_copy(k):
        s, c = chunks[k]
        return pltpu.make_async_copy(x_hbm.at[first + s, pl.ds(c * rc, rc), :], xbuf.at[k % 2], sem.at[k % 2])

    def o_copy(k):
        s, c = chunks[k]
        return pltpu.make_async_copy(o_ref.at[s, pl.ds(c * rc, rc), :], o_hbm.at[first + s, pl.ds(c * rc, rc), :],
                                     osem.at[k])

    @pl.when(e == n_exp - 1)
    def _():
        x_copy(0).start()
        scatter_add(n_exp - 1, (n_exp - 1) % 2)
        for k, (s, c) in enumerate(chunks):
            if k + 1 < len(chunks):
                x_copy(k + 1).start()
            x_copy(k).wait()
            g2 = g2_ref[s if per_sample_gate else 0]
            sl = slice(c * rc, (c + 1) * rc)
            o_ref[s, sl, :] = xbuf[k % 2] + g2 * _rms(o_ref[s, sl, :], ng_ref[...])
            o_copy(k).start()
        for k in range(len(chunks)):
            o_copy(k).wait()


def _experts(h2p, idx, gcol, wg, wu, wd, cap, x1, g2, ng, mod_row):
    b, n, dh = h2p.shape
    e, d, f = wg.shape
    nbs = max(1, min(b, 512 // cap))
    rc = min(n, 256)
    sample = lambda bi, ei: (bi, 0, 0)
    if mod_row is None:
        g2_spec = pl.BlockSpec((nbs, 1, d), sample)
    else:
        g2_spec = pl.BlockSpec((1, 1, d), lambda bi, ei: (mod_row, 0, 0))
    return pl.pallas_call(
        functools.partial(_expert_kernel, cap=cap, nbs=nbs, n_exp=e, per_sample_gate=mod_row is None),
        out_shape=jax.ShapeDtypeStruct((b, n, d), F32),
        grid=(b // nbs, e),
        in_specs=[
            pl.BlockSpec((nbs, e, cap), sample, memory_space=pltpu.SMEM),
            pl.BlockSpec((nbs, None, cap, 1), lambda bi, ei: (bi, ei, 0, 0)),
            pl.BlockSpec((nbs, n, dh), sample),
            pl.BlockSpec((None, d, f), lambda bi, ei: (ei, 0, 0)),
            pl.BlockSpec((None, d, f), lambda bi, ei: (ei, 0, 0)),
            pl.BlockSpec((None, f, d), lambda bi, ei: (ei, 0, 0)),
            pl.BlockSpec(memory_space=pl.ANY),
            g2_spec,
            pl.BlockSpec((1, d), lambda bi, ei: (0, 0)),
        ],
        out_specs=pl.BlockSpec(memory_space=pl.ANY),
        scratch_shapes=[pltpu.VMEM((nbs, n, d), F32),
                        pltpu.VMEM((2, nbs * cap, dh), jnp.uint32), pltpu.VMEM((2, nbs * cap, d), F32),
                        pltpu.VMEM((2, rc, d), F32), pltpu.SemaphoreType.DMA((2,)),
                        pltpu.SemaphoreType.DMA((nbs * (n // rc),))],
        compiler_params=_cp("parallel", "arbitrary"),
        name="experts",
    )(idx, gcol, h2p, wg, wu, wd, x1, g2, ng.reshape(1, d))


def _split_cols(w, widths):
    offs = np.cumsum((0,) + tuple(widths))
    return [w[:, int(offs[i]):int(offs[i + 1])] for i in range(len(widths))]


def _layer(xl, xc, mods, p, need_ctx):
    bsz, s, d = xl.shape
    ctx_row = bsz
    sh1, sc1, g1, sh2, sc2, g2 = mods
    ng = p["norm_g"]
    na_w = (d // 128) * HEAD_DIM
    gq_w = (d // 128) * HEAD_DIM
    gkv_w = GQA_KV_HEADS * HEAD_DIM
    splits = (d, d, na_w, na_w, na_w, gq_w, gkv_w, gkv_w, N_BRANCH * d)
    acts = (None, "gelu", None, None, None, None, None, None, "sigmoid")
    w_in = p["w_in"].astype(BF16)

    lx, lg, nq, nk, nv, gq, gk, gv, gates = _inproj(xl, sh1, sc1, ng[0], w_in, splits, acts, None)
    if need_ctx:
        cx, cg, cnq, cnk, cnv, cgq, cgk, cgv, cgates = _inproj(xc, sh1, sc1, ng[0], w_in, splits, acts, ctx_row)
    else:
        parts = _split_cols(w_in, splits)
        kv_parts = (0, 3, 4, 6, 7)
        w_kv = jnp.concatenate([parts[i] for i in kv_parts], axis=1)
        cx, cnk, cnv, cgk, cgv = _inproj(xc, sh1, sc1, ng[0], w_kv, tuple(splits[i] for i in kv_parts),
                                         tuple(acts[i] for i in kv_parts), ctx_row)

    lp = {
        "conv_w": p["conv_w"], "conv_b": p["conv_b"].reshape(1, d),
        "wab": jnp.concatenate([p["lru_wa"], p["lru_wx"]], axis=-1).astype(BF16),
        "ba": p["lru_ba"].reshape(2, 1, d), "bx": p["lru_bx"].reshape(2, 1, d),
        "lam": p["lru_lam"].reshape(2, 1, d),
    }
    hc, hfin = _lru(cx, jnp.zeros((2, bsz, d), F32), lp)
    hl, _ = _lru(lx, hfin, lp)

    yb = _na(nq, nk, nv, cnk, cnv, p["na_rpb"])
    yc = _gqa(gq, cgk, cgv, gk, gv, p["qn_g"], p["kn_g"])

    woa, wob, woc, wout = (p[k].astype(BF16) for k in ("w_o_a", "w_o_b", "w_o_c", "w_out"))
    wr_t = p["w_router"].T.astype(BF16)
    wg, wu, wd = (p[k].astype(BF16) for k in ("w_gate", "w_up", "w_down"))

    def ffn(x1, h2, h2p, mod_row):
        n = x1.shape[1]
        cap = EC_CAPACITY * n // N_EXPERTS
        idx, gcol = _route(h2, wr_t, cap)
        return _experts(h2p, idx, gcol, wg, wu, wd, cap, x1, g2, ng[3], mod_row)

    xl = ffn(*_merge(hl, lg, yb, yc, gates, xl, g1, sh2, sc2, ng[1], ng[2], woa, wob, woc, wout, None), None)
    if need_ctx:
        ybc = _attn(cnq, cnk, cnv)
        ycc = _attn(cgq, cgk, cgv, qg=p["qn_g"], kg=p["kn_g"])
        xc = ffn(*_merge(hc, cg, ybc, ycc, cgates, xc, g1, sh2, sc2, ng[1], ng[2], woa, wob, woc, wout, ctx_row),
                 ctx_row)
    return xl, xc


def kernel(x, c, ctx, c_ctx, w_mod, b_mod, norm_g, w_in, conv_w, conv_b, lru_wa, lru_ba, lru_wx, lru_bx, lru_lam,
           na_rpb, qn_g, kn_g, w_o_a, w_o_b, w_o_c, w_out, w_router, w_gate, w_up, w_down):
    bsz, _, d = x.shape
    depth = w_mod.shape[0]
    rows = -(-(bsz + 1) // SUBLANES) * SUBLANES
    cond = jnp.zeros((rows, d), F32).at[:bsz].set(c).at[bsz].set(c_ctx)
    xl, xc = x, ctx
    for l in range(depth):
        p = {
            "norm_g": norm_g[l], "w_in": w_in[l], "conv_w": conv_w[l], "conv_b": conv_b[l],
            "lru_wa": lru_wa[l], "lru_ba": lru_ba[l], "lru_wx": lru_wx[l], "lru_bx": lru_bx[l],
            "lru_lam": lru_lam[l], "na_rpb": na_rpb[l], "qn_g": qn_g[l], "kn_g": kn_g[l],
            "w_o_a": w_o_a[l], "w_o_b": w_o_b[l], "w_o_c": w_o_c[l], "w_out": w_out[l],
            "w_router": w_router[l], "w_gate": w_gate[l], "w_up": w_up[l], "w_down": w_down[l],
        }
        m = _adaln(cond, w_mod[l], b_mod[l])
        mods = [m[:, i * d:(i + 1) * d].reshape(rows, 1, d) for i in range(6)]
        xl, xc = _layer(xl, xc, mods, p, l < depth - 1)
    return xl
```

```python
import functools

import jax
import jax.numpy as jnp
import numpy as np
from jax import lax
from jax.experimental import pallas as pl
from jax.experimental.pallas import tpu as pltpu

F32 = jnp.float32
BF16 = jnp.bfloat16

GRID_W = 64
HEAD_DIM = 64
LRU_BLOCKS = 8
CONV_W = 4
LRU_C = 8.0
NA_KH_MAX = 8
NA_KW = 16
GQA_KV_HEADS = 2
ROPE_BASE = 10000.0
N_EXPERTS = 16
EC_CAPACITY = 2
N_BRANCH = 3
EPS = 1e-6
NEG_INF = -1e30

LANES = 128
SUBLANES = 8
BF16_ROWS = 16
VMEM_LIMIT_BYTES = 56 * 1024 * 1024

_NT = (((1,), (1,)), ((), ()))


def _cp(*sem):
    return pltpu.CompilerParams(dimension_semantics=sem, vmem_limit_bytes=VMEM_LIMIT_BYTES)


def _sigmoid(x):
    return 0.5 * jnp.tanh(0.5 * x) + 0.5


def _rms(x, g):
    return x * lax.rsqrt(jnp.mean(x * x, axis=-1, keepdims=True) + EPS) * g


def _adaln_kernel(c_ref, w_ref, b_ref, o_ref):
    c = c_ref[...]
    s = (c * _sigmoid(c)).astype(BF16)
    o_ref[...] = jnp.dot(s, w_ref[...].astype(BF16), preferred_element_type=F32) + b_ref[...]


def _adaln(cond, w_mod, b_mod):
    r, d = cond.shape
    n = w_mod.shape[1]
    tn = n // 4
    return pl.pallas_call(
        _adaln_kernel,
        out_shape=jax.ShapeDtypeStruct((r, n), F32),
        grid=(n // tn,),
        in_specs=[
            pl.BlockSpec((r, d), lambda j: (0, 0)),
            pl.BlockSpec((d, tn), lambda j: (0, j)),
            pl.BlockSpec((1, tn), lambda j: (0, j)),
        ],
        out_specs=pl.BlockSpec((r, tn), lambda j: (0, j)),
        compiler_params=_cp("parallel"),
        name="adaln",
    )(cond, w_mod, b_mod.reshape(1, n))


_ACTIVATIONS = {None: lambda v: v, "gelu": jax.nn.gelu, "sigmoid": _sigmoid}


def _inproj_kernel(x_ref, sh_ref, sc_ref, g_ref, w_ref, *o_refs, splits, acts):
    h = (_rms(x_ref[...], g_ref[...]) * (1.0 + sc_ref[...]) + sh_ref[...]).astype(BF16)
    off = 0
    for o_ref, width, act in zip(o_refs, splits, acts):
        y = _ACTIVATIONS[act](jnp.dot(h, w_ref[:, off:off + width], preferred_element_type=F32))
        o_ref[...] = y.astype(o_ref.dtype)
        off += width


def _inproj(x, shift, scale, gain, w, splits, acts, mod_row):
    b, t, d = x.shape
    tm = min(t, 512)
    n = w.shape[1]
    if mod_row is None:
        mod_map = lambda bi, i: (bi, 0, 0)
    else:
        mod_map = lambda bi, i: (mod_row, 0, 0)
    return pl.pallas_call(
        functools.partial(_inproj_kernel, splits=splits, acts=acts),
        out_shape=[jax.ShapeDtypeStruct((b, t, s), BF16) for s in splits],
        grid=(b, t // tm),
        in_specs=[
            pl.BlockSpec((None, tm, d), lambda bi, i: (bi, i, 0)),
            pl.BlockSpec((None, 1, d), mod_map),
            pl.BlockSpec((None, 1, d), mod_map),
            pl.BlockSpec((1, d), lambda bi, i: (0, 0)),
            pl.BlockSpec((d, n), lambda bi, i: (0, 0)),
        ],
        out_specs=[pl.BlockSpec((None, tm, s), lambda bi, i: (bi, i, 0)) for s in splits],
        compiler_params=_cp("parallel", "parallel"),
        name="inproj",
    )(x, shift, scale, gain.reshape(1, d), w)


_SHIFTED_TAPS = tuple(k for k in range(CONV_W) if k != CONV_W // 2)


def _lru_kernel(x_ref, xp_ref, xn_ref, h0_ref, smat_ref, cw_ref, cb_ref, wab_ref, ba_ref, bx_ref, lam_ref,
                y_ref, hfin_ref, u_scr, a_scr, b_scr, h_scr, st_scr, *, tc, nchunks, nb):
    d = pl.program_id(1)
    j = pl.program_id(2)
    c = jnp.where(d == 0, j, nchunks - 1 - j)
    w = x_ref.shape[-1]
    cwide = 2 * LANES

    @pl.when(j == 0)
    def _():
        for cb in range(LRU_BLOCKS):
            h0 = h0_ref[:, cb * LANES:(cb + 1) * LANES]
            st_scr[cb] = jnp.broadcast_to(h0[:, None, :], (nb, SUBLANES, LANES))

    zero = jnp.zeros((), BF16)
    c_lam = -LRU_C * jax.nn.softplus(-lam_ref[...])
    for cc in range(w // cwide):
        sl2 = slice(cc * cwide, (cc + 1) * cwide)
        for bb in range(nb):
            xe = jnp.concatenate([jnp.where(c > 0, xp_ref[bb, :, sl2], zero), x_ref[bb, :, sl2],
                                  jnp.where(c < nchunks - 1, xn_ref[bb, :, sl2], zero)], axis=0)
            sh = jnp.dot(smat_ref[...], xe, preferred_element_type=F32)
            u = cb_ref[:, sl2]
            for k in range(CONV_W):
                if k in _SHIFTED_TAPS:
                    i = _SHIFTED_TAPS.index(k)
                    tap = sh[i * tc:(i + 1) * tc]
                else:
                    tap = x_ref[bb, :, sl2].astype(F32)
                u = u + tap * cw_ref[k:k + 1, sl2]
            u_scr[bb, :, sl2] = u
        for cb in range(cc * (cwide // LANES), (cc + 1) * (cwide // LANES)):
            sl = slice(cb * LANES, (cb + 1) * LANES)
            u = u_scr[:, :, sl].reshape(nb * tc, LANES)
            gates = jnp.dot(u.astype(BF16), wab_ref[cb], preferred_element_type=F32)
            r = _sigmoid(gates[:, :LANES] + ba_ref[:, sl])
            i = _sigmoid(gates[:, LANES:] + bx_ref[:, sl])
            log_a = c_lam[:, sl] * r
            a = jnp.exp(log_a)
            a_scr[cb] = a
            m = -jnp.tanh(log_a) * (a * a + 1.0)
            b_scr[cb] = jnp.where(m > 0.0, m * lax.rsqrt(m), 0.0) * (i * u)

    @pl.when(d == 0)
    def _():
        _block_scan(a_scr, b_scr, h_scr, st_scr, False, nb, tc)

    @pl.when(d == 1)
    def _():
        _block_scan(a_scr, b_scr, h_scr, st_scr, True, nb, tc)

    for cb in range(LRU_BLOCKS):
        y_ref[:, :, cb * LANES:(cb + 1) * LANES] = h_scr[cb].astype(y_ref.dtype)

    @pl.when(j == nchunks - 1)
    def _():
        for cb in range(LRU_BLOCKS):
            hfin_ref[:, cb * LANES:(cb + 1) * LANES] = st_scr[cb][:, 0, :]


def _block_scan(a_scr, b_scr, h_scr, st_scr, rev, nb, tc):
    nblk = tc // SUBLANES
    row = lax.broadcasted_iota(jnp.int32, (1, SUBLANES, LANES), 1)
    for cb in range(LRU_BLOCKS):
        a = a_scr[cb].reshape(nb * nblk, SUBLANES, LANES)
        b = b_scr[cb].reshape(nb * nblk, SUBLANES, LANES)
        for s in (1, 2, 4):
            shift, keep = (SUBLANES - s, row < SUBLANES - s) if rev else (s, row >= s)
            b = b + a * jnp.where(keep, pltpu.roll(b, shift, 1), 0.0)
            a = a * jnp.where(keep, pltpu.roll(a, shift, 1), 1.0)
        a = a.reshape(nb, nblk, SUBLANES, LANES)
        b = b.reshape(nb, nblk, SUBLANES, LANES)
        carry = st_scr[cb]
        last = 0 if rev else SUBLANES - 1
        for k in (reversed(range(nblk)) if rev else range(nblk)):
            h = a[:, k] * carry + b[:, k]
            h_scr[cb, :, k * SUBLANES:(k + 1) * SUBLANES, :] = h
            carry = jnp.broadcast_to(h[:, last:last + 1, :], h.shape)
        st_scr[cb] = carry


def _lru(x, h0, lp):
    b, t, w = x.shape
    nb = SUBLANES
    tc = min(t, 128)
    nchunks = t // tc
    hb = tc // BF16_ROWS
    nhb = t // BF16_ROWS

    def chunk(di, j):
        return jnp.where(di == 0, j, nchunks - 1 - j)

    smat = np.zeros((len(_SHIFTED_TAPS) * tc, tc + 2 * BF16_ROWS), np.float32)
    for i, k in enumerate(_SHIFTED_TAPS):
        smat[i * tc + np.arange(tc), BF16_ROWS + np.arange(tc) + k - CONV_W // 2] = 1.0

    vec = lambda g, di, j: (0, 0)
    dvec = lambda g, di, j: (di, 0, 0)
    return pl.pallas_call(
        functools.partial(_lru_kernel, tc=tc, nchunks=nchunks, nb=nb),
        out_shape=[jax.ShapeDtypeStruct((2, b, t, w), BF16), jax.ShapeDtypeStruct((2, b, w), F32)],
        grid=(b // nb, 2, nchunks),
        in_specs=[
            pl.BlockSpec((nb, tc, w), lambda g, di, j: (g, chunk(di, j), 0)),
            pl.BlockSpec((nb, BF16_ROWS, w), lambda g, di, j: (g, jnp.maximum(chunk(di, j) * hb - 1, 0), 0)),
            pl.BlockSpec((nb, BF16_ROWS, w), lambda g, di, j: (g, jnp.minimum((chunk(di, j) + 1) * hb, nhb - 1), 0)),
            pl.BlockSpec((None, nb, w), lambda g, di, j: (di, g, 0)),
            pl.BlockSpec(smat.shape, vec),
            pl.BlockSpec((CONV_W, w), vec),
            pl.BlockSpec((1, w), vec),
            pl.BlockSpec((None, LRU_BLOCKS, LANES, 2 * LANES), lambda g, di, j: (di, 0, 0, 0)),
            pl.BlockSpec((None, 1, w), dvec),
            pl.BlockSpec((None, 1, w), dvec),
            pl.BlockSpec((None, 1, w), dvec),
        ],
        out_specs=[
            pl.BlockSpec((None, nb, tc, w), lambda g, di, j: (di, g, chunk(di, j), 0)),
            pl.BlockSpec((None, nb, w), lambda g, di, j: (di, g, 0)),
        ],
        scratch_shapes=[
            pltpu.VMEM((nb, tc, w), F32),
            pltpu.VMEM((LRU_BLOCKS, nb * tc, LANES), F32),
            pltpu.VMEM((LRU_BLOCKS, nb * tc, LANES), F32),
            pltpu.VMEM((LRU_BLOCKS, nb, tc, LANES), F32),
            pltpu.VMEM((LRU_BLOCKS, nb, SUBLANES, LANES), F32),
        ],
        compiler_params=_cp("parallel", "arbitrary", "arbitrary"),
        name="rglru",
    )(x, x, x, h0, jnp.asarray(smat, BF16), lp["conv_w"], lp["conv_b"], lp["wab"], lp["ba"], lp["bx"], lp["lam"])


def _pair_rows(x2):
    low = lax.broadcasted_iota(jnp.int32, x2.shape, 1) < HEAD_DIM
    zero = jnp.zeros((), x2.dtype)
    return jnp.concatenate([jnp.where(low, x2, zero), jnp.where(low, zero, x2)], axis=0)


def _softmax_strips(s_ref, p_ref):
    sums = []
    for r in range(s_ref.shape[0] // BF16_ROWS):
        rows = slice(r * BF16_ROWS, (r + 1) * BF16_ROWS)
        s = s_ref[rows, :]
        p = jnp.exp(s - jnp.max(s, axis=-1, keepdims=True))
        sums.append(jnp.sum(p, axis=-1, keepdims=True))
        p_ref[rows, :] = p.astype(BF16)
    return sums


def _normalise_strips(o, sums):
    return jnp.concatenate([o[r * BF16_ROWS:(r + 1) * BF16_ROWS] / l for r, l in enumerate(sums)], axis=0)


def _na_kernel(var_ref, q_ref, k_ref, v_ref, kc_ref, vc_ref, bias_ref, o_ref, s_scr, p_scr,
               *, band, win_rows, kh, rows):
    del var_ref
    r0 = pl.program_id(1) * band
    ws = jnp.clip(r0 - kh // 2, 0, rows - win_rows)
    start = pl.multiple_of(ws * GRID_W, GRID_W)
    nwin = win_rows * GRID_W
    m = band * GRID_W
    npairs = q_ref.shape[-1] // LANES
    low = lax.broadcasted_iota(jnp.int32, (m, LANES), 1) < HEAD_DIM

    def scores(hp):
        sl = slice(hp * LANES, (hp + 1) * LANES)
        qs = _pair_rows(q_ref[:, sl] * jnp.asarray(HEAD_DIM ** -0.5, BF16))
        s_lat = lax.dot_general(qs, k_ref[pl.ds(start, nwin), sl], _NT, preferred_element_type=F32)
        s_scr[hp % 2, :, 0:nwin] = s_lat + bias_ref[2 * hp:2 * hp + 2].reshape(2 * m, nwin)
        s_scr[hp % 2, :, nwin:] = lax.dot_general(qs, kc_ref[:, sl], _NT, preferred_element_type=F32)

    def finish(hp):
        sl = slice(hp * LANES, (hp + 1) * LANES)
        sums = _softmax_strips(s_scr.at[hp % 2], p_scr.at[hp % 2])
        o = jnp.dot(p_scr[hp % 2, :, 0:nwin], v_ref[pl.ds(start, nwin), sl], preferred_element_type=F32)
        o = o + jnp.dot(p_scr[hp % 2, :, nwin:], vc_ref[:, sl], preferred_element_type=F32)
        o = _normalise_strips(o, sums)
        o_ref[:, sl] = jnp.where(low, o[:m], o[m:]).astype(o_ref.dtype)

    scores(0)
    for hp in range(npairs):
        if hp + 1 < npairs:
            scores(hp + 1)
        finish(hp)


def _na_bands(rows, band, win_rows):
    kh = min(NA_KH_MAX, rows)
    variants, var_of_band = [], []
    for r0 in range(0, rows, band):
        ws = int(np.clip(r0 - kh // 2, 0, rows - win_rows))
        key = (r0 - ws, tuple(int(np.clip(r0 + rr - kh // 2, 0, rows - kh)) - ws for rr in range(band)))
        if key not in variants:
            variants.append(key)
        var_of_band.append(variants.index(key))
    return variants, np.asarray(var_of_band, np.int32)


def _na_bias_table(rpb, rows, band, win_rows):
    kh = min(NA_KH_MAX, rows)
    col = np.arange(GRID_W)
    cs = np.clip(col - NA_KW // 2, 0, GRID_W - NA_KW)
    col_mask = (col[None, :] >= cs[:, None]) & (col[None, :] < cs[:, None] + NA_KW)
    dcol = np.clip(col[None, :] - col[:, None], -(NA_KW - 1), NA_KW - 1) + NA_KW - 1
    heads, n_drow, n_dcol = rpb.shape
    onehot = jnp.asarray(dcol.reshape(-1)[None, :] == np.arange(n_dcol)[:, None], F32)
    t2 = jnp.dot(rpb.reshape(heads * n_drow, n_dcol).astype(F32), onehot, precision=lax.Precision.HIGHEST)
    t2 = t2.reshape(heads, n_drow, GRID_W, GRID_W)
    t2 = jnp.where(col_mask[None, None], t2, NEG_INF)
    masked = jnp.full((heads, GRID_W, GRID_W), NEG_INF, F32)
    variants, _ = _na_bands(rows, band, win_rows)
    tabs = []
    for delta, first in variants:
        q_rows = []
        for rr in range(band):
            blocks = [t2[:, j - delta - rr + NA_KH_MAX - 1] if first[rr] <= j < first[rr] + kh else masked
                      for j in range(win_rows)]
            q_rows.append(jnp.concatenate(blocks, axis=-1))
        tabs.append(jnp.concatenate(q_rows, axis=1))
    return jnp.stack(tabs)


def _na(q, k, v, kc, vc, rpb):
    b, s, w = q.shape
    l = kc.shape[1]
    heads = w // HEAD_DIM
    rows = s // GRID_W
    kh = min(NA_KH_MAX, rows)
    band = min(4, rows)
    win_rows = min(-(-(kh + band - 1) // 2) * 2, rows)
    m, nwin = band * GRID_W, win_rows * GRID_W
    bias = _na_bias_table(rpb, rows, band, win_rows)
    _, var_of_band = _na_bands(rows, band, win_rows)

    full = lambda bi, r, var: (bi, 0, 0)
    grid_spec = pltpu.PrefetchScalarGridSpec(
        num_scalar_prefetch=1,
        grid=(b, rows // band),
        in_specs=[
            pl.BlockSpec((None, m, w), lambda bi, r, var: (bi, r, 0)),
            pl.BlockSpec((None, s, w), full),
            pl.BlockSpec((None, s, w), full),
            pl.BlockSpec((None, l, w), full),
            pl.BlockSpec((None, l, w), full),
            pl.BlockSpec((None, heads, m, nwin), lambda bi, r, var: (var[r], 0, 0, 0)),
        ],
        out_specs=pl.BlockSpec((None, m, w), lambda bi, r, var: (bi, r, 0)),
        scratch_shapes=[pltpu.VMEM((2, 2 * m, nwin + l), F32), pltpu.VMEM((2, 2 * m, nwin + l), BF16)],
    )
    return pl.pallas_call(
        functools.partial(_na_kernel, band=band, win_rows=win_rows, kh=kh, rows=rows),
        out_shape=jax.ShapeDtypeStruct((b, s, w), BF16),
        grid_spec=grid_spec,
        compiler_params=_cp("parallel", "arbitrary"),
        name="natten",
    )(jnp.asarray(var_of_band), q, k, v, kc, vc, bias)


def _head_norm(x, g, gmat):
    cols = []
    for c in range(x.shape[1] // LANES):
        xc = x[:, c * LANES:(c + 1) * LANES]
        x2 = xc * xc
        hi = x2.astype(BF16)
        lo = (x2 - hi.astype(F32)).astype(BF16)
        ms = jnp.dot(hi, gmat, preferred_element_type=F32) + jnp.dot(lo, gmat, preferred_element_type=F32)
        cols.append(xc * lax.rsqrt(ms + EPS) * g)
    return cols[0] if len(cols) == 1 else jnp.concatenate(cols, axis=-1)


def _rope(x, cos, sin):
    half = HEAD_DIM // 2
    lane = lax.broadcasted_iota(jnp.int32, (x.shape[0], LANES), 1)
    low = (lane % HEAD_DIM) < half
    cols = []
    for c in range(x.shape[1] // LANES):
        xc = x[:, c * LANES:(c + 1) * LANES]
        sw = jnp.where(low, pltpu.roll(xc, LANES - half, 1), pltpu.roll(xc, half, 1))
        cols.append(xc * cos + sw * sin)
    return cols[0] if len(cols) == 1 else jnp.concatenate(cols, axis=-1)


def _attn_kernel(*refs, norm, group, tq):
    if norm:
        q_ref, k_ref, v_ref, qg_ref, kg_ref, gmat_ref, o_ref = refs
    else:
        q_ref, k_ref, v_ref, o_ref = refs
    q = q_ref[...].astype(F32)
    k = k_ref[...]
    if norm:
        q = _head_norm(q, qg_ref[...], gmat_ref[...])
        k = _head_norm(k.astype(F32), kg_ref[...], gmat_ref[...]).astype(BF16)
    q = (q * (HEAD_DIM ** -0.5)).astype(BF16)

    outs = []
    for g in range(k_ref.shape[1] // HEAD_DIM):
        sl = slice(g * HEAD_DIM, (g + 1) * HEAD_DIM)
        qs = [q[:, (g * group + jj) * HEAD_DIM:(g * group + jj + 1) * HEAD_DIM] for jj in range(group)]
        qs = qs[0] if group == 1 else jnp.concatenate(qs, axis=0)
        s = lax.dot_general(qs, k[:, sl], _NT, preferred_element_type=F32)
        p = jnp.exp(s - jnp.max(s, axis=-1, keepdims=True))
        o = jnp.dot(p.astype(BF16), v_ref[:, sl], preferred_element_type=F32) / jnp.sum(p, axis=-1, keepdims=True)
        for jj in range(group):
            outs.append(o[jj * tq:(jj + 1) * tq])
    o_ref[...] = jnp.concatenate(outs, axis=-1).astype(o_ref.dtype)


def _dup_heads(x):
    low = lax.broadcasted_iota(jnp.int32, x.shape, 1) < HEAD_DIM
    sw = pltpu.roll(x, HEAD_DIM, 1)
    return jnp.where(low, x, sw), jnp.where(low, sw, x)


def _gqa_kernel(q_ref, kc_ref, vc_ref, kl_ref, vl_ref, qg_ref, kg_ref, gmat_ref, cosq_ref, sinq_ref, cosk_ref,
                sink_ref, o_ref, kc_scr, kl_scr, vc_scr, vl_scr, s_scr, p_scr, *, sub):
    n_ctx = kc_ref.shape[0]
    n_kv = GQA_KV_HEADS

    @pl.when(pl.program_id(1) == 0)
    def _():
        gmat = gmat_ref[...]
        kc = _head_norm(kc_ref[...].astype(F32), kg_ref[...], gmat)
        kl = _rope(_head_norm(kl_ref[...].astype(F32), kg_ref[...], gmat), cosk_ref[...], sink_ref[...])
        for scr, val in ((kc_scr, kc), (kl_scr, kl), (vc_scr, vc_ref[...].astype(F32)),
                         (vl_scr, vl_ref[...].astype(F32))):
            for g, dup in enumerate(_dup_heads(val)):
                scr[g] = dup.astype(BF16)

    tq = q_ref.shape[0]
    chunks = []
    for c in range(q_ref.shape[1] // LANES):
        qc = q_ref[:, c * LANES:(c + 1) * LANES].astype(F32)
        qc = _rope(_head_norm(qc, qg_ref[...], gmat_ref[...]), cosq_ref[...], sinq_ref[...])
        chunks.append((qc * (HEAD_DIM ** -0.5)).astype(BF16))
    pairs_per_group = len(chunks) // n_kv
    low = lax.broadcasted_iota(jnp.int32, (sub, LANES), 1) < HEAD_DIM
    units = [(t, g) for t in range(tq // sub) for g in range(n_kv)]

    def scores(u):
        t, g = units[u]
        rows = [_pair_rows(chunks[g * pairs_per_group + j][t * sub:(t + 1) * sub]) for j in range(pairs_per_group)]
        qs = jnp.concatenate(rows, axis=0)
        s_scr[u % 2, :, 0:n_ctx] = lax.dot_general(qs, kc_scr[g], _NT, preferred_element_type=F32)
        s_scr[u % 2, :, n_ctx:] = lax.dot_general(qs, kl_scr[g], _NT, preferred_element_type=F32)

    def finish(u):
        t, g = units[u]
        sums = _softmax_strips(s_scr.at[u % 2], p_scr.at[u % 2])
        o = jnp.dot(p_scr[u % 2, :, 0:n_ctx], vc_scr[g], preferred_element_type=F32)
        o = o + jnp.dot(p_scr[u % 2, :, n_ctx:], vl_scr[g], preferred_element_type=F32)
        o = _normalise_strips(o, sums)
        for j in range(pairs_per_group):
            c = g * pairs_per_group + j
            lo, hi = o[2 * j * sub:(2 * j + 1) * sub], o[(2 * j + 1) * sub:(2 * j + 2) * sub]
            o_ref[t * sub:(t + 1) * sub, c * LANES:(c + 1) * LANES] = jnp.where(low, lo, hi).astype(o_ref.dtype)

    scores(0)
    for u in range(len(units)):
        if u + 1 < len(units):
            scores(u + 1)
        finish(u)


def _gqa(q, kc, vc, kl, vl, qg, kg):
    b, t, wq = q.shape
    l, wk = kc.shape[1], kc.shape[2]
    s = kl.shape[1]
    assert wk == GQA_KV_HEADS * HEAD_DIM == LANES
    tq = min(t, 512)
    sub = min(tq, 256)
    m_unit = 2 * (wq // LANES // GQA_KV_HEADS) * sub
    reps = LANES // HEAD_DIM
    gmat = _head_mean_matrix()
    cos, sin = _rope_tables(t)
    full = lambda bi, i: (bi, 0, 0)
    const = lambda bi, i: (0, 0)
    return pl.pallas_call(
        functools.partial(_gqa_kernel, sub=sub),
        out_shape=jax.ShapeDtypeStruct((b, t, wq), BF16),
        grid=(b, t // tq),
        in_specs=[
            pl.BlockSpec((None, tq, wq), lambda bi, i: (bi, i, 0)),
            pl.BlockSpec((None, l, wk), full), pl.BlockSpec((None, l, wk), full),
            pl.BlockSpec((None, s, wk), full), pl.BlockSpec((None, s, wk), full),
            pl.BlockSpec((1, LANES), const), pl.BlockSpec((1, LANES), const), pl.BlockSpec((LANES, LANES), const),
            pl.BlockSpec((tq, LANES), lambda bi, i: (i, 0)), pl.BlockSpec((tq, LANES), lambda bi, i: (i, 0)),
            pl.BlockSpec((s, LANES), const), pl.BlockSpec((s, LANES), const),
        ],
        out_specs=pl.BlockSpec((None, tq, wq), lambda bi, i: (bi, i, 0)),
        scratch_shapes=[
            pltpu.VMEM((GQA_KV_HEADS, l, LANES), BF16), pltpu.VMEM((GQA_KV_HEADS, s, LANES), BF16),
            pltpu.VMEM((GQA_KV_HEADS, l, LANES), BF16), pltpu.VMEM((GQA_KV_HEADS, s, LANES), BF16),
            pltpu.VMEM((2, m_unit, l + s), F32), pltpu.VMEM((2, m_unit, l + s), BF16),
        ],
        compiler_params=_cp("parallel", "arbitrary"),
        name="gqa",
    )(q, kc, vc, kl, vl, jnp.tile(qg, reps).reshape(1, LANES), jnp.tile(kg, reps).reshape(1, LANES), gmat,
      cos, sin, cos, sin)


def _rope_tables(n_tokens):
    t = np.arange(n_tokens)
    pos = np.stack([t // GRID_W, t % GRID_W], axis=-1).astype(np.float32)
    n_freq = HEAD_DIM // 4
    inv_freq = jnp.asarray(ROPE_BASE, F32) ** (-jnp.arange(n_freq, dtype=F32) / n_freq)
    ang = (jnp.asarray(pos)[:, :, None] * inv_freq).reshape(n_tokens, 2 * n_freq)
    cos, sin = jnp.cos(ang), jnp.sin(ang)
    reps = LANES // HEAD_DIM
    return jnp.tile(jnp.concatenate([cos, cos], -1), (1, reps)), jnp.tile(jnp.concatenate([-sin, sin], -1), (1, reps))


def _head_mean_matrix():
    blk = np.arange(LANES) // HEAD_DIM
    return jnp.asarray((blk[:, None] == blk[None, :]).astype(np.float32) / HEAD_DIM, BF16)


def _attn(q, k, v, qg=None, kg=None):
    b, t, wq = q.shape
    wk = k.shape[2]
    norm = qg is not None
    tok = lambda w: pl.BlockSpec((None, t, w), lambda bi: (bi, 0, 0))
    args = [q, k, v]
    in_specs = [tok(wq), tok(wk), tok(wk)]
    if norm:
        reps = LANES // HEAD_DIM
        args += [jnp.tile(qg, reps).reshape(1, LANES), jnp.tile(kg, reps).reshape(1, LANES), _head_mean_matrix()]
        in_specs += [pl.BlockSpec((1, LANES), lambda bi: (0, 0))] * 2
        in_specs += [pl.BlockSpec((LANES, LANES), lambda bi: (0, 0))]
    return pl.pallas_call(
        functools.partial(_attn_kernel, norm=norm, group=wq // wk, tq=t),
        out_shape=jax.ShapeDtypeStruct((b, t, wq), BF16),
        grid=(b,),
        in_specs=in_specs,
        out_specs=tok(wq),
        compiler_params=_cp("parallel"),
        name="attn",
    )(*args)


def _merge_kernel(hf_ref, hr_ref, gl_ref, yb_ref, yc_ref, gt_ref, x_ref, g1_ref, sh2_ref, sc2_ref, ng1_ref, ng2_ref,
                  woa_ref, wob_ref, woc_ref, wout_ref, x1_ref, h2_ref, h2p_ref):
    d = x_ref.shape[-1]
    ya = (hf_ref[...].astype(F32) + hr_ref[...].astype(F32)) * gl_ref[...].astype(F32)
    pa = jnp.dot(ya.astype(BF16), woa_ref[...], preferred_element_type=F32)
    pb = jnp.dot(yb_ref[...], wob_ref[...], preferred_element_type=F32)
    pc = jnp.dot(yc_ref[...], woc_ref[...], preferred_element_type=F32)
    ga = gt_ref[:, 0:d].astype(F32)
    gb = gt_ref[:, d:2 * d].astype(F32)
    gc = gt_ref[:, 2 * d:3 * d].astype(F32)
    m = ga * pa + gb * pb + gc * pc
    y = jnp.dot(m.astype(BF16), wout_ref[...], preferred_element_type=F32)
    x1 = x_ref[...] + g1_ref[...] * _rms(y, ng1_ref[...])
    x1_ref[...] = x1
    h2 = (_rms(x1, ng2_ref[...]) * (1.0 + sc2_ref[...]) + sh2_ref[...]).astype(BF16)
    h2_ref[...] = h2
    h2p_ref[...] = _pack_halves(h2)


def _pack_halves(xb):
    half = xb.shape[1] // 2
    bits = pltpu.bitcast(xb.astype(F32), jnp.uint32)
    return bits[:, :half] | (bits[:, half:] >> 16)


def _unpack_halves(words):
    hi = pltpu.bitcast(words & jnp.uint32(0xFFFF0000), F32)
    lo = pltpu.bitcast(words << 16, F32)
    return jnp.concatenate([hi, lo], axis=1).astype(BF16)


def _merge(h, gl, yb, yc, gates, x, g1, sh2, sc2, ng1, ng2, woa, wob, woc, wout, mod_row):
    b, t, d = x.shape
    tm = min(t, 512)
    wl, wb, wc = gl.shape[2], yb.shape[2], yc.shape[2]
    if mod_row is None:
        mod_map = lambda bi, i: (bi, 0, 0)
    else:
        mod_map = lambda bi, i: (mod_row, 0, 0)
    tok = lambda w: pl.BlockSpec((None, tm, w), lambda bi, i: (bi, i, 0))
    mod = pl.BlockSpec((None, 1, d), mod_map)
    const = lambda r, c: pl.BlockSpec((r, c), lambda bi, i: (0, 0))
    return pl.pallas_call(
        _merge_kernel,
        out_shape=[jax.ShapeDtypeStruct((b, t, d), F32), jax.ShapeDtypeStruct((b, t, d), BF16),
                   jax.ShapeDtypeStruct((b, t, d // 2), jnp.uint32)],
        grid=(b, t // tm),
        in_specs=[
            pl.BlockSpec((None, None, tm, wl), lambda bi, i: (0, bi, i, 0)),
            pl.BlockSpec((None, None, tm, wl), lambda bi, i: (1, bi, i, 0)),
            tok(wl), tok(wb), tok(wc), tok(N_BRANCH * d), tok(d),
            mod, mod, mod, const(1, d), const(1, d),
            const(wl, d), const(wb, d), const(wc, d), const(d, d),
        ],
        out_specs=[tok(d), tok(d), tok(d // 2)],
        compiler_params=_cp("parallel", "parallel"),
        name="merge",
    )(h, h, gl, yb, yc, gates, x, g1, sh2, sc2, ng1.reshape(1, d), ng2.reshape(1, d), woa, wob, woc, wout)


def _lane_cumsum(x, tri):
    e, n = x.shape
    carry = jnp.zeros((e, 1), F32)
    cols = []
    for c in range(n // LANES):
        part = jnp.dot(x[:, c * LANES:(c + 1) * LANES].astype(BF16), tri, preferred_element_type=F32) + carry
        cols.append(part)
        carry = part[:, LANES - 1:LANES]
    return cols[0] if len(cols) == 1 else jnp.concatenate(cols, axis=-1)


def _route_kernel(h_ref, wr_ref, tri_ref, tok_ref, idx_ref, gcol_ref, pos_scr, aff_scr, *, cap):
    nbs, n_exp = h_ref.shape[0], wr_ref.shape[0]
    affs = []
    for s in range(nbs):
        logits = lax.dot_general(wr_ref[...], h_ref[s], _NT, preferred_element_type=F32)
        z = jnp.exp(logits - jnp.max(logits, axis=0, keepdims=True))
        affs.append(z / jnp.sum(z, axis=0, keepdims=True))
    aff = affs[0] if nbs == 1 else jnp.concatenate(affs, axis=0)
    aff_scr[...] = aff
    bits = pltpu.bitcast(aff, jnp.int32)
    e, n = bits.shape

    def bisect(_, lohi):
        lo, hi = lohi
        mid = lo + ((hi - lo + 1) >> 1)
        cnt = jnp.sum(jnp.where(bits >= mid, 1.0, 0.0), axis=1, keepdims=True)
        ok = cnt >= cap
        return jnp.where(ok, mid, lo), jnp.where(ok, hi, mid - 1)

    one_bits = 0x3F800000
    lo, _ = lax.fori_loop(0, 31, bisect, (jnp.zeros((e, 1), jnp.int32), jnp.full((e, 1), one_bits, jnp.int32)))
    gt = jnp.where(bits > lo, 1.0, 0.0)
    eq = jnp.where(bits == lo, 1.0, 0.0)
    need = cap - jnp.sum(gt, axis=1, keepdims=True)
    tri = tri_ref[...]
    sel = gt + jnp.where(_lane_cumsum(eq, tri) <= need, eq, 0.0)
    pos = _lane_cumsum(sel, tri) - 1.0
    pos_scr[...] = jnp.where(sel > 0.0, pos, -1.0).astype(jnp.int32)

    slot = lax.broadcasted_iota(jnp.int32, (cap, n), 0)

    def compact(r, carry):
        hit = pos_scr[pl.ds(r, 1), :] == slot
        onehot = jnp.where(hit, 1.0, 0.0).astype(BF16)
        digits = lax.dot_general(tok_ref[...], onehot, _NT, preferred_element_type=F32)
        s, ex = r // n_exp, r % n_exp
        idx_ref[s, pl.ds(ex, 1), :] = (digits[0:1] * GRID_W + digits[1:2]).astype(jnp.int32)
        gcol_ref[s, ex] = jnp.sum(jnp.where(hit, aff_scr[pl.ds(r, 1), :], 0.0), axis=1, keepdims=True)
        return carry

    lax.fori_loop(0, e, compact, 0, unroll=2)


def _route(h2, w_router_t, cap):
    b, n, d = h2.shape
    e = w_router_t.shape[0]
    nbs = 4 if b % 4 == 0 else 1
    tri = jnp.asarray(np.triu(np.ones((LANES, LANES), np.float32)), BF16)
    t = np.arange(n)
    tok = np.zeros((SUBLANES, n), np.float32)
    tok[0], tok[1] = t // GRID_W, t % GRID_W
    return pl.pallas_call(
        functools.partial(_route_kernel, cap=cap),
        out_shape=[jax.ShapeDtypeStruct((b, e, cap), jnp.int32), jax.ShapeDtypeStruct((b, e, cap, 1), F32)],
        grid=(b // nbs,),
        in_specs=[
            pl.BlockSpec((nbs, n, d), lambda bi: (bi, 0, 0)),
            pl.BlockSpec((e, d), lambda bi: (0, 0)),
            pl.BlockSpec((LANES, LANES), lambda bi: (0, 0)),
            pl.BlockSpec((SUBLANES, n), lambda bi: (0, 0)),
        ],
        out_specs=[pl.BlockSpec((nbs, e, cap), lambda bi: (bi, 0, 0)),
                   pl.BlockSpec((nbs, e, cap, 1), lambda bi: (bi, 0, 0, 0))],
        scratch_shapes=[pltpu.VMEM((nbs * e, n), jnp.int32), pltpu.VMEM((nbs * e, n), F32)],
        compiler_params=_cp("parallel"),
        name="route",
    )(h2, w_router_t, tri, jnp.asarray(tok, BF16))


def _expert_kernel(idx_smem, gcol_ref, h_ref, wg_ref, wu_ref, wd_ref, x_hbm, g2_ref, ng_ref, o_hbm,
                   o_ref, hs_scr, y_scr, xbuf, sem, osem, *, cap, nbs, n_exp, per_sample_gate):
    e = pl.program_id(1)
    n = h_ref.shape[1]

    def gather(expert, buf):
        for s in range(nbs):
            for j in range(cap):
                t = idx_smem[s, expert, j]
                hs_scr[buf, pl.ds(s * cap + j, 1), :] = h_ref[s, pl.ds(t, 1), :]

    def scatter_add(expert, buf):
        group = 8
        for s in range(nbs):
            for j0 in range(0, cap, group):
                ts = [idx_smem[s, expert, j0 + g] for g in range(group)]
                acc = [o_ref[s, pl.ds(t, 1), :] for t in ts]
                for g, t in enumerate(ts):
                    o_ref[s, pl.ds(t, 1), :] = acc[g] + y_scr[buf, pl.ds(s * cap + j0 + g, 1), :]

    @pl.when(e == 0)
    def _():
        o_ref[...] = jnp.zeros_like(o_ref)
        y_scr[1] = jnp.zeros(y_scr.shape[1:], F32)
        gather(0, 0)

    def step(cur):
        hs = _unpack_halves(hs_scr[cur])
        scatter_add(jnp.maximum(e - 1, 0), 1 - cur)
        gather(jnp.minimum(e + 1, n_exp - 1), 1 - cur)
        gcol = gcol_ref[...].reshape(nbs * cap, 1)
        a = jnp.dot(hs, wg_ref[...], preferred_element_type=F32)
        u = (a * _sigmoid(a)) * jnp.dot(hs, wu_ref[...], preferred_element_type=F32)
        y_scr[cur] = jnp.dot(u.astype(BF16), wd_ref[...], preferred_element_type=F32) * gcol

    for cur in range(2):
        pl.when(e % 2 == cur)(functools.partial(step, cur))

    rc = xbuf.shape[1]
    chunks = [(s, c) for s in range(nbs) for c in range(n // rc)]
    first = pl.program_id(0) * nbs

    def x_copy(k):
        s, c = chunks[k]
        return pltpu.make_async_copy(x_hbm.at[first + s, pl.ds(c * rc, rc), :], xbuf.at[k % 2], sem.at[k % 2])

    def o_copy(k):
        s, c = chunks[k]
        return pltpu.make_async_copy(o_ref.at[s, pl.ds(c * rc, rc), :], o_hbm.at[first + s, pl.ds(c * rc, rc), :],
                                     osem.at[k])

    @pl.when(e == n_exp - 1)
    def _():
        x_copy(0).start()
        scatter_add(n_exp - 1, (n_exp - 1) % 2)
        for k, (s, c) in enumerate(chunks):
            if k + 1 < len(chunks):
                x_copy(k + 1).start()
            x_copy(k).wait()
            g2 = g2_ref[s if per_sample_gate else 0]
            sl = slice(c * rc, (c + 1) * rc)
            o_ref[s, sl, :] = xbuf[k % 2] + g2 * _rms(o_ref[s, sl, :], ng_ref[...])
            o_copy(k).start()
        for k in range(len(chunks)):
            o_copy(k).wait()


def _experts(h2p, idx, gcol, wg, wu, wd, cap, x1, g2, ng, mod_row):
    b, n, dh = h2p.shape
    e, d, f = wg.shape
    nbs = max(1, min(b, 512 // cap))
    rc = min(n, 256)
    sample = lambda bi, ei: (bi, 0, 0)
    if mod_row is None:
        g2_spec = pl.BlockSpec((nbs, 1, d), sample)
    else:
        g2_spec = pl.BlockSpec((1, 1, d), lambda bi, ei: (mod_row, 0, 0))
    return pl.pallas_call(
        functools.partial(_expert_kernel, cap=cap, nbs=nbs, n_exp=e, per_sample_gate=mod_row is None),
        out_shape=jax.ShapeDtypeStruct((b, n, d), F32),
        grid=(b // nbs, e),
        in_specs=[
            pl.BlockSpec((nbs, e, cap), sample, memory_space=pltpu.SMEM),
            pl.BlockSpec((nbs, None, cap, 1), lambda bi, ei: (bi, ei, 0, 0)),
            pl.BlockSpec((nbs, n, dh), sample),
            pl.BlockSpec((None, d, f), lambda bi, ei: (ei, 0, 0)),
            pl.BlockSpec((None, d, f), lambda bi, ei: (ei, 0, 0)),
            pl.BlockSpec((None, f, d), lambda bi, ei: (ei, 0, 0)),
            pl.BlockSpec(memory_space=pl.ANY),
            g2_spec,
            pl.BlockSpec((1, d), lambda bi, ei: (0, 0)),
        ],
        out_specs=pl.BlockSpec(memory_space=pl.ANY),
        scratch_shapes=[pltpu.VMEM((nbs, n, d), F32),
                        pltpu.VMEM((2, nbs * cap, dh), jnp.uint32), pltpu.VMEM((2, nbs * cap, d), F32),
                        pltpu.VMEM((2, rc, d), F32), pltpu.SemaphoreType.DMA((2,)),
                        pltpu.SemaphoreType.DMA((nbs * (n // rc),))],
        compiler_params=_cp("parallel", "arbitrary"),
        name="experts",
    )(idx, gcol, h2p, wg, wu, wd, x1, g2, ng.reshape(1, d))


def _split_cols(w, widths):
    offs = np.cumsum((0,) + tuple(widths))
    return [w[:, int(offs[i]):int(offs[i + 1])] for i in range(len(widths))]


def _layer(xl, xc, mods, p, need_ctx):
    bsz, s, d = xl.shape
    ctx_row = bsz
    sh1, sc1, g1, sh2, sc2, g2 = mods
    ng = p["norm_g"]
    na_w = (d // 128) * HEAD_DIM
    gq_w = (d // 128) * HEAD_DIM
    gkv_w = GQA_KV_HEADS * HEAD_DIM
    splits = (d, d, na_w, na_w, na_w, gq_w, gkv_w, gkv_w, N_BRANCH * d)
    acts = (None, "gelu", None, None, None, None, None, None, "sigmoid")
    w_in = p["w_in"].astype(BF16)

    lx, lg, nq, nk, nv, gq, gk, gv, gates = _inproj(xl, sh1, sc1, ng[0], w_in, splits, acts, None)
    if need_ctx:
        cx, cg, cnq, cnk, cnv, cgq, cgk, cgv, cgates = _inproj(xc, sh1, sc1, ng[0], w_in, splits, acts, ctx_row)
    else:
        parts = _split_cols(w_in, splits)
        kv_parts = (0, 3, 4, 6, 7)
        w_kv = jnp.concatenate([parts[i] for i in kv_parts], axis=1)
        cx, cnk, cnv, cgk, cgv = _inproj(xc, sh1, sc1, ng[0], w_kv, tuple(splits[i] for i in kv_parts),
                                         tuple(acts[i] for i in kv_parts), ctx_row)

    lp = {
        "conv_w": p["conv_w"], "conv_b": p["conv_b"].reshape(1, d),
        "wab": jnp.concatenate([p["lru_wa"], p["lru_wx"]], axis=-1).astype(BF16),
        "ba": p["lru_ba"].reshape(2, 1, d), "bx": p["lru_bx"].reshape(2, 1, d),
        "lam": p["lru_lam"].reshape(2, 1, d),
    }
    hc, hfin = _lru(cx, jnp.zeros((2, bsz, d), F32), lp)
    hl, _ = _lru(lx, hfin, lp)

    yb = _na(nq, nk, nv, cnk, cnv, p["na_rpb"])
    yc = _gqa(gq, cgk, cgv, gk, gv, p["qn_g"], p["kn_g"])

    woa, wob, woc, wout = (p[k].astype(BF16) for k in ("w_o_a", "w_o_b", "w_o_c", "w_out"))
    wr_t = p["w_router"].T.astype(BF16)
    wg, wu, wd = (p[k].astype(BF16) for k in ("w_gate", "w_up", "w_down"))

    def ffn(x1, h2, h2p, mod_row):
        n = x1.shape[1]
        cap = EC_CAPACITY * n // N_EXPERTS
        idx, gcol = _route(h2, wr_t, cap)
        return _experts(h2p, idx, gcol, wg, wu, wd, cap, x1, g2, ng[3], mod_row)

    xl = ffn(*_merge(hl, lg, yb, yc, gates, xl, g1, sh2, sc2, ng[1], ng[2], woa, wob, woc, wout, None), None)
    if need_ctx:
        ybc = _attn(cnq, cnk, cnv)
        ycc = _attn(cgq, cgk, cgv, qg=p["qn_g"], kg=p["kn_g"])
        xc = ffn(*_merge(hc, cg, ybc, ycc, cgates, xc, g1, sh2, sc2, ng[1], ng[2], woa, wob, woc, wout, ctx_row),
                 ctx_row)
    return xl, xc


def kernel(x, c, ctx, c_ctx, w_mod, b_mod, norm_g, w_in, conv_w, conv_b, lru_wa, lru_ba, lru_wx, lru_bx, lru_lam,
           na_rpb, qn_g, kn_g, w_o_a, w_o_b, w_o_c, w_out, w_router, w_gate, w_up, w_down):
    bsz, _, d = x.shape
    depth = w_mod.shape[0]
    rows = -(-(bsz + 1) // SUBLANES) * SUBLANES
    cond = jnp.zeros((rows, d), F32).at[:bsz].set(c).at[bsz].set(c_ctx)
    xl, xc = x, ctx
    for l in range(depth):
        p = {
            "norm_g": norm_g[l], "w_in": w_in[l], "conv_w": conv_w[l], "conv_b": conv_b[l],
            "lru_wa": lru_wa[l], "lru_ba": lru_ba[l], "lru_wx": lru_wx[l], "lru_bx": lru_bx[l],
            "lru_lam": lru_lam[l], "na_rpb": na_rpb[l], "qn_g": qn_g[l], "kn_g": kn_g[l],
            "w_o_a": w_o_a[l], "w_o_b": w_o_b[l], "w_o_c": w_o_c[l], "w_out": w_out[l],
            "w_router": w_router[l], "w_gate": w_gate[l], "w_up": w_up[l], "w_down": w_down[l],
        }
        m = _adaln(cond, w_mod[l], b_mod[l])
        mods = [m[:, i * d:(i + 1) * d].reshape(rows, 1, d) for i in range(6)]
        xl, xc = _layer(xl, xc, mods, p, l < depth - 1)
    return xl
```

```python
import functools

import jax
import jax.numpy as jnp
import numpy as np
from jax import lax
from jax.experimental import pallas as pl
from jax.experimental.pallas import tpu as pltpu

F32 = jnp.float32
BF16 = jnp.bfloat16

GRID_W = 64
HEAD_DIM = 64
LRU_BLOCKS = 8
CONV_W = 4
LRU_C = 8.0
NA_KH_MAX = 8
NA_KW = 16
GQA_KV_HEADS = 2
ROPE_BASE = 10000.0
N_EXPERTS = 16
EC_CAPACITY = 2
N_BRANCH = 3
EPS = 1e-6
NEG_INF = -1e30

LANES = 128
SUBLANES = 8
BF16_ROWS = 16
VMEM_LIMIT_BYTES = 56 * 1024 * 1024

_NT = (((1,), (1,)), ((), ()))


def _cp(*sem):
    return pltpu.CompilerParams(dimension_semantics=sem, vmem_limit_bytes=VMEM_LIMIT_BYTES)


def _sigmoid(x):
    return 0.5 * jnp.tanh(0.5 * x) + 0.5


def _rms(x, g):
    return x * lax.rsqrt(jnp.mean(x * x, axis=-1, keepdims=True) + EPS) * g


def _adaln_kernel(c_ref, w_ref, b_ref, o_ref):
    c = c_ref[...]
    s = (c * _sigmoid(c)).astype(BF16)
    o_ref[...] = jnp.dot(s, w_ref[...].astype(BF16), preferred_element_type=F32) + b_ref[...]


def _adaln(cond, w_mod, b_mod):
    r, d = cond.shape
    n = w_mod.shape[1]
    tn = n // 4
    return pl.pallas_call(
        _adaln_kernel,
        out_shape=jax.ShapeDtypeStruct((r, n), F32),
        grid=(n // tn,),
        in_specs=[
            pl.BlockSpec((r, d), lambda j: (0, 0)),
            pl.BlockSpec((d, tn), lambda j: (0, j)),
            pl.BlockSpec((1, tn), lambda j: (0, j)),
        ],
        out_specs=pl.BlockSpec((r, tn), lambda j: (0, j)),
        compiler_params=_cp("parallel"),
        name="adaln",
    )(cond, w_mod, b_mod.reshape(1, n))


_ACTIVATIONS = {None: lambda v: v, "gelu": jax.nn.gelu, "sigmoid": _sigmoid}


def _inproj_kernel(x_ref, sh_ref, sc_ref, g_ref, w_ref, *o_refs, splits, acts):
    h = (_rms(x_ref[...], g_ref[...]) * (1.0 + sc_ref[...]) + sh_ref[...]).astype(BF16)
    off = 0
    for o_ref, width, act in zip(o_refs, splits, acts):
        y = _ACTIVATIONS[act](jnp.dot(h, w_ref[:, off:off + width], preferred_element_type=F32))
        o_ref[...] = y.astype(o_ref.dtype)
        off += width


def _inproj(x, shift, scale, gain, w, splits, acts, mod_row):
    b, t, d = x.shape
    tm = min(t, 512)
    n = w.shape[1]
    if mod_row is None:
        mod_map = lambda bi, i: (bi, 0, 0)
    else:
        mod_map = lambda bi, i: (mod_row, 0, 0)
    return pl.pallas_call(
        functools.partial(_inproj_kernel, splits=splits, acts=acts),
        out_shape=[jax.ShapeDtypeStruct((b, t, s), BF16) for s in splits],
        grid=(b, t // tm),
        in_specs=[
            pl.BlockSpec((None, tm, d), lambda bi, i: (bi, i, 0)),
            pl.BlockSpec((None, 1, d), mod_map),
            pl.BlockSpec((None, 1, d), mod_map),
            pl.BlockSpec((1, d), lambda bi, i: (0, 0)),
            pl.BlockSpec((d, n), lambda bi, i: (0, 0)),
        ],
        out_specs=[pl.BlockSpec((None, tm, s), lambda bi, i: (bi, i, 0)) for s in splits],
        compiler_params=_cp("parallel", "parallel"),
        name="inproj",
    )(x, shift, scale, gain.reshape(1, d), w)


_SHIFTED_TAPS = tuple(k for k in range(CONV_W) if k != CONV_W // 2)


def _lru_kernel(x_ref, xp_ref, xn_ref, h0_ref, smat_ref, cw_ref, cb_ref, wab_ref, ba_ref, bx_ref, lam_ref,
                y_ref, hfin_ref, u_scr, a_scr, b_scr, h_scr, st_scr, *, tc, nchunks, nb):
    d = pl.program_id(1)
    j = pl.program_id(2)
    c = jnp.where(d == 0, j, nchunks - 1 - j)
    w = x_ref.shape[-1]
    cwide = 2 * LANES

    @pl.when(j == 0)
    def _():
        for cb in range(LRU_BLOCKS):
            h0 = h0_ref[:, cb * LANES:(cb + 1) * LANES]
            st_scr[cb] = jnp.broadcast_to(h0[:, None, :], (nb, SUBLANES, LANES))

    zero = jnp.zeros((), BF16)
    c_half = (-0.5 * LRU_C) * jax.nn.softplus(-lam_ref[...])
    for cc in range(w // cwide):
        sl2 = slice(cc * cwide, (cc + 1) * cwide)
        for bb in range(nb):
            xe = jnp.concatenate([jnp.where(c > 0, xp_ref[bb, :, sl2], zero), x_ref[bb, :, sl2],
                                  jnp.where(c < nchunks - 1, xn_ref[bb, :, sl2], zero)], axis=0)
            sh = jnp.dot(smat_ref[...], xe, preferred_element_type=F32)
            u = cb_ref[:, sl2]
            for k in range(CONV_W):
                if k in _SHIFTED_TAPS:
                    i = _SHIFTED_TAPS.index(k)
                    tap = sh[i * tc:(i + 1) * tc]
                else:
                    tap = x_ref[bb, :, sl2].astype(F32)
                u = u + tap * cw_ref[k:k + 1, sl2]
            u_scr[bb, :, sl2] = u
        for cb in range(cc * (cwide // LANES), (cc + 1) * (cwide // LANES)):
            sl = slice(cb * LANES, (cb + 1) * LANES)
            u = u_scr[:, :, sl].reshape(nb * tc, LANES)
            gates = jnp.dot(u.astype(BF16), wab_ref[cb], preferred_element_type=F32)
            t_r = jnp.tanh(gates[:, :LANES] + ba_ref[:, sl])
            i = 0.5 * jnp.tanh(gates[:, LANES:] + bx_ref[:, sl]) + 0.5
            log_a = c_half[:, sl] * t_r + c_half[:, sl]
            a = jnp.exp(log_a)
            a_scr[cb] = a
            m = -jnp.tanh(log_a) * (a * a + 1.0)
            b_scr[cb] = jnp.where(m > 0.0, m * lax.rsqrt(m), 0.0) * (i * u)

    @pl.when(d == 0)
    def _():
        _block_scan(a_scr, b_scr, h_scr, st_scr, False, nb, tc)

    @pl.when(d == 1)
    def _():
        _block_scan(a_scr, b_scr, h_scr, st_scr, True, nb, tc)

    for cb in range(LRU_BLOCKS):
        y_ref[:, :, cb * LANES:(cb + 1) * LANES] = h_scr[cb].astype(y_ref.dtype)

    @pl.when(j == nchunks - 1)
    def _():
        for cb in range(LRU_BLOCKS):
            hfin_ref[:, cb * LANES:(cb + 1) * LANES] = st_scr[cb][:, 0, :]


def _block_scan(a_scr, b_scr, h_scr, st_scr, rev, nb, tc):
    nblk = tc // SUBLANES
    row = lax.broadcasted_iota(jnp.int32, (1, SUBLANES, LANES), 1)
    for cb in range(LRU_BLOCKS):
        a = a_scr[cb].reshape(nb * nblk, SUBLANES, LANES)
        b = b_scr[cb].reshape(nb * nblk, SUBLANES, LANES)
        for s in (1, 2, 4):
            shift, keep = (SUBLANES - s, row < SUBLANES - s) if rev else (s, row >= s)
            b = b + a * jnp.where(keep, pltpu.roll(b, shift, 1), 0.0)
            a = a * jnp.where(keep, pltpu.roll(a, shift, 1), 1.0)
        a = a.reshape(nb, nblk, SUBLANES, LANES)
        b = b.reshape(nb, nblk, SUBLANES, LANES)
        carry = st_scr[cb]
        last = 0 if rev else SUBLANES - 1
        for k in (reversed(range(nblk)) if rev else range(nblk)):
            h = a[:, k] * carry + b[:, k]
            h_scr[cb, :, k * SUBLANES:(k + 1) * SUBLANES, :] = h
            carry = jnp.broadcast_to(h[:, last:last + 1, :], h.shape)
        st_scr[cb] = carry


def _lru(x, h0, lp):
    b, t, w = x.shape
    nb = SUBLANES
    tc = min(t, 128)
    nchunks = t // tc
    hb = tc // BF16_ROWS
    nhb = t // BF16_ROWS

    def chunk(di, j):
        return jnp.where(di == 0, j, nchunks - 1 - j)

    smat = np.zeros((len(_SHIFTED_TAPS) * tc, tc + 2 * BF16_ROWS), np.float32)
    for i, k in enumerate(_SHIFTED_TAPS):
        smat[i * tc + np.arange(tc), BF16_ROWS + np.arange(tc) + k - CONV_W // 2] = 1.0

    vec = lambda g, di, j: (0, 0)
    dvec = lambda g, di, j: (di, 0, 0)
    return pl.pallas_call(
        functools.partial(_lru_kernel, tc=tc, nchunks=nchunks, nb=nb),
        out_shape=[jax.ShapeDtypeStruct((2, b, t, w), BF16), jax.ShapeDtypeStruct((2, b, w), F32)],
        grid=(b // nb, 2, nchunks),
        in_specs=[
            pl.BlockSpec((nb, tc, w), lambda g, di, j: (g, chunk(di, j), 0)),
            pl.BlockSpec((nb, BF16_ROWS, w), lambda g, di, j: (g, jnp.maximum(chunk(di, j) * hb - 1, 0), 0)),
            pl.BlockSpec((nb, BF16_ROWS, w), lambda g, di, j: (g, jnp.minimum((chunk(di, j) + 1) * hb, nhb - 1), 0)),
            pl.BlockSpec((None, nb, w), lambda g, di, j: (di, g, 0)),
            pl.BlockSpec(smat.shape, vec),
            pl.BlockSpec((CONV_W, w), vec),
            pl.BlockSpec((1, w), vec),
            pl.BlockSpec((None, LRU_BLOCKS, LANES, 2 * LANES), lambda g, di, j: (di, 0, 0, 0)),
            pl.BlockSpec((None, 1, w), dvec),
            pl.BlockSpec((None, 1, w), dvec),
            pl.BlockSpec((None, 1, w), dvec),
        ],
        out_specs=[
            pl.BlockSpec((None, nb, tc, w), lambda g, di, j: (di, g, chunk(di, j), 0)),
            pl.BlockSpec((None, nb, w), lambda g, di, j: (di, g, 0)),
        ],
        scratch_shapes=[
            pltpu.VMEM((nb, tc, w), F32),
            pltpu.VMEM((LRU_BLOCKS, nb * tc, LANES), F32),
            pltpu.VMEM((LRU_BLOCKS, nb * tc, LANES), F32),
            pltpu.VMEM((LRU_BLOCKS, nb, tc, LANES), F32),
            pltpu.VMEM((LRU_BLOCKS, nb, SUBLANES, LANES), F32),
        ],
        compiler_params=_cp("parallel", "arbitrary", "arbitrary"),
        name="rglru",
    )(x, x, x, h0, jnp.asarray(smat, BF16), lp["conv_w"], lp["conv_b"], lp["wab"], lp["ba"], lp["bx"], lp["lam"])


def _pair_rows(x2):
    low = lax.broadcasted_iota(jnp.int32, x2.shape, 1) < HEAD_DIM
    zero = jnp.zeros((), x2.dtype)
    return jnp.concatenate([jnp.where(low, x2, zero), jnp.where(low, zero, x2)], axis=0)


def _softmax_strips(s_ref, p_ref):
    sums = []
    for r in range(s_ref.shape[0] // BF16_ROWS):
        rows = slice(r * BF16_ROWS, (r + 1) * BF16_ROWS)
        s = s_ref[rows, :]
        p = jnp.exp(s - jnp.max(s, axis=-1, keepdims=True))
        sums.append(jnp.sum(p, axis=-1, keepdims=True))
        p_ref[rows, :] = p.astype(BF16)
    return sums


def _normalise_strips(o, sums):
    return jnp.concatenate([o[r * BF16_ROWS:(r + 1) * BF16_ROWS] / l for r, l in enumerate(sums)], axis=0)


def _na_kernel(var_ref, q_ref, k_ref, v_ref, kc_ref, vc_ref, bias_ref, o_ref, s_scr, p_scr,
               *, band, win_rows, kh, rows):
    del var_ref
    r0 = pl.program_id(1) * band
    ws = jnp.clip(r0 - kh // 2, 0, rows - win_rows)
    start = pl.multiple_of(ws * GRID_W, GRID_W)
    nwin = win_rows * GRID_W
    m = band * GRID_W
    npairs = q_ref.shape[-1] // LANES
    low = lax.broadcasted_iota(jnp.int32, (m, LANES), 1) < HEAD_DIM

    def scores(hp):
        sl = slice(hp * LANES, (hp + 1) * LANES)
        qs = _pair_rows(q_ref[:, sl] * jnp.asarray(HEAD_DIM ** -0.5, BF16))
        s_lat = lax.dot_general(qs, k_ref[pl.ds(start, nwin), sl], _NT, preferred_element_type=F32)
        s_scr[hp % 2, :, 0:nwin] = s_lat + bias_ref[2 * hp:2 * hp + 2].reshape(2 * m, nwin)
        s_scr[hp % 2, :, nwin:] = lax.dot_general(qs, kc_ref[:, sl], _NT, preferred_element_type=F32)

    def finish(hp):
        sl = slice(hp * LANES, (hp + 1) * LANES)
        sums = _softmax_strips(s_scr.at[hp % 2], p_scr.at[hp % 2])
        o = jnp.dot(p_scr[hp % 2, :, 0:nwin], v_ref[pl.ds(start, nwin), sl], preferred_element_type=F32)
        o = o + jnp.dot(p_scr[hp % 2, :, nwin:], vc_ref[:, sl], preferred_element_type=F32)
        o = _normalise_strips(o, sums)
        o_ref[:, sl] = jnp.where(low, o[:m], o[m:]).astype(o_ref.dtype)

    scores(0)
    for hp in range(npairs):
        if hp + 1 < npairs:
            scores(hp + 1)
        finish(hp)


def _na_bands(rows, band, win_rows):
    kh = min(NA_KH_MAX, rows)
    variants, var_of_band = [], []
    for r0 in range(0, rows, band):
        ws = int(np.clip(r0 - kh // 2, 0, rows - win_rows))
        key = (r0 - ws, tuple(int(np.clip(r0 + rr - kh // 2, 0, rows - kh)) - ws for rr in range(band)))
        if key not in variants:
            variants.append(key)
        var_of_band.append(variants.index(key))
    return variants, np.asarray(var_of_band, np.int32)


def _na_bias_table(rpb, rows, band, win_rows):
    kh = min(NA_KH_MAX, rows)
    col = np.arange(GRID_W)
    cs = np.clip(col - NA_KW // 2, 0, GRID_W - NA_KW)
    col_mask = (col[None, :] >= cs[:, None]) & (col[None, :] < cs[:, None] + NA_KW)
    dcol = np.clip(col[None, :] - col[:, None], -(NA_KW - 1), NA_KW - 1) + NA_KW - 1
    heads, n_drow, n_dcol = rpb.shape
    onehot = jnp.asarray(dcol.reshape(-1)[None, :] == np.arange(n_dcol)[:, None], F32)
    t2 = jnp.dot(rpb.reshape(heads * n_drow, n_dcol).astype(F32), onehot, precision=lax.Precision.HIGHEST)
    t2 = t2.reshape(heads, n_drow, GRID_W, GRID_W)
    t2 = jnp.where(col_mask[None, None], t2, NEG_INF)
    masked = jnp.full((heads, GRID_W, GRID_W), NEG_INF, F32)
    variants, _ = _na_bands(rows, band, win_rows)
    tabs = []
    for delta, first in variants:
        q_rows = []
        for rr in range(band):
            blocks = [t2[:, j - delta - rr + NA_KH_MAX - 1] if first[rr] <= j < first[rr] + kh else masked
                      for j in range(win_rows)]
            q_rows.append(jnp.concatenate(blocks, axis=-1))
        tabs.append(jnp.concatenate(q_rows, axis=1))
    return jnp.stack(tabs)


def _na(q, k, v, kc, vc, rpb):
    b, s, w = q.shape
    l = kc.shape[1]
    heads = w // HEAD_DIM
    rows = s // GRID_W
    kh = min(NA_KH_MAX, rows)
    band = min(4, rows)
    win_rows = min(-(-(kh + band - 1) // 2) * 2, rows)
    m, nwin = band * GRID_W, win_rows * GRID_W
    bias = _na_bias_table(rpb, rows, band, win_rows)
    _, var_of_band = _na_bands(rows, band, win_rows)

    full = lambda bi, r, var: (bi, 0, 0)
    grid_spec = pltpu.PrefetchScalarGridSpec(
        num_scalar_prefetch=1,
        grid=(b, rows // band),
        in_specs=[
            pl.BlockSpec((None, m, w), lambda bi, r, var: (bi, r, 0)),
            pl.BlockSpec((None, s, w), full),
            pl.BlockSpec((None, s, w), full),
            pl.BlockSpec((None, l, w), full),
            pl.BlockSpec((None, l, w), full),
            pl.BlockSpec((None, heads, m, nwin), lambda bi, r, var: (var[r], 0, 0, 0)),
        ],
        out_specs=pl.BlockSpec((None, m, w), lambda bi, r, var: (bi, r, 0)),
        scratch_shapes=[pltpu.VMEM((2, 2 * m, nwin + l), F32), pltpu.VMEM((2, 2 * m, nwin + l), BF16)],
    )
    return pl.pallas_call(
        functools.partial(_na_kernel, band=band, win_rows=win_rows, kh=kh, rows=rows),
        out_shape=jax.ShapeDtypeStruct((b, s, w), BF16),
        grid_spec=grid_spec,
        compiler_params=_cp("parallel", "arbitrary"),
        name="natten",
    )(jnp.asarray(var_of_band), q, k, v, kc, vc, bias)


def _head_norm(x, g, gmat):
    cols = []
    for c in range(x.shape[1] // LANES):
        xc = x[:, c * LANES:(c + 1) * LANES]
        x2 = xc * xc
        hi = x2.astype(BF16)
        lo = (x2 - hi.astype(F32)).astype(BF16)
        ms = jnp.dot(hi, gmat, preferred_element_type=F32) + jnp.dot(lo, gmat, preferred_element_type=F32)
        cols.append(xc * lax.rsqrt(ms + EPS) * g)
    return cols[0] if len(cols) == 1 else jnp.concatenate(cols, axis=-1)


def _rope(x, cos, sin):
    half = HEAD_DIM // 2
    lane = lax.broadcasted_iota(jnp.int32, (x.shape[0], LANES), 1)
    low = (lane % HEAD_DIM) < half
    cols = []
    for c in range(x.shape[1] // LANES):
        xc = x[:, c * LANES:(c + 1) * LANES]
        sw = jnp.where(low, pltpu.roll(xc, LANES - half, 1), pltpu.roll(xc, half, 1))
        cols.append(xc * cos + sw * sin)
    return cols[0] if len(cols) == 1 else jnp.concatenate(cols, axis=-1)


def _attn_kernel(*refs, norm, group, tq):
    if norm:
        q_ref, k_ref, v_ref, qg_ref, kg_ref, gmat_ref, o_ref = refs
    else:
        q_ref, k_ref, v_ref, o_ref = refs
    q = q_ref[...].astype(F32)
    k = k_ref[...]
    if norm:
        q = _head_norm(q, qg_ref[...], gmat_ref[...])
        k = _head_norm(k.astype(F32), kg_ref[...], gmat_ref[...]).astype(BF16)
    q = (q * (HEAD_DIM ** -0.5)).astype(BF16)

    outs = []
    for g in range(k_ref.shape[1] // HEAD_DIM):
        sl = slice(g * HEAD_DIM, (g + 1) * HEAD_DIM)
        qs = [q[:, (g * group + jj) * HEAD_DIM:(g * group + jj + 1) * HEAD_DIM] for jj in range(group)]
        qs = qs[0] if group == 1 else jnp.concatenate(qs, axis=0)
        s = lax.dot_general(qs, k[:, sl], _NT, preferred_element_type=F32)
        p = jnp.exp(s - jnp.max(s, axis=-1, keepdims=True))
        o = jnp.dot(p.astype(BF16), v_ref[:, sl], preferred_element_type=F32) / jnp.sum(p, axis=-1, keepdims=True)
        for jj in range(group):
            outs.append(o[jj * tq:(jj + 1) * tq])
    o_ref[...] = jnp.concatenate(outs, axis=-1).astype(o_ref.dtype)


def _dup_heads(x):
    low = lax.broadcasted_iota(jnp.int32, x.shape, 1) < HEAD_DIM
    sw = pltpu.roll(x, HEAD_DIM, 1)
    return jnp.where(low, x, sw), jnp.where(low, sw, x)


def _gqa_kernel(q_ref, kc_ref, vc_ref, kl_ref, vl_ref, qg_ref, kg_ref, gmat_ref, cosq_ref, sinq_ref, cosk_ref,
                sink_ref, o_ref, kc_scr, kl_scr, vc_scr, vl_scr, s_scr, p_scr, *, sub):
    n_ctx = kc_ref.shape[0]
    n_kv = GQA_KV_HEADS

    @pl.when(pl.program_id(1) == 0)
    def _():
        gmat = gmat_ref[...]
        kc = _head_norm(kc_ref[...].astype(F32), kg_ref[...], gmat)
        kl = _rope(_head_norm(kl_ref[...].astype(F32), kg_ref[...], gmat), cosk_ref[...], sink_ref[...])
        for scr, val in ((kc_scr, kc), (kl_scr, kl), (vc_scr, vc_ref[...].astype(F32)),
                         (vl_scr, vl_ref[...].astype(F32))):
            for g, dup in enumerate(_dup_heads(val)):
                scr[g] = dup.astype(BF16)

    tq = q_ref.shape[0]
    chunks = []
    for c in range(q_ref.shape[1] // LANES):
        qc = q_ref[:, c * LANES:(c + 1) * LANES].astype(F32)
        qc = _rope(_head_norm(qc, qg_ref[...], gmat_ref[...]), cosq_ref[...], sinq_ref[...])
        chunks.append((qc * (HEAD_DIM ** -0.5)).astype(BF16))
    pairs_per_group = len(chunks) // n_kv
    low = lax.broadcasted_iota(jnp.int32, (sub, LANES), 1) < HEAD_DIM
    units = [(t, g) for t in range(tq // sub) for g in range(n_kv)]

    def scores(u):
        t, g = units[u]
        rows = [_pair_rows(chunks[g * pairs_per_group + j][t * sub:(t + 1) * sub]) for j in range(pairs_per_group)]
        qs = jnp.concatenate(rows, axis=0)
        s_scr[u % 2, :, 0:n_ctx] = lax.dot_general(qs, kc_scr[g], _NT, preferred_element_type=F32)
        s_scr[u % 2, :, n_ctx:] = lax.dot_general(qs, kl_scr[g], _NT, preferred_element_type=F32)

    def finish(u):
        t, g = units[u]
        sums = _softmax_strips(s_scr.at[u % 2], p_scr.at[u % 2])
        o = jnp.dot(p_scr[u % 2, :, 0:n_ctx], vc_scr[g], preferred_element_type=F32)
        o = o + jnp.dot(p_scr[u % 2, :, n_ctx:], vl_scr[g], preferred_element_type=F32)
        o = _normalise_strips(o, sums)
        for j in range(pairs_per_group):
            c = g * pairs_per_group + j
            lo, hi = o[2 * j * sub:(2 * j + 1) * sub], o[(2 * j + 1) * sub:(2 * j + 2) * sub]
            o_ref[t * sub:(t + 1) * sub, c * LANES:(c + 1) * LANES] = jnp.where(low, lo, hi).astype(o_ref.dtype)

    scores(0)
    for u in range(len(units)):
        if u + 1 < len(units):
            scores(u + 1)
        finish(u)


def _gqa(q, kc, vc, kl, vl, qg, kg):
    b, t, wq = q.shape
    l, wk = kc.shape[1], kc.shape[2]
    s = kl.shape[1]
    assert wk == GQA_KV_HEADS * HEAD_DIM == LANES
    tq = min(t, 512)
    sub = min(tq, 256)
    m_unit = 2 * (wq // LANES // GQA_KV_HEADS) * sub
    reps = LANES // HEAD_DIM
    gmat = _head_mean_matrix()
    cos, sin = _rope_tables(t)
    full = lambda bi, i: (bi, 0, 0)
    const = lambda bi, i: (0, 0)
    return pl.pallas_call(
        functools.partial(_gqa_kernel, sub=sub),
        out_shape=jax.ShapeDtypeStruct((b, t, wq), BF16),
        grid=(b, t // tq),
        in_specs=[
            pl.BlockSpec((None, tq, wq), lambda bi, i: (bi, i, 0)),
            pl.BlockSpec((None, l, wk), full), pl.BlockSpec((None, l, wk), full),
            pl.BlockSpec((None, s, wk), full), pl.BlockSpec((None, s, wk), full),
            pl.BlockSpec((1, LANES), const), pl.BlockSpec((1, LANES), const), pl.BlockSpec((LANES, LANES), const),
            pl.BlockSpec((tq, LANES), lambda bi, i: (i, 0)), pl.BlockSpec((tq, LANES), lambda bi, i: (i, 0)),
            pl.BlockSpec((s, LANES), const), pl.BlockSpec((s, LANES), const),
        ],
        out_specs=pl.BlockSpec((None, tq, wq), lambda bi, i: (bi, i, 0)),
        scratch_shapes=[
            pltpu.VMEM((GQA_KV_HEADS, l, LANES), BF16), pltpu.VMEM((GQA_KV_HEADS, s, LANES), BF16),
            pltpu.VMEM((GQA_KV_HEADS, l, LANES), BF16), pltpu.VMEM((GQA_KV_HEADS, s, LANES), BF16),
            pltpu.VMEM((2, m_unit, l + s), F32), pltpu.VMEM((2, m_unit, l + s), BF16),
        ],
        compiler_params=_cp("parallel", "arbitrary"),
        name="gqa",
    )(q, kc, vc, kl, vl, jnp.tile(qg, reps).reshape(1, LANES), jnp.tile(kg, reps).reshape(1, LANES), gmat,
      cos, sin, cos, sin)


def _rope_tables(n_tokens):
    t = np.arange(n_tokens)
    pos = np.stack([t // GRID_W, t % GRID_W], axis=-1).astype(np.float32)
    n_freq = HEAD_DIM // 4
    inv_freq = jnp.asarray(ROPE_BASE, F32) ** (-jnp.arange(n_freq, dtype=F32) / n_freq)
    ang = (jnp.asarray(pos)[:, :, None] * inv_freq).reshape(n_tokens, 2 * n_freq)
    cos, sin = jnp.cos(ang), jnp.sin(ang)
    reps = LANES // HEAD_DIM
    return jnp.tile(jnp.concatenate([cos, cos], -1), (1, reps)), jnp.tile(jnp.concatenate([-sin, sin], -1), (1, reps))


def _head_mean_matrix():
    blk = np.arange(LANES) // HEAD_DIM
    return jnp.asarray((blk[:, None] == blk[None, :]).astype(np.float32) / HEAD_DIM, BF16)


def _attn(q, k, v, qg=None, kg=None):
    b, t, wq = q.shape
    wk = k.shape[2]
    norm = qg is not None
    tok = lambda w: pl.BlockSpec((None, t, w), lambda bi: (bi, 0, 0))
    args = [q, k, v]
    in_specs = [tok(wq), tok(wk), tok(wk)]
    if norm:
        reps = LANES // HEAD_DIM
        args += [jnp.tile(qg, reps).reshape(1, LANES), jnp.tile(kg, reps).reshape(1, LANES), _head_mean_matrix()]
        in_specs += [pl.BlockSpec((1, LANES), lambda bi: (0, 0))] * 2
        in_specs += [pl.BlockSpec((LANES, LANES), lambda bi: (0, 0))]
    return pl.pallas_call(
        functools.partial(_attn_kernel, norm=norm, group=wq // wk, tq=t),
        out_shape=jax.ShapeDtypeStruct((b, t, wq), BF16),
        grid=(b,),
        in_specs=in_specs,
        out_specs=tok(wq),
        compiler_params=_cp("parallel"),
        name="attn",
    )(*args)


def _merge_kernel(hf_ref, hr_ref, gl_ref, yb_ref, yc_ref, gt_ref, x_ref, g1_ref, sh2_ref, sc2_ref, ng1_ref, ng2_ref,
                  woa_ref, wob_ref, woc_ref, wout_ref, x1_ref, h2_ref, h2p_ref):
    d = x_ref.shape[-1]
    ya = (hf_ref[...].astype(F32) + hr_ref[...].astype(F32)) * gl_ref[...].astype(F32)
    pa = jnp.dot(ya.astype(BF16), woa_ref[...], preferred_element_type=F32)
    pb = jnp.dot(yb_ref[...], wob_ref[...], preferred_element_type=F32)
    pc = jnp.dot(yc_ref[...], woc_ref[...], preferred_element_type=F32)
    ga = gt_ref[:, 0:d].astype(F32)
    gb = gt_ref[:, d:2 * d].astype(F32)
    gc = gt_ref[:, 2 * d:3 * d].astype(F32)
    m = ga * pa + gb * pb + gc * pc
    y = jnp.dot(m.astype(BF16), wout_ref[...], preferred_element_type=F32)
    x1 = x_ref[...] + g1_ref[...] * _rms(y, ng1_ref[...])
    x1_ref[...] = x1
    h2 = (_rms(x1, ng2_ref[...]) * (1.0 + sc2_ref[...]) + sh2_ref[...]).astype(BF16)
    h2_ref[...] = h2
    h2p_ref[...] = _pack_halves(h2)


def _pack_halves(xb):
    half = xb.shape[1] // 2
    bits = pltpu.bitcast(xb.astype(F32), jnp.uint32)
    return bits[:, :half] | (bits[:, half:] >> 16)


def _unpack_halves(words):
    hi = pltpu.bitcast(words & jnp.uint32(0xFFFF0000), F32)
    lo = pltpu.bitcast(words << 16, F32)
    return jnp.concatenate([hi, lo], axis=1).astype(BF16)


def _merge(h, gl, yb, yc, gates, x, g1, sh2, sc2, ng1, ng2, woa, wob, woc, wout, mod_row):
    b, t, d = x.shape
    tm = min(t, 512)
    wl, wb, wc = gl.shape[2], yb.shape[2], yc.shape[2]
    if mod_row is None:
        mod_map = lambda bi, i: (bi, 0, 0)
    else:
        mod_map = lambda bi, i: (mod_row, 0, 0)
    tok = lambda w: pl.BlockSpec((None, tm, w), lambda bi, i: (bi, i, 0))
    mod = pl.BlockSpec((None, 1, d), mod_map)
    const = lambda r, c: pl.BlockSpec((r, c), lambda bi, i: (0, 0))
    return pl.pallas_call(
        _merge_kernel,
        out_shape=[jax.ShapeDtypeStruct((b, t, d), F32), jax.ShapeDtypeStruct((b, t, d), BF16),
                   jax.ShapeDtypeStruct((b, t, d // 2), jnp.uint32)],
        grid=(b, t // tm),
        in_specs=[
            pl.BlockSpec((None, None, tm, wl), lambda bi, i: (0, bi, i, 0)),
            pl.BlockSpec((None, None, tm, wl), lambda bi, i: (1, bi, i, 0)),
            tok(wl), tok(wb), tok(wc), tok(N_BRANCH * d), tok(d),
            mod, mod, mod, const(1, d), const(1, d),
            const(wl, d), const(wb, d), const(wc, d), const(d, d),
        ],
        out_specs=[tok(d), tok(d), tok(d // 2)],
        compiler_params=_cp("parallel", "parallel"),
        name="merge",
    )(h, h, gl, yb, yc, gates, x, g1, sh2, sc2, ng1.reshape(1, d), ng2.reshape(1, d), woa, wob, woc, wout)


def _lane_cumsum(x, tri):
    e, n = x.shape
    carry = jnp.zeros((e, 1), F32)
    cols = []
    for c in range(n // LANES):
        part = jnp.dot(x[:, c * LANES:(c + 1) * LANES].astype(BF16), tri, preferred_element_type=F32) + carry
        cols.append(part)
        carry = part[:, LANES - 1:LANES]
    return cols[0] if len(cols) == 1 else jnp.concatenate(cols, axis=-1)


def _route_kernel(h_ref, wr_ref, tri_ref, tok_ref, idx_ref, gcol_ref, pos_scr, aff_scr, *, cap):
    nbs, n_exp = h_ref.shape[0], wr_ref.shape[0]
    affs = []
    for s in range(nbs):
        logits = lax.dot_general(wr_ref[...], h_ref[s], _NT, preferred_element_type=F32)
        z = jnp.exp(logits - jnp.max(logits, axis=0, keepdims=True))
        affs.append(z / jnp.sum(z, axis=0, keepdims=True))
    aff = affs[0] if nbs == 1 else jnp.concatenate(affs, axis=0)
    aff_scr[...] = aff
    bits = pltpu.bitcast(aff, jnp.int32)
    e, n = bits.shape

    def bisect(_, lohi):
        lo, hi = lohi
        mid = lo + ((hi - lo + 1) >> 1)
        cnt = jnp.sum(jnp.where(bits >= mid, 1.0, 0.0), axis=1, keepdims=True)
        ok = cnt >= cap
        return jnp.where(ok, mid, lo), jnp.where(ok, hi, mid - 1)

    one_bits = 0x3F800000
    lo, _ = lax.fori_loop(0, 31, bisect, (jnp.zeros((e, 1), jnp.int32), jnp.full((e, 1), one_bits, jnp.int32)))
    gt = jnp.where(bits > lo, 1.0, 0.0)
    eq = jnp.where(bits == lo, 1.0, 0.0)
    need = cap - jnp.sum(gt, axis=1, keepdims=True)
    tri = tri_ref[...]
    sel = gt + jnp.where(_lane_cumsum(eq, tri) <= need, eq, 0.0)
    pos = _lane_cumsum(sel, tri) - 1.0
    pos_scr[...] = jnp.where(sel > 0.0, pos, -1.0).astype(jnp.int32)

    slot = lax.broadcasted_iota(jnp.int32, (cap, n), 0)

    def compact(r, carry):
        hit = pos_scr[pl.ds(r, 1), :] == slot
        onehot = jnp.where(hit, 1.0, 0.0).astype(BF16)
        digits = lax.dot_general(tok_ref[...], onehot, _NT, preferred_element_type=F32)
        s, ex = r // n_exp, r % n_exp
        idx_ref[s, pl.ds(ex, 1), :] = (digits[0:1] * GRID_W + digits[1:2]).astype(jnp.int32)
        gcol_ref[s, ex] = jnp.sum(jnp.where(hit, aff_scr[pl.ds(r, 1), :], 0.0), axis=1, keepdims=True)
        return carry

    lax.fori_loop(0, e, compact, 0, unroll=2)


def _route(h2, w_router_t, cap):
    b, n, d = h2.shape
    e = w_router_t.shape[0]
    nbs = 4 if b % 4 == 0 else 1
    tri = jnp.asarray(np.triu(np.ones((LANES, LANES), np.float32)), BF16)
    t = np.arange(n)
    tok = np.zeros((SUBLANES, n), np.float32)
    tok[0], tok[1] = t // GRID_W, t % GRID_W
    return pl.pallas_call(
        functools.partial(_route_kernel, cap=cap),
        out_shape=[jax.ShapeDtypeStruct((b, e, cap), jnp.int32), jax.ShapeDtypeStruct((b, e, cap, 1), F32)],
        grid=(b // nbs,),
        in_specs=[
            pl.BlockSpec((nbs, n, d), lambda bi: (bi, 0, 0)),
            pl.BlockSpec((e, d), lambda bi: (0, 0)),
            pl.BlockSpec((LANES, LANES), lambda bi: (0, 0)),
            pl.BlockSpec((SUBLANES, n), lambda bi: (0, 0)),
        ],
        out_specs=[pl.BlockSpec((nbs, e, cap), lambda bi: (bi, 0, 0)),
                   pl.BlockSpec((nbs, e, cap, 1), lambda bi: (bi, 0, 0, 0))],
        scratch_shapes=[pltpu.VMEM((nbs * e, n), jnp.int32), pltpu.VMEM((nbs * e, n), F32)],
        compiler_params=_cp("parallel"),
        name="route",
    )(h2, w_router_t, tri, jnp.asarray(tok, BF16))


def _expert_kernel(idx_smem, gcol_ref, h_ref, wg_ref, wu_ref, wd_ref, x_hbm, g2_ref, ng_ref, o_hbm,
                   o_ref, hs_scr, y_scr, xbuf, sem, osem, *, cap, nbs, n_exp, per_sample_gate):
    e = pl.program_id(1)
    n = h_ref.shape[1]

    def gather(expert, buf):
        for s in range(nbs):
            for j in range(cap):
                t = idx_smem[s, expert, j]
                hs_scr[buf, pl.ds(s * cap + j, 1), :] = h_ref[s, pl.ds(t, 1), :]

    def scatter_add(expert, buf):
        group = 8
        for s in range(nbs):
            for j0 in range(0, cap, group):
                ts = [idx_smem[s, expert, j0 + g] for g in range(group)]
                acc = [o_ref[s, pl.ds(t, 1), :] for t in ts]
                for g, t in enumerate(ts):
                    o_ref[s, pl.ds(t, 1), :] = acc[g] + y_scr[buf, pl.ds(s * cap + j0 + g, 1), :]

    @pl.when(e == 0)
    def _():
        o_ref[...] = jnp.zeros_like(o_ref)
        y_scr[1] = jnp.zeros(y_scr.shape[1:], F32)
        gather(0, 0)

    def step(cur):
        hs = _unpack_halves(hs_scr[cur])
        scatter_add(jnp.maximum(e - 1, 0), 1 - cur)
        gather(jnp.minimum(e + 1, n_exp - 1), 1 - cur)
        gcol = gcol_ref[...].reshape(nbs * cap, 1)
        a = jnp.dot(hs, wg_ref[...], preferred_element_type=F32)
        u = (a * _sigmoid(a)) * jnp.dot(hs, wu_ref[...], preferred_element_type=F32)
        y_scr[cur] = jnp.dot(u.astype(BF16), wd_ref[...], preferred_element_type=F32) * gcol

    for cur in range(2):
        pl.when(e % 2 == cur)(functools.partial(step, cur))

    rc = xbuf.shape[1]
    chunks = [(s, c) for s in range(nbs) for c in range(n // rc)]
    first = pl.program_id(0) * nbs

    def x_copy(k):
        s, c = chunks[k]
        return pltpu.make_async_copy(x_hbm.at[first + s, pl.ds(c * rc, rc), :], xbuf.at[k % 2], sem.at[k % 2])

    def o_copy(k):
        s, c = chunks[k]
        return pltpu.make_async_copy(o_ref.at[s, pl.ds(c * rc, rc), :], o_hbm.at[first + s, pl.ds(c * rc, rc), :],
                                     osem.at[k])

    @pl.when(e == n_exp - 1)
    def _():
        x_copy(0).start()
        scatter_add(n_exp - 1, (n_exp - 1) % 2)
        for k, (s, c) in enumerate(chunks):
            if k + 1 < len(chunks):
                x_copy(k + 1).start()
            x_copy(k).wait()
            g2 = g2_ref[s if per_sample_gate else 0]
            sl = slice(c * rc, (c + 1) * rc)
            o_ref[s, sl, :] = xbuf[k % 2] + g2 * _rms(o_ref[s, sl, :], ng_ref[...])
            o_copy(k).start()
        for k in range(len(chunks)):
            o_copy(k).wait()


def _experts(h2p, idx, gcol, wg, wu, wd, cap, x1, g2, ng, mod_row):
    b, n, dh = h2p.shape
    e, d, f = wg.shape
    nbs = max(1, min(b, 512 // cap))
    rc = min(n, 256)
    sample = lambda bi, ei: (bi, 0, 0)
    if mod_row is None:
        g2_spec = pl.BlockSpec((nbs, 1, d), sample)
    else:
        g2_spec = pl.BlockSpec((1, 1, d), lambda bi, ei: (mod_row, 0, 0))
    return pl.pallas_call(
        functools.partial(_expert_kernel, cap=cap, nbs=nbs, n_exp=e, per_sample_gate=mod_row is None),
        out_shape=jax.ShapeDtypeStruct((b, n, d), F32),
        grid=(b // nbs, e),
        in_specs=[
            pl.BlockSpec((nbs, e, cap), sample, memory_space=pltpu.SMEM),
            pl.BlockSpec((nbs, None, cap, 1), lambda bi, ei: (bi, ei, 0, 0)),
            pl.BlockSpec((nbs, n, dh), sample),
            pl.BlockSpec((None, d, f), lambda bi, ei: (ei, 0, 0)),
            pl.BlockSpec((None, d, f), lambda bi, ei: (ei, 0, 0)),
            pl.BlockSpec((None, f, d), lambda bi, ei: (ei, 0, 0)),
            pl.BlockSpec(memory_space=pl.ANY),
            g2_spec,
            pl.BlockSpec((1, d), lambda bi, ei: (0, 0)),
        ],
        out_specs=pl.BlockSpec(memory_space=pl.ANY),
        scratch_shapes=[pltpu.VMEM((nbs, n, d), F32),
                        pltpu.VMEM((2, nbs * cap, dh), jnp.uint32), pltpu.VMEM((2, nbs * cap, d), F32),
                        pltpu.VMEM((2, rc, d), F32), pltpu.SemaphoreType.DMA((2,)),
                        pltpu.SemaphoreType.DMA((nbs * (n // rc),))],
        compiler_params=_cp("parallel", "arbitrary"),
        name="experts",
    )(idx, gcol, h2p, wg, wu, wd, x1, g2, ng.reshape(1, d))


def _split_cols(w, widths):
    offs = np.cumsum((0,) + tuple(widths))
    return [w[:, int(offs[i]):int(offs[i + 1])] for i in range(len(widths))]


def _layer(xl, xc, mods, p, need_ctx):
    bsz, s, d = xl.shape
    ctx_row = bsz
    sh1, sc1, g1, sh2, sc2, g2 = mods
    ng = p["norm_g"]
    na_w = (d // 128) * HEAD_DIM
    gq_w = (d // 128) * HEAD_DIM
    gkv_w = GQA_KV_HEADS * HEAD_DIM
    splits = (d, d, na_w, na_w, na_w, gq_w, gkv_w, gkv_w, N_BRANCH * d)
    acts = (None, "gelu", None, None, None, None, None, None, "sigmoid")
    w_in = p["w_in"].astype(BF16)

    lx, lg, nq, nk, nv, gq, gk, gv, gates = _inproj(xl, sh1, sc1, ng[0], w_in, splits, acts, None)
    if need_ctx:
        cx, cg, cnq, cnk, cnv, cgq, cgk, cgv, cgates = _inproj(xc, sh1, sc1, ng[0], w_in, splits, acts, ctx_row)
    else:
        parts = _split_cols(w_in, splits)
        kv_parts = (0, 3, 4, 6, 7)
        w_kv = jnp.concatenate([parts[i] for i in kv_parts], axis=1)
        cx, cnk, cnv, cgk, cgv = _inproj(xc, sh1, sc1, ng[0], w_kv, tuple(splits[i] for i in kv_parts),
                                         tuple(acts[i] for i in kv_parts), ctx_row)

    lp = {
        "conv_w": p["conv_w"], "conv_b": p["conv_b"].reshape(1, d),
        "wab": (0.5 * jnp.concatenate([p["lru_wa"], p["lru_wx"]], axis=-1)).astype(BF16),
        "ba": 0.5 * p["lru_ba"].reshape(2, 1, d), "bx": 0.5 * p["lru_bx"].reshape(2, 1, d),
        "lam": p["lru_lam"].reshape(2, 1, d),
    }
    hc, hfin = _lru(cx, jnp.zeros((2, bsz, d), F32), lp)
    hl, _ = _lru(lx, hfin, lp)

    yb = _na(nq, nk, nv, cnk, cnv, p["na_rpb"])
    yc = _gqa(gq, cgk, cgv, gk, gv, p["qn_g"], p["kn_g"])

    woa, wob, woc, wout = (p[k].astype(BF16) for k in ("w_o_a", "w_o_b", "w_o_c", "w_out"))
    wr_t = p["w_router"].T.astype(BF16)
    wg, wu, wd = (p[k].astype(BF16) for k in ("w_gate", "w_up", "w_down"))

    def ffn(x1, h2, h2p, mod_row):
        n = x1.shape[1]
        cap = EC_CAPACITY * n // N_EXPERTS
        idx, gcol = _route(h2, wr_t, cap)
        return _experts(h2p, idx, gcol, wg, wu, wd, cap, x1, g2, ng[3], mod_row)

    xl = ffn(*_merge(hl, lg, yb, yc, gates, xl, g1, sh2, sc2, ng[1], ng[2], woa, wob, woc, wout, None), None)
    if need_ctx:
        ybc = _attn(cnq, cnk, cnv)
        ycc = _attn(cgq, cgk, cgv, qg=p["qn_g"], kg=p["kn_g"])
        xc = ffn(*_merge(hc, cg, ybc, ycc, cgates, xc, g1, sh2, sc2, ng[1], ng[2], woa, wob, woc, wout, ctx_row),
                 ctx_row)
    return xl, xc


def kernel(x, c, ctx, c_ctx, w_mod, b_mod, norm_g, w_in, conv_w, conv_b, lru_wa, lru_ba, lru_wx, lru_bx, lru_lam,
           na_rpb, qn_g, kn_g, w_o_a, w_o_b, w_o_c, w_out, w_router, w_gate, w_up, w_down):
    bsz, _, d = x.shape
    depth = w_mod.shape[0]
    rows = -(-(bsz + 1) // SUBLANES) * SUBLANES
    cond = jnp.zeros((rows, d), F32).at[:bsz].set(c).at[bsz].set(c_ctx)
    xl, xc = x, ctx
    for l in range(depth):
        p = {
            "norm_g": norm_g[l], "w_in": w_in[l], "conv_w": conv_w[l], "conv_b": conv_b[l],
            "lru_wa": lru_wa[l], "lru_ba": lru_ba[l], "lru_wx": lru_wx[l], "lru_bx": lru_bx[l],
            "lru_lam": lru_lam[l], "na_rpb": na_rpb[l], "qn_g": qn_g[l], "kn_g": kn_g[l],
            "w_o_a": w_o_a[l], "w_o_b": w_o_b[l], "w_o_c": w_o_c[l], "w_out": w_out[l],
            "w_router": w_router[l], "w_gate": w_gate[l], "w_up": w_up[l], "w_down": w_down[l],
        }
        m = _adaln(cond, w_mod[l], b_mod[l])
        mods = [m[:, i * d:(i + 1) * d].reshape(rows, 1, d) for i in range(6)]
        xl, xc = _layer(xl, xc, mods, p, l < depth - 1)
    return xl
```
